```python
import math
import jax, jax.numpy as jnp
from jax import lax
import numpy as np


D_MODEL = 1024
BATCH = 4
SEQ = 8192
DEPTH = 4

N_MIXERS = 3
GRID_W = 64
EPS = 1e-6
N_HEADS = 16
HEAD_DIM = D_MODEL // N_HEADS
N_KV_HEADS = 4
Q_BLOCK = 128
ROPE_THETA = 10000.0
HY_ORDER = 2
HY_EMB_DIM = 33
HY_FILTER_WIDTH = 64
HY_FAST_DECAY = 0.3
HY_SLOW_DECAY = 1.5
HY_TARGET = 1e-2
S5_GROUP = 16
S5_GROUPS = D_MODEL // S5_GROUP
S5_STATE = 64
S5_DT_MIN = 1e-3
S5_DT_MAX = 1e-1
D_FF = 2816

kernel_name = 'hybrid_attn_hyena_s5_encoder'


def rmsnorm(x, g):
    xf = x.astype(jnp.float32)
    y = xf * lax.rsqrt(jnp.mean(xf * xf, axis=-1, keepdims=True) + EPS)
    return (y * g.astype(jnp.float32)).astype(x.dtype)


def dwconv3(x, w, b):
    L = x.shape[1]
    xp = jnp.pad(x, ((0, 0), (1, 1), (0, 0)))
    return xp[:, :L] * w[0] + xp[:, 1:L + 1] * w[1] + xp[:, 2:] * w[2] + b


def axial_rope_tables(L):
    rows = L // GRID_W
    n_freq = HEAD_DIM // 4
    inv = 1.0 / (ROPE_THETA ** (jnp.arange(n_freq, dtype=jnp.float32) / n_freq))
    r = jnp.arange(rows, dtype=jnp.float32)
    col = jnp.arange(GRID_W, dtype=jnp.float32)
    ang_r = jnp.broadcast_to(r[:, None, None] * inv, (rows, GRID_W, n_freq))
    ang_c = jnp.broadcast_to(col[None, :, None] * inv, (rows, GRID_W, n_freq))
    ang = jnp.concatenate([ang_r, ang_c], axis=-1).reshape(L, 2 * n_freq)
    return jnp.cos(ang), jnp.sin(ang)


def apply_rope(x, cos, sin):
    xf = x.astype(jnp.float32).reshape(x.shape[:-1] + (HEAD_DIM // 2, 2))
    x0, x1 = xf[..., 0], xf[..., 1]
    c = cos[None, :, None, :]
    s = sin[None, :, None, :]
    out = jnp.stack([x0 * c - x1 * s, x0 * s + x1 * c], axis=-1).reshape(x.shape)
    return out.astype(x.dtype)


def attention_mixer(h, w_qkv, w_o, q_gain, k_gain, cos, sin):
    Bsz, L, _ = h.shape
    G = N_HEADS // N_KV_HEADS
    qkv = h @ w_qkv
    q, k, v = jnp.split(qkv, [N_HEADS * HEAD_DIM, (N_HEADS + N_KV_HEADS) * HEAD_DIM], axis=-1)
    q = q.reshape(Bsz, L, N_HEADS, HEAD_DIM)
    k = k.reshape(Bsz, L, N_KV_HEADS, HEAD_DIM)
    v = v.reshape(Bsz, L, N_KV_HEADS, HEAD_DIM)
    q = apply_rope(rmsnorm(q, q_gain), cos, sin)
    k = apply_rope(rmsnorm(k, k_gain), cos, sin)
    nb = L // Q_BLOCK
    qb = q.reshape(Bsz, nb, Q_BLOCK, N_KV_HEADS, G, HEAD_DIM).transpose(1, 0, 2, 3, 4, 5)
    scale = HEAD_DIM ** -0.5

    def block(q_blk):
        s = jnp.einsum('bqkgd,bskd->bkgqs', q_blk, k, preferred_element_type=jnp.float32) * scale
        p = jax.nn.softmax(s, axis=-1).astype(v.dtype)
        return jnp.einsum('bkgqs,bskd->bqkgd', p, v)

    o = lax.map(block, qb)
    o = o.transpose(1, 0, 2, 3, 4, 5).reshape(Bsz, L, N_HEADS * HEAD_DIM)
    return o @ w_o


def hyena_pos_features(L):
    t = jnp.linspace(0.0, 1.0, L, dtype=jnp.float32)[:, None]
    bands = (HY_EMB_DIM - 1) // 2
    w = 2.0 * math.pi * jnp.arange(L, dtype=jnp.float32) / L
    f = jnp.linspace(1e-4, bands - 1, bands, dtype=jnp.float32)
    ang = w[:, None] * f[None, :]
    z = jnp.concatenate([t, jnp.cos(ang), -jnp.sin(ang)], axis=-1)
    deltas = jnp.abs(jnp.linspace(math.log(HY_TARGET) / HY_SLOW_DECAY,
                                  math.log(HY_TARGET) / HY_FAST_DECAY, D_MODEL, dtype=jnp.float32))
    decay = jnp.exp(-t * deltas[None, :])
    return z, decay


def hyena_mixer(h, w_in, conv_w, conv_b, f_w1, f_b1, f_w2, f_b2, f_w3, f_freq, skip, w_out, z, decay):
    Bsz, L, D = h.shape
    u = dwconv3(h @ w_in, conv_w, conv_b)
    v, x1, x2 = jnp.split(u, 3, axis=-1)
    a = jnp.sin(f_freq * (z @ f_w1 + f_b1))
    a = jnp.sin(f_freq * (a @ f_w2 + f_b2))
    filt = (a @ f_w3).astype(jnp.float32).reshape(L, HY_ORDER, 2, D) * decay[:, None, None, :]
    fwd = filt[:, :, 0]
    bwd = filt[:, :, 1]
    k2 = jnp.concatenate([fwd, jnp.zeros_like(fwd[:1]), bwd[1:][::-1]], axis=0)
    k2 = k2 / jnp.sum(jnp.abs(k2), axis=0, keepdims=True)
    kf = jnp.fft.rfft(k2, axis=0)
    gates = (x1, x2)
    zc = v
    for o in range(HY_ORDER):
        zf = jnp.fft.rfft(zc.astype(jnp.float32), n=2 * L, axis=1)
        y = jnp.fft.irfft(zf * kf[None, :, o], n=2 * L, axis=1)[:, :L]
        zc = gates[o] * (y.astype(h.dtype) + skip[o] * zc)
    return zc @ w_out


def s5_direction(u, A_re, A_im, log_dt, B_re, B_im, C_re, C_im, reverse):
    L = u.shape[1]
    lam = lax.complex(jnp.minimum(A_re.astype(jnp.float32), -1e-4), A_im.astype(jnp.float32))
    dt = jnp.exp(log_dt.astype(jnp.float32))[:, None]
    lam_bar = jnp.exp(lam * dt)
    b_c = lax.complex(B_re.astype(jnp.float32), B_im.astype(jnp.float32))
    b_bar = ((lam_bar - 1.0) / lam)[..., None] * b_c
    bu = jnp.einsum('blgh,gph->blgp', u.astype(jnp.complex64), b_bar)
    a = jnp.broadcast_to(lam_bar[None, None], (1, L) + lam_bar.shape)

    def combine(e1, e2):
        a1, b1 = e1
        a2, b2 = e2
        return a2 * a1, a2 * b1 + b2

    _, xs = lax.associative_scan(combine, (a, bu), axis=1, reverse=reverse)
    c_c = lax.complex(C_re.astype(jnp.float32), C_im.astype(jnp.float32))
    return jnp.einsum('blgp,ghp->blgh', xs, c_c).real


def s5_mixer(h, A_re, A_im, log_dt, B_re, B_im, C_re, C_im, d_skip, w_glu):
    Bsz, L, D = h.shape
    hf = h.astype(jnp.float32)
    u = hf.reshape(Bsz, L, S5_GROUPS, S5_GROUP)
    y_f = s5_direction(u, A_re[0], A_im[0], log_dt[0], B_re[0], B_im[0], C_re[0], C_im[0], False)
    y_b = s5_direction(u, A_re[1], A_im[1], log_dt[1], B_re[1], B_im[1], C_re[1], C_im[1], True)
    y = (y_f + y_b).reshape(Bsz, L, D) + d_skip.astype(jnp.float32) * hf
    y = jax.nn.gelu(y.astype(h.dtype))
    g_a, g_b = jnp.split(y @ w_glu, 2, axis=-1)
    return g_a * jax.nn.sigmoid(g_b)


def conv_ffn(h, w_up, conv_w, conv_b, w_down):
    gate, val = jnp.split(h @ w_up, 2, axis=-1)
    gate = dwconv3(gate, conv_w, conv_b)
    return (jax.nn.silu(gate) * val) @ w_down


def setup_inputs(seed: int = 0) -> dict:
    key = jax.random.key(seed)
    ks = iter(jax.random.split(key, 48))
    f32 = jnp.float32

    def nrm(shape, scale):
        return jax.random.normal(next(ks), shape, f32) * scale

    n_attn = len(range(0, DEPTH, N_MIXERS))
    n_hy = len(range(1, DEPTH, N_MIXERS))
    n_s5 = len(range(2, DEPTH, N_MIXERS))
    D = D_MODEL
    G, P, Hg = S5_GROUPS, S5_STATE, S5_GROUP
    qkv_w = (N_HEADS + 2 * N_KV_HEADS) * HEAD_DIM
    return {
        'x': nrm((BATCH, SEQ, D), 1.0),
        'c': nrm((BATCH, D), 1.0),
        'ada_w': nrm((DEPTH, D, 6 * D), 0.5 * D ** -0.5),
        'ada_b': nrm((DEPTH, 6 * D), 0.02),
        'norm1_g': 1.0 + nrm((DEPTH, D), 0.02),
        'norm2_g': 1.0 + nrm((DEPTH, D), 0.02),
        'final_g': 1.0 + nrm((D,), 0.02),
        'attn_w_qkv': nrm((n_attn, D, qkv_w), D ** -0.5),
        'attn_w_o': nrm((n_attn, N_HEADS * HEAD_DIM, D), D ** -0.5),
        'attn_q_gain': 1.0 + nrm((n_attn, HEAD_DIM), 0.02),
        'attn_k_gain': 1.0 + nrm((n_attn, HEAD_DIM), 0.02),
        'hy_w_in': nrm((n_hy, D, 3 * D), D ** -0.5),
        'hy_conv_w': nrm((n_hy, 3, 3 * D), 3 ** -0.5),
        'hy_conv_b': nrm((n_hy, 3 * D), 0.02),
        'hy_f_w1': nrm((n_hy, HY_EMB_DIM, HY_FILTER_WIDTH), HY_EMB_DIM ** -0.5),
        'hy_f_b1': nrm((n_hy, HY_FILTER_WIDTH), 0.1),
        'hy_f_w2': nrm((n_hy, HY_FILTER_WIDTH, HY_FILTER_WIDTH), HY_FILTER_WIDTH ** -0.5),
        'hy_f_b2': nrm((n_hy, HY_FILTER_WIDTH), 0.1),
        'hy_f_w3': nrm((n_hy, HY_FILTER_WIDTH, HY_ORDER * 2 * D), HY_FILTER_WIDTH ** -0.5),
        'hy_f_freq': 1.0 + nrm((n_hy, HY_FILTER_WIDTH), 0.02),
        'hy_skip': nrm((n_hy, HY_ORDER, D), 1.0),
        'hy_w_out': nrm((n_hy, D, D), D ** -0.5),
        's5_A_re': -0.5 + nrm((n_s5, 2, G, P), 0.01),
        's5_A_im': math.pi * jnp.arange(P, dtype=f32) + nrm((n_s5, 2, G, P), 0.01),
        's5_log_dt': jax.random.uniform(next(ks), (n_s5, 2, G), f32, math.log(S5_DT_MIN), math.log(S5_DT_MAX)),
        's5_B_re': nrm((n_s5, 2, G, P, Hg), (2 * Hg) ** -0.5),
        's5_B_im': nrm((n_s5, 2, G, P, Hg), (2 * Hg) ** -0.5),
        's5_C_re': nrm((n_s5, 2, G, Hg, P), (2 * P) ** -0.5),
        's5_C_im': nrm((n_s5, 2, G, Hg, P), (2 * P) ** -0.5),
        's5_D': nrm((n_s5, D), 1.0),
        's5_w_glu': nrm((n_s5, D, 2 * D), D ** -0.5),
        'ffn_w_up': nrm((DEPTH, D, 2 * D_FF), D ** -0.5),
        'ffn_conv_w': nrm((DEPTH, 3, D_FF), 3 ** -0.5),
        'ffn_conv_b': nrm((DEPTH, D_FF), 0.02),
        'ffn_w_down': nrm((DEPTH, D_FF, D), D_FF ** -0.5),
    }


def reference(x, c, ada_w, ada_b, norm1_g, norm2_g, final_g,
              attn_w_qkv, attn_w_o, attn_q_gain, attn_k_gain,
              hy_w_in, hy_conv_w, hy_conv_b, hy_f_w1, hy_f_b1, hy_f_w2, hy_f_b2, hy_f_w3, hy_f_freq,
              hy_skip, hy_w_out,
              s5_A_re, s5_A_im, s5_log_dt, s5_B_re, s5_B_im, s5_C_re, s5_C_im, s5_D, s5_w_glu,
              ffn_w_up, ffn_conv_w, ffn_conv_b, ffn_w_down):
    L = x.shape[1]
    cos, sin = axial_rope_tables(L)
    z, decay = hyena_pos_features(L)
    c_act = jax.nn.silu(c)
    for i in range(DEPTH):
        m, j = i % N_MIXERS, i // N_MIXERS
        mod = (c_act @ ada_w[i] + ada_b[i])[:, None, :]
        sh1, sc1, g1, sh2, sc2, g2 = jnp.split(mod, 6, axis=-1)
        h = rmsnorm(x, norm1_g[i]) * (1.0 + sc1) + sh1
        if m == 0:
            y = attention_mixer(h, attn_w_qkv[j], attn_w_o[j], attn_q_gain[j], attn_k_gain[j], cos, sin)
        elif m == 1:
            y = hyena_mixer(h, hy_w_in[j], hy_conv_w[j], hy_conv_b[j], hy_f_w1[j], hy_f_b1[j],
                            hy_f_w2[j], hy_f_b2[j], hy_f_w3[j], hy_f_freq[j], hy_skip[j], hy_w_out[j],
                            z, decay)
        else:
            y = s5_mixer(h, s5_A_re[j], s5_A_im[j], s5_log_dt[j], s5_B_re[j], s5_B_im[j],
                         s5_C_re[j], s5_C_im[j], s5_D[j], s5_w_glu[j])
        x = x + g1 * y
        h = rmsnorm(x, norm2_g[i]) * (1.0 + sc2) + sh2
        x = x + g2 * conv_ffn(h, ffn_w_up[i], ffn_conv_w[i], ffn_conv_b[i], ffn_w_down[i])
    return rmsnorm(x, final_g)
```

```python
import functools
import math

import numpy as np
import jax
import jax.numpy as jnp
from jax import lax
from jax.experimental import pallas as pl
from jax.experimental.pallas import tpu as pltpu

D_MODEL = 1024
DEPTH = 4
N_MIXERS = 3
GRID_W = 64
EPS = 1e-6
N_HEADS = 16
HEAD_DIM = D_MODEL // N_HEADS
N_KV_HEADS = 4
KV_GROUP = N_HEADS // N_KV_HEADS
ROPE_THETA = 10000.0
HY_ORDER = 2
HY_EMB_DIM = 33
HY_FILTER_WIDTH = 64
HY_FAST_DECAY = 0.3
HY_SLOW_DECAY = 1.5
HY_TARGET = 1e-2
S5_GROUP = 16
S5_GROUPS = D_MODEL // S5_GROUP
S5_STATE = 64
D_FF = 2816

F32 = jnp.float32
BF16 = jnp.bfloat16
HIGHEST = lax.Precision.HIGHEST

LANES = 128
SUBLANES = 8
BF16_ROWS = 16
VMEM_LIMIT = 56 * 1024 * 1024

ROW_TILE = 512
HALO = BF16_ROWS
FF_CHUNK = 256
HY_CHUNK = 512
Q_TILE = 128
KV_TILE = 512
DFT_N2 = 128
DFT_LANES = 2048
S5_SLAB = 128
S5_STEPS = 128


def _params(*sem):
    return pltpu.CompilerParams(dimension_semantics=sem, vmem_limit_bytes=VMEM_LIMIT)


def _resident(shape):
    zeros = (0,) * len(shape)
    return pl.BlockSpec(shape, lambda *_: zeros, pipeline_mode=pl.Buffered(1))


def _norm_mod(x, g, scale, shift):
    y = x * lax.rsqrt(jnp.mean(x * x, axis=-1, keepdims=True) + EPS)
    return (y * g) * (1.0 + scale) + shift


def _ada_kernel(c_ref, w_ref, b_ref, o_ref):
    c = c_ref[...]
    a = c * jax.nn.sigmoid(c)
    o_ref[0] = jnp.dot(a, w_ref[0], precision=HIGHEST, preferred_element_type=F32) + b_ref[0]


def _ada_mod(c, ada_w, ada_b):
    bsz, d = c.shape
    rows = -(-bsz // SUBLANES) * SUBLANES
    cp = jnp.pad(c, ((0, rows - bsz), (0, 0)))
    tn = 1536
    out = pl.pallas_call(
        _ada_kernel,
        grid=(DEPTH, 6 * d // tn),
        in_specs=[
            pl.BlockSpec((rows, d), lambda i, j: (0, 0)),
            pl.BlockSpec((1, d, tn), lambda i, j: (i, 0, j)),
            pl.BlockSpec((1, 1, tn), lambda i, j: (i, 0, j)),
        ],
        out_specs=pl.BlockSpec((1, rows, tn), lambda i, j: (i, 0, j)),
        out_shape=jax.ShapeDtypeStruct((DEPTH, rows, 6 * d), F32),
        compiler_params=_params("arbitrary", "arbitrary"),
        name="ada_mod",
    )(cp, ada_w, ada_b.reshape(DEPTH, 1, 6 * d))
    return out[:, :bsz].reshape(DEPTH, bsz, 6, d)


def _qkv_kernel(x_ref, mod_ref, g_ref, w_ref, cos_ref, sin_ref, qg_ref, kg_ref, q_ref, k_ref, v_ref):
    m = mod_ref[0]
    h = _norm_mod(x_ref[0], g_ref[...], m[1:2], m[0:1])
    r = jnp.dot(h.astype(BF16), w_ref[...], preferred_element_type=F32)
    c = cos_ref[...]
    s = sin_ref[...]
    half = HEAD_DIM // 2

    def head(xh, gain, scale):
        n = xh * lax.rsqrt(jnp.mean(xh * xh, axis=-1, keepdims=True) + EPS) * gain
        e = n[:, :half]
        o = n[:, half:]
        return jnp.concatenate([e * c - o * s, e * s + o * c], axis=-1) * scale

    qg = qg_ref[...]
    kg = kg_ref[...]
    for hh in range(N_HEADS):
        q_ref[0, hh] = head(r[:, hh * HEAD_DIM:(hh + 1) * HEAD_DIM], qg, HEAD_DIM ** -0.5).astype(BF16)
    k0 = N_HEADS * HEAD_DIM
    v0 = k0 + N_KV_HEADS * HEAD_DIM
    for kk in range(N_KV_HEADS):
        k_ref[0, kk] = head(r[:, k0 + kk * HEAD_DIM:k0 + (kk + 1) * HEAD_DIM], kg, 1.0).astype(BF16)
        v_ref[0, kk] = r[:, v0 + kk * HEAD_DIM:v0 + (kk + 1) * HEAD_DIM].astype(BF16)


def _flash_kernel(q_ref, k_ref, v_ref, o_ref, m_ref, l_ref, acc_ref, *, n_kv):
    rows = KV_GROUP * Q_TILE
    q = q_ref[0].reshape(rows, HEAD_DIM)
    m_ref[...] = jnp.full(m_ref.shape, -jnp.inf, F32)
    l_ref[...] = jnp.zeros(l_ref.shape, F32)
    acc_ref[...] = jnp.zeros(acc_ref.shape, F32)

    def body(j, carry):
        start = pl.multiple_of(j * KV_TILE, KV_TILE)
        kj = k_ref[0, 0, pl.ds(start, KV_TILE), :]
        vj = v_ref[0, 0, pl.ds(start, KV_TILE), :]
        s = lax.dot_general(q, kj, (((1,), (1,)), ((), ())), preferred_element_type=F32)
        m_prev = m_ref[...]
        m_next = jnp.maximum(m_prev, jnp.max(s, axis=-1, keepdims=True))
        alpha = jnp.exp(m_prev - m_next)
        p = jnp.exp(s - m_next)
        l_ref[...] = alpha * l_ref[...] + jnp.sum(p, axis=-1, keepdims=True)
        acc_ref[...] = alpha * acc_ref[...] + jnp.dot(p.astype(BF16), vj, preferred_element_type=F32)
        m_ref[...] = m_next
        return carry

    lax.fori_loop(0, n_kv, body, 0)
    o = acc_ref[...] / l_ref[...]
    o_ref[0] = jnp.concatenate(
        [o[g * Q_TILE:(g + 1) * Q_TILE] for g in range(KV_GROUP)], axis=-1).astype(o_ref.dtype)


def _proj_res_kernel(a_ref, w_ref, x_ref, mod_ref, o_ref, *, gate_row):
    y = jnp.dot(a_ref[0].astype(BF16), w_ref[...], preferred_element_type=F32)
    o_ref[0] = x_ref[0] + mod_ref[0][gate_row:gate_row + 1] * y


def _proj_res(a, w, x, mod_l, gate_row):
    bsz, seq, d = x.shape
    kdim = a.shape[-1]
    bm = min(ROW_TILE, seq)
    return pl.pallas_call(
        functools.partial(_proj_res_kernel, gate_row=gate_row),
        grid=(bsz, seq // bm),
        in_specs=[
            pl.BlockSpec((1, bm, kdim), lambda b, i: (b, i, 0)),
            _resident((kdim, d)),
            pl.BlockSpec((1, bm, d), lambda b, i: (b, i, 0)),
            pl.BlockSpec((1, 6, d), lambda b, i: (b, 0, 0)),
        ],
        out_specs=pl.BlockSpec((1, bm, d), lambda b, i: (b, i, 0)),
        out_shape=jax.ShapeDtypeStruct(x.shape, F32),
        compiler_params=_params("parallel", "parallel"),
        name="proj_res",
    )(a, w.astype(BF16), x, mod_l)


def _rope_tables(seq):
    rows = seq // GRID_W
    n_freq = HEAD_DIM // 4
    inv = 1.0 / (ROPE_THETA ** (jnp.arange(n_freq, dtype=F32) / n_freq))
    r = jnp.arange(rows, dtype=F32)
    col = jnp.arange(GRID_W, dtype=F32)
    ang_r = jnp.broadcast_to(r[:, None, None] * inv, (rows, GRID_W, n_freq))
    ang_c = jnp.broadcast_to(col[None, :, None] * inv, (rows, GRID_W, n_freq))
    ang = jnp.concatenate([ang_r, ang_c], axis=-1).reshape(seq, 2 * n_freq)
    return jnp.cos(ang), jnp.sin(ang)


def _attention_layer(x, mod_l, norm_g, w_qkv, w_o, q_gain, k_gain, cos, sin):
    bsz, seq, d = x.shape
    bm = min(ROW_TILE, seq)
    perm = np.concatenate([np.arange(0, HEAD_DIM, 2), np.arange(1, HEAD_DIM, 2)])
    n_qk = (N_HEADS + N_KV_HEADS) * HEAD_DIM
    cols = np.concatenate([(np.arange(N_HEADS + N_KV_HEADS)[:, None] * HEAD_DIM + perm[None, :]).reshape(-1),
                           np.arange(n_qk, n_qk + N_KV_HEADS * HEAD_DIM)])
    w = w_qkv[:, cols].astype(BF16)
    n_out = w.shape[1]
    q, k, v = pl.pallas_call(
        _qkv_kernel,
        grid=(bsz, seq // bm),
        in_specs=[
            pl.BlockSpec((1, bm, d), lambda b, i: (b, i, 0)),
            pl.BlockSpec((1, 6, d), lambda b, i: (b, 0, 0)),
            _resident((1, d)),
            _resident((d, n_out)),
            pl.BlockSpec((bm, HEAD_DIM // 2), lambda b, i: (i, 0)),
            pl.BlockSpec((bm, HEAD_DIM // 2), lambda b, i: (i, 0)),
            _resident((1, HEAD_DIM)),
            _resident((1, HEAD_DIM)),
        ],
        out_specs=[
            pl.BlockSpec((1, N_HEADS, bm, HEAD_DIM), lambda b, i: (b, 0, i, 0)),
            pl.BlockSpec((1, N_KV_HEADS, bm, HEAD_DIM), lambda b, i: (b, 0, i, 0)),
            pl.BlockSpec((1, N_KV_HEADS, bm, HEAD_DIM), lambda b, i: (b, 0, i, 0)),
        ],
        out_shape=[
            jax.ShapeDtypeStruct((bsz, N_HEADS, seq, HEAD_DIM), BF16),
            jax.ShapeDtypeStruct((bsz, N_KV_HEADS, seq, HEAD_DIM), BF16),
            jax.ShapeDtypeStruct((bsz, N_KV_HEADS, seq, HEAD_DIM), BF16),
        ],
        compiler_params=_params("parallel", "parallel"),
        name="attn_qkv",
    )(x, mod_l, norm_g.reshape(1, d), w, cos, sin,
      q_gain[perm].reshape(1, HEAD_DIM), k_gain[perm].reshape(1, HEAD_DIM))

    rows = KV_GROUP * Q_TILE
    o = pl.pallas_call(
        functools.partial(_flash_kernel, n_kv=seq // KV_TILE),
        grid=(bsz, N_KV_HEADS, seq // Q_TILE),
        in_specs=[
            pl.BlockSpec((1, KV_GROUP, Q_TILE, HEAD_DIM), lambda b, g, i: (b, g, i, 0)),
            pl.BlockSpec((1, 1, seq, HEAD_DIM), lambda b, g, i: (b, g, 0, 0)),
            pl.BlockSpec((1, 1, seq, HEAD_DIM), lambda b, g, i: (b, g, 0, 0)),
        ],
        out_specs=pl.BlockSpec((1, Q_TILE, KV_GROUP * HEAD_DIM), lambda b, g, i: (b, i, g)),
        out_shape=jax.ShapeDtypeStruct((bsz, seq, N_HEADS * HEAD_DIM), BF16),
        scratch_shapes=[
            pltpu.VMEM((rows, 1), F32),
            pltpu.VMEM((rows, 1), F32),
            pltpu.VMEM((rows, HEAD_DIM), F32),
        ],
        compiler_params=_params("parallel", "parallel", "arbitrary"),
        name="attn_flash",
    )(q, k, v)
    return _proj_res(o, w_o, x, mod_l, 2)


def _halo_rows(i, bm, seq):
    t = i * bm - HALO + lax.broadcasted_iota(jnp.int32, (bm + 2 * HALO, 1), 0)
    return (t >= 0) & (t < seq)


def _fill_window(hs_ref, xp_ref, x_ref, xn_ref, g, scale, shift, bm):
    hs_ref[0:HALO] = _norm_mod(xp_ref[0], g, scale, shift).astype(BF16)
    hs_ref[HALO:HALO + bm] = _norm_mod(x_ref[0], g, scale, shift).astype(BF16)
    hs_ref[HALO + bm:] = _norm_mod(xn_ref[0], g, scale, shift).astype(BF16)


def _conv3(win_ref, cw, cb, bm):
    return (win_ref[pl.ds(HALO - 1, bm), :] * cw[0:1] + win_ref[pl.ds(HALO, bm), :] * cw[1:2]
            + win_ref[pl.ds(HALO + 1, bm), :] * cw[2:3] + cb)


def _ffn_kernel(xp_ref, x_ref, xn_ref, mod_ref, g_ref, wu_ref, cw_ref, cb_ref, wd_ref, fg_ref,
                o_ref, hs_ref, gs_ref, acc_ref, *, bm, seq, final_norm):
    m = mod_ref[0]
    _fill_window(hs_ref, xp_ref, x_ref, xn_ref, g_ref[...], m[4:5], m[3:4], bm)
    valid = _halo_rows(pl.program_id(1), bm, seq)
    acc_ref[...] = jnp.zeros(acc_ref.shape, F32)
    for j in range(D_FF // FF_CHUNK):
        c0 = j * FF_CHUNK
        gate = jnp.dot(hs_ref[...], wu_ref[:, c0:c0 + FF_CHUNK], preferred_element_type=F32)
        gs_ref[...] = jnp.where(valid, gate, 0.0)
        gate = _conv3(gs_ref, cw_ref[:, c0:c0 + FF_CHUNK], cb_ref[:, c0:c0 + FF_CHUNK], bm)
        val = jnp.dot(hs_ref[HALO:HALO + bm], wu_ref[:, D_FF + c0:D_FF + c0 + FF_CHUNK],
                      preferred_element_type=F32)
        a = (gate * jax.nn.sigmoid(gate)) * val
        acc_ref[...] += jnp.dot(a.astype(BF16), wd_ref[c0:c0 + FF_CHUNK, :], preferred_element_type=F32)
    y = x_ref[0] + m[5:6] * acc_ref[...]
    if final_norm:
        y = y * lax.rsqrt(jnp.mean(y * y, axis=-1, keepdims=True) + EPS) * fg_ref[...]
    o_ref[0] = y


def _halo_specs(bm, seq, d):
    nh = seq // HALO
    per = bm // HALO
    return [
        pl.BlockSpec((1, HALO, d), lambda b, i: (b, jnp.maximum(i * per - 1, 0), 0)),
        pl.BlockSpec((1, bm, d), lambda b, i: (b, i, 0)),
        pl.BlockSpec((1, HALO, d), lambda b, i: (b, jnp.minimum((i + 1) * per, nh - 1), 0)),
    ]


def _conv_ffn_layer(x, mod_l, norm_g, w_up, conv_w, conv_b, w_down, final_g, final_norm):
    bsz, seq, d = x.shape
    bm = min(ROW_TILE, seq)
    return pl.pallas_call(
        functools.partial(_ffn_kernel, bm=bm, seq=seq, final_norm=final_norm),
        grid=(bsz, seq // bm),
        in_specs=_halo_specs(bm, seq, d) + [
            pl.BlockSpec((1, 6, d), lambda b, i: (b, 0, 0)),
            _resident((1, d)),
            _resident((d, 2 * D_FF)),
            _resident((3, D_FF)),
            _resident((1, D_FF)),
            _resident((D_FF, d)),
            _resident((1, d)),
        ],
        out_specs=pl.BlockSpec((1, bm, d), lambda b, i: (b, i, 0)),
        out_shape=jax.ShapeDtypeStruct(x.shape, F32),
        scratch_shapes=[
            pltpu.VMEM((bm + 2 * HALO, d), BF16),
            pltpu.VMEM((bm + 2 * HALO, FF_CHUNK), F32),
            pltpu.VMEM((bm, d), F32),
        ],
        compiler_params=_params("parallel", "parallel"),
        name="conv_ffn",
    )(x, x, x, mod_l, norm_g.reshape(1, d), w_up.astype(BF16), conv_w, conv_b.reshape(1, D_FF),
      w_down.astype(BF16), final_g.reshape(1, d))


def _hy_in_kernel(xp_ref, x_ref, xn_ref, mod_ref, g_ref, w_ref, cw_ref, cb_ref,
                  v_ref, x1_ref, x2_ref, hs_ref, us_ref, *, bm, seq):
    m = mod_ref[0]
    _fill_window(hs_ref, xp_ref, x_ref, xn_ref, g_ref[...], m[1:2], m[0:1], bm)
    valid = _halo_rows(pl.program_id(1), bm, seq)
    outs = (v_ref, x1_ref, x2_ref)
    for j in range(3 * D_MODEL // HY_CHUNK):
        c0 = j * HY_CHUNK
        u = jnp.dot(hs_ref[...], w_ref[:, c0:c0 + HY_CHUNK], preferred_element_type=F32)
        us_ref[...] = jnp.where(valid, u, 0.0)
        y = _conv3(us_ref, cw_ref[:, c0:c0 + HY_CHUNK], cb_ref[:, c0:c0 + HY_CHUNK], bm)
        off = c0 % D_MODEL
        outs[c0 // D_MODEL][0, :, off:off + HY_CHUNK] = y


def _hy_filter_kernel(z_ref, w1_ref, b1_ref, w2_ref, b2_ref, w3_ref, fr_ref, dl_ref, k_ref, n_ref,
                      *, bm, seq):
    i = pl.program_id(0)
    z = z_ref[...]
    fr = fr_ref[...]
    a = jnp.sin(fr * (jnp.dot(z, w1_ref[...], precision=HIGHEST, preferred_element_type=F32) + b1_ref[...]))
    a = jnp.sin(fr * (jnp.dot(a, w2_ref[...], precision=HIGHEST, preferred_element_type=F32) + b2_ref[...]))
    decay = jnp.exp(-z[:, 0:1] * dl_ref[...])
    t = i * bm + lax.broadcasted_iota(jnp.int32, (bm, 1), 0)
    decay = jnp.where(t == seq, 0.0, decay)

    @pl.when(i == 0)
    def _():
        n_ref[...] = jnp.zeros(n_ref.shape, F32)

    for o in range(HY_ORDER):
        f = jnp.dot(a, w3_ref[0, o], precision=HIGHEST, preferred_element_type=F32) * decay
        k_ref[o] = f
        n_ref[o:o + 1] += jnp.sum(jnp.abs(f), axis=0, keepdims=True)


def _dft_in_kernel(x_ref, w_ref, o_ref, *, n1):
    x = jnp.concatenate([x_ref[0], x_ref[1]], axis=0)
    r = jnp.dot(w_ref[...], x, precision=HIGHEST, preferred_element_type=F32)
    o_ref[0, 0] = r[:n1]
    o_ref[0, 1] = r[n1:]


def _dft_filter_in_kernel(x_ref, w_ref, o_ref, *, n1):
    r = jnp.dot(w_ref[...], x_ref[0], precision=HIGHEST, preferred_element_type=F32)
    o_ref[0, 0] = r[:n1]
    o_ref[0, 1] = r[n1:]


def _twiddle(ar, ai, twr, twi):
    return ar * twr - ai * twi, ar * twi + ai * twr


def _dft_filter_mid_kernel(a_ref, twr_ref, twi_ref, w_ref, n_ref, o_ref):
    xr, xi = _twiddle(a_ref[0, 0, 0], a_ref[0, 1, 0], twr_ref[0], twi_ref[0])
    z = jnp.dot(w_ref[...], jnp.concatenate([xr, xi], axis=0), precision=HIGHEST, preferred_element_type=F32)
    inv = 1.0 / n_ref[0]
    o_ref[0, 0, 0] = z[:DFT_N2] * inv
    o_ref[0, 1, 0] = z[DFT_N2:] * inv


def _dft_mid_kernel(a_ref, kf_ref, twr_ref, twi_ref, w_ref, wc_ref, o_ref):
    twr = twr_ref[0]
    twi = twi_ref[0]
    xr, xi = _twiddle(a_ref[0, 0, 0], a_ref[0, 1, 0], twr, twi)
    z = jnp.dot(w_ref[...], jnp.concatenate([xr, xi], axis=0), precision=HIGHEST, preferred_element_type=F32)
    yr, yi = _twiddle(z[:DFT_N2], z[DFT_N2:], kf_ref[0, 0, 0], kf_ref[0, 1, 0])
    b = jnp.dot(wc_ref[...], jnp.concatenate([yr, yi], axis=0), precision=HIGHEST, preferred_element_type=F32)
    br, bi = _twiddle(b[:DFT_N2], b[DFT_N2:], twr, -twi)
    o_ref[0, 0, 0] = br
    o_ref[0, 1, 0] = bi


def _dft_out_kernel(b_ref, w_ref, gate_ref, zc_ref, skip_ref, o_ref, *, half):
    x = jnp.concatenate([b_ref[0, 0], b_ref[0, 1]], axis=0)
    y = jnp.dot(w_ref[...], x, precision=HIGHEST, preferred_element_type=F32)
    skip = skip_ref[...]
    o_ref[0] = gate_ref[0] * (y[:half] + skip * zc_ref[0])
    o_ref[1] = gate_ref[1] * (y[half:] + skip * zc_ref[1])


def _dft_constants(seq):
    n = 2 * seq
    n2 = DFT_N2
    n1 = n // n2
    half = n1 // 2
    k1 = np.arange(n1, dtype=np.float64)
    f1 = np.exp(-2j * np.pi * np.outer(k1, k1) / n1)
    k2 = np.arange(n2, dtype=np.float64)
    f2 = np.exp(-2j * np.pi * np.outer(k2, k2) / n2)
    tw = np.exp(-2j * np.pi * np.outer(k1, k2) / n)

    def c2(m):
        return np.block([[m.real, -m.imag], [m.imag, m.real]])

    w_in = c2(f1[:, :half])
    w_fin = np.concatenate([f1.real, f1.imag], axis=0)
    w_mid = c2(f2)
    w_mid_c = c2(np.conj(f2))
    w_out = c2(np.conj(f1)[:half, :]) / n
    as32 = lambda a: jnp.asarray(a, dtype=F32)
    return dict(n1=n1, half=half, w_in=as32(w_in), w_fin=as32(w_fin), w_mid=as32(w_mid),
                w_mid_c=as32(w_mid_c), w_out=as32(w_out),
                twr=as32(tw.real).reshape(n1, n2, 1), twi=as32(tw.imag).reshape(n1, n2, 1))


def _hyena_pos_features(seq):
    t = jnp.linspace(0.0, 1.0, seq, dtype=F32)[:, None]
    bands = (HY_EMB_DIM - 1) // 2
    w = 2.0 * math.pi * jnp.arange(seq, dtype=F32) / seq
    f = jnp.linspace(1e-4, bands - 1, bands, dtype=F32)
    ang = w[:, None] * f[None, :]
    z = jnp.concatenate([t, jnp.cos(ang), -jnp.sin(ang)], axis=-1)
    deltas = jnp.abs(jnp.linspace(math.log(HY_TARGET) / HY_SLOW_DECAY,
                                  math.log(HY_TARGET) / HY_FAST_DECAY, D_MODEL, dtype=F32))
    return z, deltas


def _hyena_filter_spectrum(seq, consts, f_w1, f_b1, f_w2, f_b2, f_w3, f_freq):
    d = D_MODEL
    n = 2 * seq
    n1, n2 = consts["n1"], DFT_N2
    z, deltas = _hyena_pos_features(seq)
    z2 = jnp.concatenate([z, z[:1], z[1:][::-1]], axis=0)
    kpad = HY_FILTER_WIDTH - HY_EMB_DIM
    z2 = jnp.pad(z2, ((0, 0), (0, kpad)))
    w1 = jnp.pad(f_w1, ((0, kpad), (0, 0)))
    bm = min(ROW_TILE, seq)
    nblk = n // bm
    k2, norm = pl.pallas_call(
        functools.partial(_hy_filter_kernel, bm=bm, seq=seq),
        grid=(nblk,),
        in_specs=[
            pl.BlockSpec((bm, HY_FILTER_WIDTH), lambda i: (i, 0)),
            _resident((HY_FILTER_WIDTH, HY_FILTER_WIDTH)),
            _resident((1, HY_FILTER_WIDTH)),
            _resident((HY_FILTER_WIDTH, HY_FILTER_WIDTH)),
            _resident((1, HY_FILTER_WIDTH)),
            pl.BlockSpec((1, HY_ORDER, HY_FILTER_WIDTH, d), lambda i: ((2 * i) // nblk, 0, 0, 0)),
            _resident((1, HY_FILTER_WIDTH)),
            _resident((1, d)),
        ],
        out_specs=[
            pl.BlockSpec((HY_ORDER, bm, d), lambda i: (0, i, 0)),
            pl.BlockSpec((HY_ORDER, d), lambda i: (0, 0)),
        ],
        out_shape=[
            jax.ShapeDtypeStruct((HY_ORDER, n, d), F32),
            jax.ShapeDtypeStruct((HY_ORDER, d), F32),
        ],
        compiler_params=_params("arbitrary"),
        name="hy_filter",
    )(z2, w1, f_b1.reshape(1, -1), f_w2, f_b2.reshape(1, -1),
      f_w3.reshape(HY_FILTER_WIDTH, HY_ORDER, 2, d).transpose(2, 1, 0, 3), f_freq.reshape(1, -1),
      deltas.reshape(1, d))

    nt = DFT_LANES
    a = pl.pallas_call(
        functools.partial(_dft_filter_in_kernel, n1=n1),
        grid=(HY_ORDER, n2 * d // nt),
        in_specs=[
            pl.BlockSpec((1, n1, nt), lambda o, j: (o, 0, j)),
            _resident((2 * n1, n1)),
        ],
        out_specs=pl.BlockSpec((1, 2, n1, nt), lambda o, j: (o, 0, 0, j)),
        out_shape=jax.ShapeDtypeStruct((HY_ORDER, 2, n1, n2 * d), F32),
        compiler_params=_params("parallel", "parallel"),
        name="hy_filter_dft_in",
    )(k2.reshape(HY_ORDER, n1, n2 * d), consts["w_fin"])

    return pl.pallas_call(
        _dft_filter_mid_kernel,
        grid=(HY_ORDER, n1),
        in_specs=[
            pl.BlockSpec((1, 2, 1, n2, d), lambda o, k: (o, 0, k, 0, 0)),
            pl.BlockSpec((1, n2, 1), lambda o, k: (k, 0, 0)),
            pl.BlockSpec((1, n2, 1), lambda o, k: (k, 0, 0)),
            _resident((2 * n2, 2 * n2)),
            pl.BlockSpec((1, 1, d), lambda o, k: (o, 0, 0)),
        ],
        out_specs=pl.BlockSpec((1, 2, 1, n2, d), lambda o, k: (o, 0, k, 0, 0)),
        out_shape=jax.ShapeDtypeStruct((HY_ORDER, 2, n1, n2, d), F32),
        compiler_params=_params("parallel", "parallel"),
        name="hy_filter_dft_mid",
    )(a.reshape(HY_ORDER, 2, n1, n2, d), consts["twr"], consts["twi"], consts["w_mid"],
      norm.reshape(HY_ORDER, 1, d))


def _long_conv_gate(zc, gate, kf, order, skip, consts):
    bsz, seq, d = zc.shape
    n1, half, n2 = consts["n1"], consts["half"], DFT_N2
    pairs = bsz // 2
    nt = DFT_LANES
    zv = zc.reshape(bsz, half, n2 * d)
    a = pl.pallas_call(
        functools.partial(_dft_in_kernel, n1=n1),
        grid=(pairs, n2 * d // nt),
        in_specs=[
            pl.BlockSpec((2, half, nt), lambda p, j: (p, 0, j)),
            _resident((2 * n1, n1)),
        ],
        out_specs=pl.BlockSpec((1, 2, n1, nt), lambda p, j: (p, 0, 0, j)),
        out_shape=jax.ShapeDtypeStruct((pairs, 2, n1, n2 * d), F32),
        compiler_params=_params("parallel", "parallel"),
        name="hy_dft_in",
    )(zv, consts["w_in"])

    b = pl.pallas_call(
        _dft_mid_kernel,
        grid=(n1, pairs),
        in_specs=[
            pl.BlockSpec((1, 2, 1, n2, d), lambda k, p: (p, 0, k, 0, 0)),
            pl.BlockSpec((1, 2, 1, n2, d), lambda k, p: (order, 0, k, 0, 0)),
            pl.BlockSpec((1, n2, 1), lambda k, p: (k, 0, 0)),
            pl.BlockSpec((1, n2, 1), lambda k, p: (k, 0, 0)),
            _resident((2 * n2, 2 * n2)),
            _resident((2 * n2, 2 * n2)),
        ],
        out_specs=pl.BlockSpec((1, 2, 1, n2, d), lambda k, p: (p, 0, k, 0, 0)),
        out_shape=jax.ShapeDtypeStruct((pairs, 2, n1, n2, d), F32),
        compiler_params=_params("parallel", "arbitrary"),
        name="hy_dft_mid",
    )(a.reshape(pairs, 2, n1, n2, d), kf, consts["twr"], consts["twi"], consts["w_mid"], consts["w_mid_c"])

    out = pl.pallas_call(
        functools.partial(_dft_out_kernel, half=half),
        grid=(pairs, n2 * d // nt),
        in_specs=[
            pl.BlockSpec((1, 2, n1, nt), lambda p, j: (p, 0, 0, j)),
            _resident((n1, 2 * n1)),
            pl.BlockSpec((2, half, nt), lambda p, j: (p, 0, j)),
            pl.BlockSpec((2, half, nt), lambda p, j: (p, 0, j)),
            _resident((1, nt)),
        ],
        out_specs=pl.BlockSpec((2, half, nt), lambda p, j: (p, 0, j)),
        out_shape=jax.ShapeDtypeStruct((bsz, half, n2 * d), F32),
        compiler_params=_params("parallel", "parallel"),
        name="hy_dft_out",
    )(b.reshape(pairs, 2, n1, n2 * d), consts["w_out"], gate.reshape(bsz, half, n2 * d), zv,
      jnp.tile(skip.reshape(1, d), (1, nt // d)))
    return out.reshape(bsz, seq, d)


def _hyena_layer(x, mod_l, norm_g, w_in, conv_w, conv_b, kf, skip, w_out, consts):
    bsz, seq, d = x.shape
    bm = min(ROW_TILE, seq)
    blk = pl.BlockSpec((1, bm, d), lambda b, i: (b, i, 0))
    shp = jax.ShapeDtypeStruct(x.shape, F32)
    v, x1, x2 = pl.pallas_call(
        functools.partial(_hy_in_kernel, bm=bm, seq=seq),
        grid=(bsz, seq // bm),
        in_specs=_halo_specs(bm, seq, d) + [
            pl.BlockSpec((1, 6, d), lambda b, i: (b, 0, 0)),
            _resident((1, d)),
            _resident((d, 3 * d)),
            _resident((3, 3 * d)),
            _resident((1, 3 * d)),
        ],
        out_specs=[blk, blk, blk],
        out_shape=[shp, shp, shp],
        scratch_shapes=[
            pltpu.VMEM((bm + 2 * HALO, d), BF16),
            pltpu.VMEM((bm + 2 * HALO, HY_CHUNK), F32),
        ],
        compiler_params=_params("parallel", "parallel"),
        name="hy_in",
    )(x, x, x, mod_l, norm_g.reshape(1, d), w_in.astype(BF16), conv_w, conv_b.reshape(1, 3 * d))
    zc = v
    for o, gate in enumerate((x1, x2)):
        zc = _long_conv_gate(zc, gate, kf, o, skip[o], consts)
    return _proj_res(zc, w_out, x, mod_l, 2)


def _s5_prep_kernel(x_ref, mod_ref, g_ref, o_ref):
    m = mod_ref[0]
    o_ref[...] = _norm_mod(x_ref[0], g_ref[...], m[1:2], m[0:1])


def _s5_scan_kernel(h_ref, bm_ref, cm_ref, lam_ref, y_ref, s_ref, carry_ref, *, rows):
    d = pl.program_id(0)
    c = pl.program_id(1)
    n_slab = D_MODEL // S5_SLAB
    width = S5_SLAB // S5_GROUP * S5_STATE
    n_tiles = rows // SUBLANES

    @pl.when(c == 0)
    def _():
        carry_ref[...] = jnp.zeros(carry_ref.shape, F32)

    take_rolled = lax.broadcasted_iota(jnp.int32, (SUBLANES, width), 0) // (SUBLANES // 2) == d

    for k in range(n_slab):
        u = h_ref[:, k * S5_SLAB:(k + 1) * S5_SLAB]
        s_ref[...] = jnp.dot(u, bm_ref[0, k], precision=HIGHEST, preferred_element_type=F32)
        l1r = lam_ref[0, k, 0]
        l1i = lam_ref[0, k, 1]
        l2r = lam_ref[0, k, 2]
        l2i = lam_ref[0, k, 3]

        def step(i, carry):
            cr, ci = carry
            ii = jnp.where(d == 0, i, n_tiles - 1 - i)
            r0 = pl.multiple_of(ii * SUBLANES, SUBLANES)
            br = s_ref[pl.ds(r0, SUBLANES), 0:width]
            bi = s_ref[pl.ds(r0, SUBLANES), width:2 * width]
            sr = pltpu.roll(br, SUBLANES // 2, axis=0)
            si = pltpu.roll(bi, SUBLANES // 2, axis=0)
            vr = br + (l1r * sr - l1i * si)
            vi = bi + (l1r * si + l1i * sr)
            pr = jnp.where(take_rolled, pltpu.roll(cr, SUBLANES // 2, axis=0), cr)
            pi = jnp.where(take_rolled, pltpu.roll(ci, SUBLANES // 2, axis=0), ci)
            xr = vr + (l2r * pr - l2i * pi)
            xi = vi + (l2r * pi + l2i * pr)
            s_ref[pl.ds(r0, SUBLANES), 0:width] = xr
            s_ref[pl.ds(r0, SUBLANES), width:2 * width] = xi
            return xr, xi

        cr, ci = lax.fori_loop(0, n_tiles, step, (carry_ref[k, 0], carry_ref[k, 1]))
        carry_ref[k, 0] = cr
        carry_ref[k, 1] = ci
        y_ref[0, :, k * S5_SLAB:(k + 1) * S5_SLAB] = jnp.dot(
            s_ref[...], cm_ref[0, k], precision=HIGHEST, preferred_element_type=F32)


def _s5_glu_kernel(yf_ref, yb_ref, x_ref, mod_ref, g_ref, dsk_ref, w_ref, o_ref):
    m = mod_ref[0]
    x = x_ref[0]
    h = _norm_mod(x, g_ref[...], m[1:2], m[0:1])
    y = jax.nn.gelu(yf_ref[0] + yb_ref[0] + dsk_ref[...] * h)
    g = jnp.dot(y.astype(BF16), w_ref[...], preferred_element_type=F32)
    d = x.shape[-1]
    o_ref[0] = x + m[2:3] * (g[:, :d] * jax.nn.sigmoid(g[:, d:]))


def _s5_tables(a_re, a_im, log_dt, b_re, b_im, c_re, c_im):
    n_slab = D_MODEL // S5_SLAB
    gps = S5_SLAB // S5_GROUP
    lam = lax.complex(jnp.minimum(a_re.astype(F32), -1e-4), a_im.astype(F32))
    dt = jnp.exp(log_dt.astype(F32))[..., None]
    lam_bar = jnp.exp(lam * dt)
    b_bar = ((lam_bar - 1.0) / lam)[..., None] * lax.complex(b_re.astype(F32), b_im.astype(F32))
    eye = jnp.eye(gps, dtype=F32)
    bb = b_bar.reshape(2, n_slab, gps, S5_STATE, S5_GROUP)

    def b_block(part):
        m = jnp.einsum("dkgph,gj->dkghjp", part, eye)
        return m.reshape(2, n_slab, S5_SLAB, gps * S5_STATE)

    b_mat = jnp.concatenate([b_block(bb.real), b_block(bb.imag)], axis=-1)
    cc = lax.complex(c_re.astype(F32), c_im.astype(F32)).reshape(2, n_slab, gps, S5_GROUP, S5_STATE)

    def c_block(part):
        m = jnp.einsum("dkghp,gj->dkgpjh", part, eye)
        return m.reshape(2, n_slab, gps * S5_STATE, S5_SLAB)

    c_mat = jnp.concatenate([c_block(cc.real), -c_block(cc.imag)], axis=-2)
    lam1 = lam_bar.reshape(2, n_slab, 1, gps * S5_STATE)
    lam2 = lam1 * lam1
    zero = jnp.zeros_like(lam1)
    half = SUBLANES // 2

    def rows(lo, hi):
        return jnp.concatenate([jnp.broadcast_to(lo, lo.shape[:2] + (half,) + lo.shape[3:]),
                                jnp.broadcast_to(hi, hi.shape[:2] + (half,) + hi.shape[3:])], axis=2)

    first = jnp.stack([rows(zero, lam1)[0], rows(lam1, zero)[1]])
    second = jnp.stack([rows(lam1, lam2)[0], rows(lam2, lam1)[1]])
    lam_t = jnp.stack([first.real, first.imag, second.real, second.imag], axis=2)
    return b_mat, c_mat, lam_t


def _s5_layer(x, mod_l, norm_g, a_re, a_im, log_dt, b_re, b_im, c_re, c_im, d_skip, w_glu):
    bsz, seq, d = x.shape
    assert 2 * bsz == SUBLANES, bsz
    bm = min(ROW_TILE, seq)
    n_slab = d // S5_SLAB
    width = S5_SLAB // S5_GROUP * S5_STATE
    h2 = pl.pallas_call(
        _s5_prep_kernel,
        grid=(bsz, seq // bm),
        in_specs=[
            pl.BlockSpec((1, bm, d), lambda b, i: (b, i, 0)),
            pl.BlockSpec((1, 6, d), lambda b, i: (b, 0, 0)),
            _resident((1, d)),
        ],
        out_specs=pl.BlockSpec((bm, d), lambda b, i: (i, b)),
        out_shape=jax.ShapeDtypeStruct((seq, bsz * d), F32),
        compiler_params=_params("parallel", "parallel"),
        name="s5_prep",
    )(x, mod_l, norm_g.reshape(1, d))

    b_mat, c_mat, lam_t = _s5_tables(a_re, a_im, log_dt, b_re, b_im, c_re, c_im)
    steps = min(S5_STEPS, seq)
    rows = steps * bsz
    nc = seq // steps
    chunk = lambda dd, c: c + dd * (nc - 1 - 2 * c)
    y = pl.pallas_call(
        functools.partial(_s5_scan_kernel, rows=rows),
        grid=(2, nc),
        in_specs=[
            pl.BlockSpec((rows, d), lambda dd, c: (chunk(dd, c), 0)),
            pl.BlockSpec((1, n_slab, S5_SLAB, 2 * width), lambda dd, c: (dd, 0, 0, 0)),
            pl.BlockSpec((1, n_slab, 2 * width, S5_SLAB), lambda dd, c: (dd, 0, 0, 0)),
            pl.BlockSpec((1, n_slab, 4, SUBLANES, width), lambda dd, c: (dd, 0, 0, 0, 0)),
        ],
        out_specs=pl.BlockSpec((1, rows, d), lambda dd, c: (dd, chunk(dd, c), 0)),
        out_shape=jax.ShapeDtypeStruct((2, seq * bsz, d), F32),
        scratch_shapes=[
            pltpu.VMEM((rows, 2 * width), F32),
            pltpu.VMEM((n_slab, 2, SUBLANES, width), F32),
        ],
        compiler_params=_params("arbitrary", "arbitrary"),
        name="s5_scan",
    )(h2.reshape(seq * bsz, d), b_mat, c_mat, lam_t)

    yv = y.reshape(2, seq, bsz * d)
    return pl.pallas_call(
        _s5_glu_kernel,
        grid=(bsz, seq // bm),
        in_specs=[
            pl.BlockSpec((1, bm, d), lambda b, i: (0, i, b)),
            pl.BlockSpec((1, bm, d), lambda b, i: (1, i, b)),
            pl.BlockSpec((1, bm, d), lambda b, i: (b, i, 0)),
            pl.BlockSpec((1, 6, d), lambda b, i: (b, 0, 0)),
            _resident((1, d)),
            _resident((1, d)),
            _resident((d, 2 * d)),
        ],
        out_specs=pl.BlockSpec((1, bm, d), lambda b, i: (b, i, 0)),
        out_shape=jax.ShapeDtypeStruct(x.shape, F32),
        compiler_params=_params("parallel", "parallel"),
        name="s5_glu",
    )(yv, yv, x, mod_l, norm_g.reshape(1, d), d_skip.reshape(1, d), w_glu.astype(BF16))


def kernel(x, c, ada_w, ada_b, norm1_g, norm2_g, final_g, attn_w_qkv, attn_w_o, attn_q_gain, attn_k_gain, hy_w_in, hy_conv_w, hy_conv_b, hy_f_w1, hy_f_b1, hy_f_w2, hy_f_b2, hy_f_w3, hy_f_freq, hy_skip, hy_w_out, s5_A_re, s5_A_im, s5_log_dt, s5_B_re, s5_B_im, s5_C_re, s5_C_im, s5_D, s5_w_glu, ffn_w_up, ffn_conv_w, ffn_conv_b, ffn_w_down):
    seq = x.shape[1]
    mod = _ada_mod(c, ada_w, ada_b)
    cos, sin = _rope_tables(seq)
    consts = _dft_constants(seq)
    for i in range(DEPTH):
        m, j = i % N_MIXERS, i // N_MIXERS
        if m == 0:
            x = _attention_layer(x, mod[i], norm1_g[i], attn_w_qkv[j], attn_w_o[j],
                                 attn_q_gain[j], attn_k_gain[j], cos, sin)
        elif m == 1:
            kf = _hyena_filter_spectrum(seq, consts, hy_f_w1[j], hy_f_b1[j], hy_f_w2[j], hy_f_b2[j],
                                        hy_f_w3[j], hy_f_freq[j])
            x = _hyena_layer(x, mod[i], norm1_g[i], hy_w_in[j], hy_conv_w[j], hy_conv_b[j], kf,
                             hy_skip[j], hy_w_out[j], consts)
        else:
            x = _s5_layer(x, mod[i], norm1_g[i], s5_A_re[j], s5_A_im[j], s5_log_dt[j], s5_B_re[j],
                          s5_B_im[j], s5_C_re[j], s5_C_im[j], s5_D[j], s5_w_glu[j])
        x = _conv_ffn_layer(x, mod[i], norm2_g[i], ffn_w_up[i], ffn_conv_w[i], ffn_conv_b[i],
                            ffn_w_down[i], final_g, final_norm=(i == DEPTH - 1))
    return x
```

```python
import functools
import math

import numpy as np
import jax
import jax.numpy as jnp
from jax import lax
from jax.experimental import pallas as pl
from jax.experimental.pallas import tpu as pltpu

D_MODEL = 1024
DEPTH = 4
N_MIXERS = 3
GRID_W = 64
EPS = 1e-6
N_HEADS = 16
HEAD_DIM = D_MODEL // N_HEADS
N_KV_HEADS = 4
KV_GROUP = N_HEADS // N_KV_HEADS
ROPE_THETA = 10000.0
HY_ORDER = 2
HY_EMB_DIM = 33
HY_FILTER_WIDTH = 64
HY_FAST_DECAY = 0.3
HY_SLOW_DECAY = 1.5
HY_TARGET = 1e-2
S5_GROUP = 16
S5_GROUPS = D_MODEL // S5_GROUP
S5_STATE = 64
D_FF = 2816

F32 = jnp.float32
BF16 = jnp.bfloat16
HIGHEST = lax.Precision.HIGHEST

LANES = 128
SUBLANES = 8
BF16_ROWS = 16
VMEM_LIMIT = 56 * 1024 * 1024

ROW_TILE = 512
HALO = BF16_ROWS
FF_CHUNK = 256
HY_CHUNK = 512
Q_TILE = 128
KV_TILE = 1024
DFT_N2 = 128
DFT_LANES = 2048
S5_SLAB = 128
S5_STEPS = 128


def _params(*sem):
    return pltpu.CompilerParams(dimension_semantics=sem, vmem_limit_bytes=VMEM_LIMIT)


def _resident(shape):
    zeros = (0,) * len(shape)
    return pl.BlockSpec(shape, lambda *_: zeros, pipeline_mode=pl.Buffered(1))


def _norm_mod(x, g, scale, shift):
    y = x * lax.rsqrt(jnp.mean(x * x, axis=-1, keepdims=True) + EPS)
    return (y * g) * (1.0 + scale) + shift


def _ada_kernel(c_ref, w_ref, b_ref, o_ref):
    c = c_ref[...]
    a = c * jax.nn.sigmoid(c)
    o_ref[0] = jnp.dot(a, w_ref[0], precision=HIGHEST, preferred_element_type=F32) + b_ref[0]


def _ada_mod(c, ada_w, ada_b):
    bsz, d = c.shape
    rows = -(-bsz // SUBLANES) * SUBLANES
    cp = jnp.pad(c, ((0, rows - bsz), (0, 0)))
    tn = 1536
    out = pl.pallas_call(
        _ada_kernel,
        grid=(DEPTH, 6 * d // tn),
        in_specs=[
            pl.BlockSpec((rows, d), lambda i, j: (0, 0)),
            pl.BlockSpec((1, d, tn), lambda i, j: (i, 0, j)),
            pl.BlockSpec((1, 1, tn), lambda i, j: (i, 0, j)),
        ],
        out_specs=pl.BlockSpec((1, rows, tn), lambda i, j: (i, 0, j)),
        out_shape=jax.ShapeDtypeStruct((DEPTH, rows, 6 * d), F32),
        compiler_params=_params("arbitrary", "arbitrary"),
        name="ada_mod",
    )(cp, ada_w, ada_b.reshape(DEPTH, 1, 6 * d))
    return out[:, :bsz].reshape(DEPTH, bsz, 6, d)


def _qkv_kernel(x_ref, mod_ref, g_ref, w_ref, cos_ref, sin_ref, qg_ref, kg_ref, q_ref, k_ref, v_ref):
    m = mod_ref[0]
    h = _norm_mod(x_ref[0], g_ref[...], m[1:2], m[0:1])
    r = jnp.dot(h.astype(BF16), w_ref[...], preferred_element_type=F32)
    c = cos_ref[...]
    s = sin_ref[...]
    half = HEAD_DIM // 2

    def head(xh, gain, scale):
        n = xh * lax.rsqrt(jnp.mean(xh * xh, axis=-1, keepdims=True) + EPS) * gain
        e = n[:, :half]
        o = n[:, half:]
        return jnp.concatenate([e * c - o * s, e * s + o * c], axis=-1) * scale

    qg = qg_ref[...]
    kg = kg_ref[...]
    for hh in range(N_HEADS):
        q_ref[0, hh] = head(r[:, hh * HEAD_DIM:(hh + 1) * HEAD_DIM], qg, HEAD_DIM ** -0.5).astype(BF16)
    k0 = N_HEADS * HEAD_DIM
    v0 = k0 + N_KV_HEADS * HEAD_DIM
    for kk in range(N_KV_HEADS):
        k_ref[0, kk] = head(r[:, k0 + kk * HEAD_DIM:k0 + (kk + 1) * HEAD_DIM], kg, 1.0).astype(BF16)
        vh = r[:, v0 + kk * HEAD_DIM:v0 + (kk + 1) * HEAD_DIM]
        v_ref[0, kk] = jnp.concatenate([vh, jnp.ones_like(vh)], axis=-1).astype(BF16)


def _flash_kernel(q_ref, k_ref, v_ref, o_ref, sa_ref, sb_ref, m_ref, acc_ref, *, n_kv):
    rows = KV_GROUP * Q_TILE
    q = q_ref[0].reshape(rows, HEAD_DIM)
    m_ref[...] = jnp.full(m_ref.shape, -jnp.inf, F32)
    acc_ref[...] = jnp.zeros(acc_ref.shape, F32)

    def scores(s_ref, j):
        start = pl.multiple_of(j * KV_TILE, KV_TILE)
        kj = k_ref[0, 0, pl.ds(start, KV_TILE), :]
        s_ref[...] = lax.dot_general(q, kj, (((1,), (1,)), ((), ())), preferred_element_type=F32)

    def consume(s_ref, j):
        s = s_ref[...]
        start = pl.multiple_of(j * KV_TILE, KV_TILE)
        vj = v_ref[0, 0, pl.ds(start, KV_TILE), :]
        m_prev = m_ref[...]
        m_next = jnp.maximum(m_prev, jnp.max(s, axis=-1, keepdims=True))
        alpha = jnp.exp(m_prev - m_next)
        p = jnp.exp(s - pltpu.repeat(m_next, KV_TILE // LANES, axis=1))
        acc_ref[...] = alpha * acc_ref[...] + jnp.dot(p.astype(BF16), vj, preferred_element_type=F32)
        m_ref[...] = m_next

    scores(sa_ref, 0)

    def body(jj, carry):
        scores(sb_ref, 2 * jj + 1)
        consume(sa_ref, 2 * jj)
        scores(sa_ref, 2 * jj + 2)
        consume(sb_ref, 2 * jj + 1)
        return carry

    lax.fori_loop(0, n_kv // 2 - 1, body, 0)
    scores(sb_ref, n_kv - 1)
    consume(sa_ref, n_kv - 2)
    consume(sb_ref, n_kv - 1)
    acc = acc_ref[...]
    o = acc[:, :HEAD_DIM] / acc[:, HEAD_DIM:HEAD_DIM + 1]
    o_ref[0] = jnp.concatenate(
        [o[g * Q_TILE:(g + 1) * Q_TILE] for g in range(KV_GROUP)], axis=-1).astype(o_ref.dtype)


def _proj_res_kernel(a_ref, w_ref, x_ref, mod_ref, o_ref, *, gate_row):
    y = jnp.dot(a_ref[0].astype(BF16), w_ref[...], preferred_element_type=F32)
    o_ref[0] = x_ref[0] + mod_ref[0][gate_row:gate_row + 1] * y


def _proj_res(a, w, x, mod_l, gate_row):
    bsz, seq, d = x.shape
    kdim = a.shape[-1]
    bm = min(ROW_TILE, seq)
    return pl.pallas_call(
        functools.partial(_proj_res_kernel, gate_row=gate_row),
        grid=(bsz, seq // bm),
        in_specs=[
            pl.BlockSpec((1, bm, kdim), lambda b, i: (b, i, 0)),
            _resident((kdim, d)),
            pl.BlockSpec((1, bm, d), lambda b, i: (b, i, 0)),
            pl.BlockSpec((1, 6, d), lambda b, i: (b, 0, 0)),
        ],
        out_specs=pl.BlockSpec((1, bm, d), lambda b, i: (b, i, 0)),
        out_shape=jax.ShapeDtypeStruct(x.shape, F32),
        compiler_params=_params("parallel", "parallel"),
        name="proj_res",
    )(a, w.astype(BF16), x, mod_l)


def _rope_tables(seq):
    rows = seq // GRID_W
    n_freq = HEAD_DIM // 4
    inv = 1.0 / (ROPE_THETA ** (jnp.arange(n_freq, dtype=F32) / n_freq))
    r = jnp.arange(rows, dtype=F32)
    col = jnp.arange(GRID_W, dtype=F32)
    ang_r = jnp.broadcast_to(r[:, None, None] * inv, (rows, GRID_W, n_freq))
    ang_c = jnp.broadcast_to(col[None, :, None] * inv, (rows, GRID_W, n_freq))
    ang = jnp.concatenate([ang_r, ang_c], axis=-1).reshape(seq, 2 * n_freq)
    return jnp.cos(ang), jnp.sin(ang)


def _attention_layer(x, mod_l, norm_g, w_qkv, w_o, q_gain, k_gain, cos, sin):
    bsz, seq, d = x.shape
    bm = min(ROW_TILE, seq)
    perm = np.concatenate([np.arange(0, HEAD_DIM, 2), np.arange(1, HEAD_DIM, 2)])
    n_qk = (N_HEADS + N_KV_HEADS) * HEAD_DIM
    cols = np.concatenate([(np.arange(N_HEADS + N_KV_HEADS)[:, None] * HEAD_DIM + perm[None, :]).reshape(-1),
                           np.arange(n_qk, n_qk + N_KV_HEADS * HEAD_DIM)])
    w = w_qkv[:, cols].astype(BF16)
    n_out = w.shape[1]
    q, k, v = pl.pallas_call(
        _qkv_kernel,
        grid=(bsz, seq // bm),
        in_specs=[
            pl.BlockSpec((1, bm, d), lambda b, i: (b, i, 0)),
            pl.BlockSpec((1, 6, d), lambda b, i: (b, 0, 0)),
            _resident((1, d)),
            _resident((d, n_out)),
            pl.BlockSpec((bm, HEAD_DIM // 2), lambda b, i: (i, 0)),
            pl.BlockSpec((bm, HEAD_DIM // 2), lambda b, i: (i, 0)),
            _resident((1, HEAD_DIM)),
            _resident((1, HEAD_DIM)),
        ],
        out_specs=[
            pl.BlockSpec((1, N_HEADS, bm, HEAD_DIM), lambda b, i: (b, 0, i, 0)),
            pl.BlockSpec((1, N_KV_HEADS, bm, HEAD_DIM), lambda b, i: (b, 0, i, 0)),
            pl.BlockSpec((1, N_KV_HEADS, bm, 2 * HEAD_DIM), lambda b, i: (b, 0, i, 0)),
        ],
        out_shape=[
            jax.ShapeDtypeStruct((bsz, N_HEADS, seq, HEAD_DIM), BF16),
            jax.ShapeDtypeStruct((bsz, N_KV_HEADS, seq, HEAD_DIM), BF16),
            jax.ShapeDtypeStruct((bsz, N_KV_HEADS, seq, 2 * HEAD_DIM), BF16),
        ],
        compiler_params=_params("parallel", "parallel"),
        name="attn_qkv",
    )(x, mod_l, norm_g.reshape(1, d), w, cos, sin,
      q_gain[perm].reshape(1, HEAD_DIM), k_gain[perm].reshape(1, HEAD_DIM))

    o = _flash_attention(q, k, v)
    return _proj_res(o, w_o, x, mod_l, 2)


def _flash_attention(q, k, v):
    bsz, _, seq, _ = q.shape
    assert (seq // KV_TILE) % 2 == 0, seq
    rows = KV_GROUP * Q_TILE
    return pl.pallas_call(
        functools.partial(_flash_kernel, n_kv=seq // KV_TILE),
        grid=(bsz, N_KV_HEADS, seq // Q_TILE),
        in_specs=[
            pl.BlockSpec((1, KV_GROUP, Q_TILE, HEAD_DIM), lambda b, g, i: (b, g, i, 0)),
            pl.BlockSpec((1, 1, seq, HEAD_DIM), lambda b, g, i: (b, g, 0, 0)),
            pl.BlockSpec((1, 1, seq, 2 * HEAD_DIM), lambda b, g, i: (b, g, 0, 0)),
        ],
        out_specs=pl.BlockSpec((1, Q_TILE, KV_GROUP * HEAD_DIM), lambda b, g, i: (b, i, g)),
        out_shape=jax.ShapeDtypeStruct((bsz, seq, N_HEADS * HEAD_DIM), BF16),
        scratch_shapes=[
            pltpu.VMEM((rows, KV_TILE), F32),
            pltpu.VMEM((rows, KV_TILE), F32),
            pltpu.VMEM((rows, LANES), F32),
            pltpu.VMEM((rows, 2 * HEAD_DIM), F32),
        ],
        compiler_params=_params("parallel", "parallel", "arbitrary"),
        name="attn_flash",
    )(q, k, v)


def _halo_rows(i, bm, seq):
    t = i * bm - HALO + lax.broadcasted_iota(jnp.int32, (bm + 2 * HALO, 1), 0)
    return (t >= 0) & (t < seq)


def _fill_window(hs_ref, xp_ref, x_ref, xn_ref, g, scale, shift, bm):
    hs_ref[0:HALO] = _norm_mod(xp_ref[0], g, scale, shift).astype(BF16)
    hs_ref[HALO:HALO + bm] = _norm_mod(x_ref[0], g, scale, shift).astype(BF16)
    hs_ref[HALO + bm:] = _norm_mod(xn_ref[0], g, scale, shift).astype(BF16)


def _conv3(win_ref, cw, cb, bm):
    return (win_ref[pl.ds(HALO - 1, bm), :] * cw[0:1] + win_ref[pl.ds(HALO, bm), :] * cw[1:2]
            + win_ref[pl.ds(HALO + 1, bm), :] * cw[2:3] + cb)


def _ffn_kernel(xp_ref, x_ref, xn_ref, mod_ref, g_ref, wu_ref, cw_ref, cb_ref, wd_ref, fg_ref,
                o_ref, hs_ref, gs_ref, acc_ref, *, bm, seq, final_norm):
    m = mod_ref[0]
    _fill_window(hs_ref, xp_ref, x_ref, xn_ref, g_ref[...], m[4:5], m[3:4], bm)
    valid = _halo_rows(pl.program_id(1), bm, seq)
    acc_ref[...] = jnp.zeros(acc_ref.shape, F32)
    for j in range(D_FF // FF_CHUNK):
        c0 = j * FF_CHUNK
        gate = jnp.dot(hs_ref[...], wu_ref[:, c0:c0 + FF_CHUNK], preferred_element_type=F32)
        gs_ref[...] = jnp.where(valid, gate, 0.0)
        gate = _conv3(gs_ref, cw_ref[:, c0:c0 + FF_CHUNK], cb_ref[:, c0:c0 + FF_CHUNK], bm)
        val = jnp.dot(hs_ref[HALO:HALO + bm], wu_ref[:, D_FF + c0:D_FF + c0 + FF_CHUNK],
                      preferred_element_type=F32)
        a = (gate * jax.nn.sigmoid(gate)) * val
        acc_ref[...] += jnp.dot(a.astype(BF16), wd_ref[c0:c0 + FF_CHUNK, :], preferred_element_type=F32)
    y = x_ref[0] + m[5:6] * acc_ref[...]
    if final_norm:
        y = y * lax.rsqrt(jnp.mean(y * y, axis=-1, keepdims=True) + EPS) * fg_ref[...]
    o_ref[0] = y


def _halo_specs(bm, seq, d):
    nh = seq // HALO
    per = bm // HALO
    return [
        pl.BlockSpec((1, HALO, d), lambda b, i: (b, jnp.maximum(i * per - 1, 0), 0)),
        pl.BlockSpec((1, bm, d), lambda b, i: (b, i, 0)),
        pl.BlockSpec((1, HALO, d), lambda b, i: (b, jnp.minimum((i + 1) * per, nh - 1), 0)),
    ]


def _conv_ffn_layer(x, mod_l, norm_g, w_up, conv_w, conv_b, w_down, final_g, final_norm):
    bsz, seq, d = x.shape
    bm = min(ROW_TILE, seq)
    return pl.pallas_call(
        functools.partial(_ffn_kernel, bm=bm, seq=seq, final_norm=final_norm),
        grid=(bsz, seq // bm),
        in_specs=_halo_specs(bm, seq, d) + [
            pl.BlockSpec((1, 6, d), lambda b, i: (b, 0, 0)),
            _resident((1, d)),
            _resident((d, 2 * D_FF)),
            _resident((3, D_FF)),
            _resident((1, D_FF)),
            _resident((D_FF, d)),
            _resident((1, d)),
        ],
        out_specs=pl.BlockSpec((1, bm, d), lambda b, i: (b, i, 0)),
        out_shape=jax.ShapeDtypeStruct(x.shape, F32),
        scratch_shapes=[
            pltpu.VMEM((bm + 2 * HALO, d), BF16),
            pltpu.VMEM((bm + 2 * HALO, FF_CHUNK), F32),
            pltpu.VMEM((bm, d), F32),
        ],
        compiler_params=_params("parallel", "parallel"),
        name="conv_ffn",
    )(x, x, x, mod_l, norm_g.reshape(1, d), w_up.astype(BF16), conv_w, conv_b.reshape(1, D_FF),
      w_down.astype(BF16), final_g.reshape(1, d))


def _hy_in_kernel(xp_ref, x_ref, xn_ref, mod_ref, g_ref, w_ref, cw_ref, cb_ref,
                  v_ref, x1_ref, x2_ref, hs_ref, us_ref, *, bm, seq):
    m = mod_ref[0]
    _fill_window(hs_ref, xp_ref, x_ref, xn_ref, g_ref[...], m[1:2], m[0:1], bm)
    valid = _halo_rows(pl.program_id(1), bm, seq)
    outs = (v_ref, x1_ref, x2_ref)
    for j in range(3 * D_MODEL // HY_CHUNK):
        c0 = j * HY_CHUNK
        u = jnp.dot(hs_ref[...], w_ref[:, c0:c0 + HY_CHUNK], preferred_element_type=F32)
        us_ref[...] = jnp.where(valid, u, 0.0)
        y = _conv3(us_ref, cw_ref[:, c0:c0 + HY_CHUNK], cb_ref[:, c0:c0 + HY_CHUNK], bm)
        off = c0 % D_MODEL
        outs[c0 // D_MODEL][0, :, off:off + HY_CHUNK] = y


def _hy_filter_kernel(z_ref, w1_ref, b1_ref, w2_ref, b2_ref, w3_ref, fr_ref, dl_ref, k_ref, n_ref,
                      *, bm, seq):
    i = pl.program_id(0)
    z = z_ref[...]
    fr = fr_ref[...]
    a = jnp.sin(fr * (jnp.dot(z, w1_ref[...], precision=HIGHEST, preferred_element_type=F32) + b1_ref[...]))
    a = jnp.sin(fr * (jnp.dot(a, w2_ref[...], precision=HIGHEST, preferred_element_type=F32) + b2_ref[...]))
    decay = jnp.exp(-z[:, 0:1] * dl_ref[...])
    t = i * bm + lax.broadcasted_iota(jnp.int32, (bm, 1), 0)
    decay = jnp.where(t == seq, 0.0, decay)

    @pl.when(i == 0)
    def _():
        n_ref[...] = jnp.zeros(n_ref.shape, F32)

    for o in range(HY_ORDER):
        f = jnp.dot(a, w3_ref[0, o], precision=HIGHEST, preferred_element_type=F32) * decay
        k_ref[o] = f
        n_ref[o:o + 1] += jnp.sum(jnp.abs(f), axis=0, keepdims=True)


def _dft_dot(w_ref, x):
    return jnp.dot(w_ref[...], x.astype(BF16), preferred_element_type=F32)


def _dft_in_kernel(x_ref, w_ref, o_ref, *, n1):
    r = _dft_dot(w_ref, jnp.concatenate([x_ref[0], x_ref[1]], axis=0))
    o_ref[0, 0] = r[:n1]
    o_ref[0, 1] = r[n1:]


def _dft_filter_in_kernel(x_ref, w_ref, o_ref, *, n1):
    r = _dft_dot(w_ref, x_ref[0])
    o_ref[0, 0] = r[:n1]
    o_ref[0, 1] = r[n1:]


def _twiddle(ar, ai, twr, twi):
    return ar * twr - ai * twi, ar * twi + ai * twr


def _dft_filter_mid_kernel(a_ref, twr_ref, twi_ref, w_ref, n_ref, o_ref):
    xr, xi = _twiddle(a_ref[0, 0, 0], a_ref[0, 1, 0], twr_ref[0], twi_ref[0])
    z = _dft_dot(w_ref, jnp.concatenate([xr, xi], axis=0))
    inv = 1.0 / n_ref[0]
    o_ref[0, 0, 0] = z[:DFT_N2] * inv
    o_ref[0, 1, 0] = z[DFT_N2:] * inv


def _dft_mid_kernel(a_ref, kf_ref, twr_ref, twi_ref, w_ref, wc_ref, o_ref):
    twr = twr_ref[0]
    twi = twi_ref[0]
    xr, xi = _twiddle(a_ref[0, 0, 0], a_ref[0, 1, 0], twr, twi)
    z = _dft_dot(w_ref, jnp.concatenate([xr, xi], axis=0))
    yr, yi = _twiddle(z[:DFT_N2], z[DFT_N2:], kf_ref[0, 0, 0], kf_ref[0, 1, 0])
    b = _dft_dot(wc_ref, jnp.concatenate([yr, yi], axis=0))
    br, bi = _twiddle(b[:DFT_N2], b[DFT_N2:], twr, -twi)
    o_ref[0, 0, 0] = br
    o_ref[0, 1, 0] = bi


def _dft_out_kernel(b_ref, w_ref, gate_ref, zc_ref, skip_ref, o_ref, *, half):
    y = _dft_dot(w_ref, jnp.concatenate([b_ref[0, 0], b_ref[0, 1]], axis=0))
    skip = skip_ref[...]
    o_ref[0] = gate_ref[0] * (y[:half] + skip * zc_ref[0])
    o_ref[1] = gate_ref[1] * (y[half:] + skip * zc_ref[1])


def _dft_constants(seq):
    n = 2 * seq
    n2 = DFT_N2
    n1 = n // n2
    half = n1 // 2
    k1 = np.arange(n1, dtype=np.float64)
    f1 = np.exp(-2j * np.pi * np.outer(k1, k1) / n1)
    k2 = np.arange(n2, dtype=np.float64)
    f2 = np.exp(-2j * np.pi * np.outer(k2, k2) / n2)
    tw = np.exp(-2j * np.pi * np.outer(k1, k2) / n)

    def c2(m):
        return np.block([[m.real, -m.imag], [m.imag, m.real]])

    w_in = c2(f1[:, :half])
    w_fin = np.concatenate([f1.real, f1.imag], axis=0)
    w_mid = c2(f2)
    w_mid_c = c2(np.conj(f2))
    w_out = c2(np.conj(f1)[:half, :]) / n
    as32 = lambda a: jnp.asarray(a, dtype=F32)
    as16 = lambda a: jnp.asarray(a, dtype=F32).astype(BF16)
    return dict(n1=n1, half=half, w_in=as16(w_in), w_fin=as16(w_fin), w_mid=as16(w_mid),
                w_mid_c=as16(w_mid_c), w_out=as16(w_out),
                twr=as32(tw.real).reshape(n1, n2, 1), twi=as32(tw.imag).reshape(n1, n2, 1))


def _hyena_pos_features(seq):
    t = jnp.linspace(0.0, 1.0, seq, dtype=F32)[:, None]
    bands = (HY_EMB_DIM - 1) // 2
    w = 2.0 * math.pi * jnp.arange(seq, dtype=F32) / seq
    f = jnp.linspace(1e-4, bands - 1, bands, dtype=F32)
    ang = w[:, None] * f[None, :]
    z = jnp.concatenate([t, jnp.cos(ang), -jnp.sin(ang)], axis=-1)
    deltas = jnp.abs(jnp.linspace(math.log(HY_TARGET) / HY_SLOW_DECAY,
                                  math.log(HY_TARGET) / HY_FAST_DECAY, D_MODEL, dtype=F32))
    return z, deltas


def _hyena_filter_spectrum(seq, consts, f_w1, f_b1, f_w2, f_b2, f_w3, f_freq):
    d = D_MODEL
    n = 2 * seq
    n1, n2 = consts["n1"], DFT_N2
    z, deltas = _hyena_pos_features(seq)
    z2 = jnp.concatenate([z, z[:1], z[1:][::-1]], axis=0)
    kpad = HY_FILTER_WIDTH - HY_EMB_DIM
    z2 = jnp.pad(z2, ((0, 0), (0, kpad)))
    w1 = jnp.pad(f_w1, ((0, kpad), (0, 0)))
    bm = min(ROW_TILE, seq)
    nblk = n // bm
    k2, norm = pl.pallas_call(
        functools.partial(_hy_filter_kernel, bm=bm, seq=seq),
        grid=(nblk,),
        in_specs=[
            pl.BlockSpec((bm, HY_FILTER_WIDTH), lambda i: (i, 0)),
            _resident((HY_FILTER_WIDTH, HY_FILTER_WIDTH)),
            _resident((1, HY_FILTER_WIDTH)),
            _resident((HY_FILTER_WIDTH, HY_FILTER_WIDTH)),
            _resident((1, HY_FILTER_WIDTH)),
            pl.BlockSpec((1, HY_ORDER, HY_FILTER_WIDTH, d), lambda i: ((2 * i) // nblk, 0, 0, 0)),
            _resident((1, HY_FILTER_WIDTH)),
            _resident((1, d)),
        ],
        out_specs=[
            pl.BlockSpec((HY_ORDER, bm, d), lambda i: (0, i, 0)),
            pl.BlockSpec((HY_ORDER, d), lambda i: (0, 0)),
        ],
        out_shape=[
            jax.ShapeDtypeStruct((HY_ORDER, n, d), F32),
            jax.ShapeDtypeStruct((HY_ORDER, d), F32),
        ],
        compiler_params=_params("arbitrary"),
        name="hy_filter",
    )(z2, w1, f_b1.reshape(1, -1), f_w2, f_b2.reshape(1, -1),
      f_w3.reshape(HY_FILTER_WIDTH, HY_ORDER, 2, d).transpose(2, 1, 0, 3), f_freq.reshape(1, -1),
      deltas.reshape(1, d))

    nt = DFT_LANES
    a = pl.pallas_call(
        functools.partial(_dft_filter_in_kernel, n1=n1),
        grid=(HY_ORDER, n2 * d // nt),
        in_specs=[
            pl.BlockSpec((1, n1, nt), lambda o, j: (o, 0, j)),
            _resident((2 * n1, n1)),
        ],
        out_specs=pl.BlockSpec((1, 2, n1, nt), lambda o, j: (o, 0, 0, j)),
        out_shape=jax.ShapeDtypeStruct((HY_ORDER, 2, n1, n2 * d), F32),
        compiler_params=_params("parallel", "parallel"),
        name="hy_filter_dft_in",
    )(k2.reshape(HY_ORDER, n1, n2 * d), consts["w_fin"])

    return pl.pallas_call(
        _dft_filter_mid_kernel,
        grid=(HY_ORDER, n1),
        in_specs=[
            pl.BlockSpec((1, 2, 1, n2, d), lambda o, k: (o, 0, k, 0, 0)),
            pl.BlockSpec((1, n2, 1), lambda o, k: (k, 0, 0)),
            pl.BlockSpec((1, n2, 1), lambda o, k: (k, 0, 0)),
            _resident((2 * n2, 2 * n2)),
            pl.BlockSpec((1, 1, d), lambda o, k: (o, 0, 0)),
        ],
        out_specs=pl.BlockSpec((1, 2, 1, n2, d), lambda o, k: (o, 0, k, 0, 0)),
        out_shape=jax.ShapeDtypeStruct((HY_ORDER, 2, n1, n2, d), F32),
        compiler_params=_params("parallel", "parallel"),
        name="hy_filter_dft_mid",
    )(a.reshape(HY_ORDER, 2, n1, n2, d), consts["twr"], consts["twi"], consts["w_mid"],
      norm.reshape(HY_ORDER, 1, d))


def _long_conv_gate(zc, gate, kf, order, skip, consts):
    bsz, seq, d = zc.shape
    n1, half, n2 = consts["n1"], consts["half"], DFT_N2
    pairs = bsz // 2
    nt = DFT_LANES
    zv = zc.reshape(bsz, half, n2 * d)
    a = pl.pallas_call(
        functools.partial(_dft_in_kernel, n1=n1),
        grid=(pairs, n2 * d // nt),
        in_specs=[
            pl.BlockSpec((2, half, nt), lambda p, j: (p, 0, j)),
            _resident((2 * n1, n1)),
        ],
        out_specs=pl.BlockSpec((1, 2, n1, nt), lambda p, j: (p, 0, 0, j)),
        out_shape=jax.ShapeDtypeStruct((pairs, 2, n1, n2 * d), F32),
        compiler_params=_params("parallel", "parallel"),
        name="hy_dft_in",
    )(zv, consts["w_in"])

    b = pl.pallas_call(
        _dft_mid_kernel,
        grid=(n1, pairs),
        in_specs=[
            pl.BlockSpec((1, 2, 1, n2, d), lambda k, p: (p, 0, k, 0, 0)),
            pl.BlockSpec((1, 2, 1, n2, d), lambda k, p: (order, 0, k, 0, 0)),
            pl.BlockSpec((1, n2, 1), lambda k, p: (k, 0, 0)),
            pl.BlockSpec((1, n2, 1), lambda k, p: (k, 0, 0)),
            _resident((2 * n2, 2 * n2)),
            _resident((2 * n2, 2 * n2)),
        ],
        out_specs=pl.BlockSpec((1, 2, 1, n2, d), lambda k, p: (p, 0, k, 0, 0)),
        out_shape=jax.ShapeDtypeStruct((pairs, 2, n1, n2, d), F32),
        compiler_params=_params("parallel", "arbitrary"),
        name="hy_dft_mid",
    )(a.reshape(pairs, 2, n1, n2, d), kf, consts["twr"], consts["twi"], consts["w_mid"], consts["w_mid_c"])

    out = pl.pallas_call(
        functools.partial(_dft_out_kernel, half=half),
        grid=(pairs, n2 * d // nt),
        in_specs=[
            pl.BlockSpec((1, 2, n1, nt), lambda p, j: (p, 0, 0, j)),
            _resident((n1, 2 * n1)),
            pl.BlockSpec((2, half, nt), lambda p, j: (p, 0, j)),
            pl.BlockSpec((2, half, nt), lambda p, j: (p, 0, j)),
            _resident((1, nt)),
        ],
        out_specs=pl.BlockSpec((2, half, nt), lambda p, j: (p, 0, j)),
        out_shape=jax.ShapeDtypeStruct((bsz, half, n2 * d), F32),
        compiler_params=_params("parallel", "parallel"),
        name="hy_dft_out",
    )(b.reshape(pairs, 2, n1, n2 * d), consts["w_out"], gate.reshape(bsz, half, n2 * d), zv,
      jnp.tile(skip.reshape(1, d), (1, nt // d)))
    return out.reshape(bsz, seq, d)


def _hyena_layer(x, mod_l, norm_g, w_in, conv_w, conv_b, kf, skip, w_out, consts):
    bsz, seq, d = x.shape
    bm = min(ROW_TILE, seq)
    blk = pl.BlockSpec((1, bm, d), lambda b, i: (b, i, 0))
    shp = jax.ShapeDtypeStruct(x.shape, F32)
    v, x1, x2 = pl.pallas_call(
        functools.partial(_hy_in_kernel, bm=bm, seq=seq),
        grid=(bsz, seq // bm),
        in_specs=_halo_specs(bm, seq, d) + [
            pl.BlockSpec((1, 6, d), lambda b, i: (b, 0, 0)),
            _resident((1, d)),
            _resident((d, 3 * d)),
            _resident((3, 3 * d)),
            _resident((1, 3 * d)),
        ],
        out_specs=[blk, blk, blk],
        out_shape=[shp, shp, shp],
        scratch_shapes=[
            pltpu.VMEM((bm + 2 * HALO, d), BF16),
            pltpu.VMEM((bm + 2 * HALO, HY_CHUNK), F32),
        ],
        compiler_params=_params("parallel", "parallel"),
        name="hy_in",
    )(x, x, x, mod_l, norm_g.reshape(1, d), w_in.astype(BF16), conv_w, conv_b.reshape(1, 3 * d))
    zc = v
    for o, gate in enumerate((x1, x2)):
        zc = _long_conv_gate(zc, gate, kf, o, skip[o], consts)
    return _proj_res(zc, w_out, x, mod_l, 2)


def _s5_prep_kernel(x_ref, mod_ref, g_ref, o_ref):
    m = mod_ref[0]
    o_ref[...] = _norm_mod(x_ref[0], g_ref[...], m[1:2], m[0:1])


def _s5_scan_kernel(h_ref, bm_ref, cm_ref, lam_ref, y_ref, s_ref, carry_ref, *, rows):
    d = pl.program_id(0)
    c = pl.program_id(1)
    n_slab = D_MODEL // S5_SLAB
    width = S5_SLAB // S5_GROUP * S5_STATE
    n_tiles = rows // SUBLANES

    @pl.when(c == 0)
    def _():
        carry_ref[...] = jnp.zeros(carry_ref.shape, F32)

    take_rolled = lax.broadcasted_iota(jnp.int32, (SUBLANES, width), 0) // (SUBLANES // 2) == d

    for k in range(n_slab):
        u = h_ref[:, k * S5_SLAB:(k + 1) * S5_SLAB]
        s_ref[...] = jnp.dot(u.astype(BF16), bm_ref[0, k], preferred_element_type=F32)
        l1r = lam_ref[0, k, 0]
        l1i = lam_ref[0, k, 1]
        l2r = lam_ref[0, k, 2]
        l2i = lam_ref[0, k, 3]

        def step(i, carry):
            cr, ci = carry
            ii = jnp.where(d == 0, i, n_tiles - 1 - i)
            r0 = pl.multiple_of(ii * SUBLANES, SUBLANES)
            br = s_ref[pl.ds(r0, SUBLANES), 0:width]
            bi = s_ref[pl.ds(r0, SUBLANES), width:2 * width]
            sr = pltpu.roll(br, SUBLANES // 2, axis=0)
            si = pltpu.roll(bi, SUBLANES // 2, axis=0)
            vr = br + (l1r * sr - l1i * si)
            vi = bi + (l1r * si + l1i * sr)
            pr = jnp.where(take_rolled, pltpu.roll(cr, SUBLANES // 2, axis=0), cr)
            pi = jnp.where(take_rolled, pltpu.roll(ci, SUBLANES // 2, axis=0), ci)
            xr = vr + (l2r * pr - l2i * pi)
            xi = vi + (l2r * pi + l2i * pr)
            s_ref[pl.ds(r0, SUBLANES), 0:width] = xr
            s_ref[pl.ds(r0, SUBLANES), width:2 * width] = xi
            return xr, xi

        cr, ci = lax.fori_loop(0, n_tiles, step, (carry_ref[k, 0], carry_ref[k, 1]))
        carry_ref[k, 0] = cr
        carry_ref[k, 1] = ci
        y_ref[0, :, k * S5_SLAB:(k + 1) * S5_SLAB] = jnp.dot(
            s_ref[...].astype(BF16), cm_ref[0, k], preferred_element_type=F32)


def _s5_glu_kernel(yf_ref, yb_ref, x_ref, mod_ref, g_ref, dsk_ref, w_ref, o_ref):
    m = mod_ref[0]
    x = x_ref[0]
    h = _norm_mod(x, g_ref[...], m[1:2], m[0:1])
    y = jax.nn.gelu(yf_ref[0] + yb_ref[0] + dsk_ref[...] * h)
    g = jnp.dot(y.astype(BF16), w_ref[...], preferred_element_type=F32)
    d = x.shape[-1]
    o_ref[0] = x + m[2:3] * (g[:, :d] * jax.nn.sigmoid(g[:, d:]))


def _s5_tables(a_re, a_im, log_dt, b_re, b_im, c_re, c_im):
    n_slab = D_MODEL // S5_SLAB
    gps = S5_SLAB // S5_GROUP
    lam = lax.complex(jnp.minimum(a_re.astype(F32), -1e-4), a_im.astype(F32))
    dt = jnp.exp(log_dt.astype(F32))[..., None]
    lam_bar = jnp.exp(lam * dt)
    b_bar = ((lam_bar - 1.0) / lam)[..., None] * lax.complex(b_re.astype(F32), b_im.astype(F32))
    eye = jnp.eye(gps, dtype=F32)
    bb = b_bar.reshape(2, n_slab, gps, S5_STATE, S5_GROUP)

    def b_block(part):
        m = jnp.einsum("dkgph,gj->dkghjp", part, eye)
        return m.reshape(2, n_slab, S5_SLAB, gps * S5_STATE)

    b_mat = jnp.concatenate([b_block(bb.real), b_block(bb.imag)], axis=-1)
    cc = lax.complex(c_re.astype(F32), c_im.astype(F32)).reshape(2, n_slab, gps, S5_GROUP, S5_STATE)

    def c_block(part):
        m = jnp.einsum("dkghp,gj->dkgpjh", part, eye)
        return m.reshape(2, n_slab, gps * S5_STATE, S5_SLAB)

    c_mat = jnp.concatenate([c_block(cc.real), -c_block(cc.imag)], axis=-2)
    lam1 = lam_bar.reshape(2, n_slab, 1, gps * S5_STATE)
    lam2 = lam1 * lam1
    zero = jnp.zeros_like(lam1)
    half = SUBLANES // 2

    def rows(lo, hi):
        return jnp.concatenate([jnp.broadcast_to(lo, lo.shape[:2] + (half,) + lo.shape[3:]),
                                jnp.broadcast_to(hi, hi.shape[:2] + (half,) + hi.shape[3:])], axis=2)

    first = jnp.stack([rows(zero, lam1)[0], rows(lam1, zero)[1]])
    second = jnp.stack([rows(lam1, lam2)[0], rows(lam2, lam1)[1]])
    lam_t = jnp.stack([first.real, first.imag, second.real, second.imag], axis=2)
    return b_mat, c_mat, lam_t


def _s5_layer(x, mod_l, norm_g, a_re, a_im, log_dt, b_re, b_im, c_re, c_im, d_skip, w_glu):
    bsz, seq, d = x.shape
    assert 2 * bsz == SUBLANES, bsz
    bm = min(ROW_TILE, seq)
    n_slab = d // S5_SLAB
    width = S5_SLAB // S5_GROUP * S5_STATE
    h2 = pl.pallas_call(
        _s5_prep_kernel,
        grid=(bsz, seq // bm),
        in_specs=[
            pl.BlockSpec((1, bm, d), lambda b, i: (b, i, 0)),
            pl.BlockSpec((1, 6, d), lambda b, i: (b, 0, 0)),
            _resident((1, d)),
        ],
        out_specs=pl.BlockSpec((bm, d), lambda b, i: (i, b)),
        out_shape=jax.ShapeDtypeStruct((seq, bsz * d), F32),
        compiler_params=_params("parallel", "parallel"),
        name="s5_prep",
    )(x, mod_l, norm_g.reshape(1, d))

    b_mat, c_mat, lam_t = _s5_tables(a_re, a_im, log_dt, b_re, b_im, c_re, c_im)
    steps = min(S5_STEPS, seq)
    rows = steps * bsz
    nc = seq // steps
    chunk = lambda dd, c: c + dd * (nc - 1 - 2 * c)
    y = pl.pallas_call(
        functools.partial(_s5_scan_kernel, rows=rows),
        grid=(2, nc),
        in_specs=[
            pl.BlockSpec((rows, d), lambda dd, c: (chunk(dd, c), 0)),
            pl.BlockSpec((1, n_slab, S5_SLAB, 2 * width), lambda dd, c: (dd, 0, 0, 0)),
            pl.BlockSpec((1, n_slab, 2 * width, S5_SLAB), lambda dd, c: (dd, 0, 0, 0)),
            pl.BlockSpec((1, n_slab, 4, SUBLANES, width), lambda dd, c: (dd, 0, 0, 0, 0)),
        ],
        out_specs=pl.BlockSpec((1, rows, d), lambda dd, c: (dd, chunk(dd, c), 0)),
        out_shape=jax.ShapeDtypeStruct((2, seq * bsz, d), F32),
        scratch_shapes=[
            pltpu.VMEM((rows, 2 * width), F32),
            pltpu.VMEM((n_slab, 2, SUBLANES, width), F32),
        ],
        compiler_params=_params("arbitrary", "arbitrary"),
        name="s5_scan",
    )(h2.reshape(seq * bsz, d), b_mat.astype(BF16), c_mat.astype(BF16), lam_t)

    yv = y.reshape(2, seq, bsz * d)
    return pl.pallas_call(
        _s5_glu_kernel,
        grid=(bsz, seq // bm),
        in_specs=[
            pl.BlockSpec((1, bm, d), lambda b, i: (0, i, b)),
            pl.BlockSpec((1, bm, d), lambda b, i: (1, i, b)),
            pl.BlockSpec((1, bm, d), lambda b, i: (b, i, 0)),
            pl.BlockSpec((1, 6, d), lambda b, i: (b, 0, 0)),
            _resident((1, d)),
            _resident((1, d)),
            _resident((d, 2 * d)),
        ],
        out_specs=pl.BlockSpec((1, bm, d), lambda b, i: (b, i, 0)),
        out_shape=jax.ShapeDtypeStruct(x.shape, F32),
        compiler_params=_params("parallel", "parallel"),
        name="s5_glu",
    )(yv, yv, x, mod_l, norm_g.reshape(1, d), d_skip.reshape(1, d), w_glu.astype(BF16))


def kernel(x, c, ada_w, ada_b, norm1_g, norm2_g, final_g, attn_w_qkv, attn_w_o, attn_q_gain, attn_k_gain, hy_w_in, hy_conv_w, hy_conv_b, hy_f_w1, hy_f_b1, hy_f_w2, hy_f_b2, hy_f_w3, hy_f_freq, hy_skip, hy_w_out, s5_A_re, s5_A_im, s5_log_dt, s5_B_re, s5_B_im, s5_C_re, s5_C_im, s5_D, s5_w_glu, ffn_w_up, ffn_conv_w, ffn_conv_b, ffn_w_down):
    seq = x.shape[1]
    mod = _ada_mod(c, ada_w, ada_b)
    cos, sin = _rope_tables(seq)
    consts = _dft_constants(seq)
    for i in range(DEPTH):
        m, j = i % N_MIXERS, i // N_MIXERS
        if m == 0:
            x = _attention_layer(x, mod[i], norm1_g[i], attn_w_qkv[j], attn_w_o[j],
                                 attn_q_gain[j], attn_k_gain[j], cos, sin)
        elif m == 1:
            kf = _hyena_filter_spectrum(seq, consts, hy_f_w1[j], hy_f_b1[j], hy_f_w2[j], hy_f_b2[j],
                                        hy_f_w3[j], hy_f_freq[j])
            x = _hyena_layer(x, mod[i], norm1_g[i], hy_w_in[j], hy_conv_w[j], hy_conv_b[j], kf,
                             hy_skip[j], hy_w_out[j], consts)
        else:
            x = _s5_layer(x, mod[i], norm1_g[i], s5_A_re[j], s5_A_im[j], s5_log_dt[j], s5_B_re[j],
                          s5_B_im[j], s5_C_re[j], s5_C_im[j], s5_D[j], s5_w_glu[j])
        x = _conv_ffn_layer(x, mod[i], norm2_g[i], ffn_w_up[i], ffn_conv_w[i], ffn_conv_b[i],
                            ffn_w_down[i], final_g, final_norm=(i == DEPTH - 1))
    return x
```

```python
import functools
import math

import numpy as np
import jax
import jax.numpy as jnp
from jax import lax
from jax.experimental import pallas as pl
from jax.experimental.pallas import tpu as pltpu

D_MODEL = 1024
DEPTH = 4
N_MIXERS = 3
GRID_W = 64
EPS = 1e-6
N_HEADS = 16
HEAD_DIM = D_MODEL // N_HEADS
N_KV_HEADS = 4
KV_GROUP = N_HEADS // N_KV_HEADS
ROPE_THETA = 10000.0
HY_ORDER = 2
HY_EMB_DIM = 33
HY_FILTER_WIDTH = 64
HY_FAST_DECAY = 0.3
HY_SLOW_DECAY = 1.5
HY_TARGET = 1e-2
S5_GROUP = 16
S5_GROUPS = D_MODEL // S5_GROUP
S5_STATE = 64
D_FF = 2816

F32 = jnp.float32
BF16 = jnp.bfloat16
HIGHEST = lax.Precision.HIGHEST

LANES = 128
SUBLANES = 8
BF16_ROWS = 16
VMEM_LIMIT = 56 * 1024 * 1024

ROW_TILE = 512
HALO = BF16_ROWS
FF_CHUNK = 256
HY_CHUNK = 512
Q_TILE = 256
KV_TILE = 1024
DFT_N2 = 128
DFT_STEP = SUBLANES
DFT_LANES = 512
S5_SLAB = 128
S5_STEPS = 128


def _params(*sem):
    return pltpu.CompilerParams(dimension_semantics=sem, vmem_limit_bytes=VMEM_LIMIT)


def _resident(shape):
    zeros = (0,) * len(shape)
    return pl.BlockSpec(shape, lambda *_: zeros, pipeline_mode=pl.Buffered(1))


def _norm_mod(x, g, scale, shift):
    y = x * lax.rsqrt(jnp.mean(x * x, axis=-1, keepdims=True) + EPS)
    return (y * g) * (1.0 + scale) + shift


def _ada_kernel(c_ref, w_ref, b_ref, o_ref):
    c = c_ref[...]
    a = c * jax.nn.sigmoid(c)
    o_ref[0] = jnp.dot(a, w_ref[0], precision=HIGHEST, preferred_element_type=F32) + b_ref[0]


def _ada_mod(c, ada_w, ada_b):
    bsz, d = c.shape
    rows = -(-bsz // SUBLANES) * SUBLANES
    cp = jnp.pad(c, ((0, rows - bsz), (0, 0)))
    tn = 1536
    out = pl.pallas_call(
        _ada_kernel,
        grid=(DEPTH, 6 * d // tn),
        in_specs=[
            pl.BlockSpec((rows, d), lambda i, j: (0, 0)),
            pl.BlockSpec((1, d, tn), lambda i, j: (i, 0, j)),
            pl.BlockSpec((1, 1, tn), lambda i, j: (i, 0, j)),
        ],
        out_specs=pl.BlockSpec((1, rows, tn), lambda i, j: (i, 0, j)),
        out_shape=jax.ShapeDtypeStruct((DEPTH, rows, 6 * d), F32),
        compiler_params=_params("arbitrary", "arbitrary"),
        name="ada_mod",
    )(cp, ada_w, ada_b.reshape(DEPTH, 1, 6 * d))
    return out[:, :bsz].reshape(DEPTH, bsz, 6, d)


def _qkv_kernel(x_ref, mod_ref, g_ref, w_ref, seg_ref, c_ref, s1_ref, s2_ref, qg_ref, kg_ref,
                q_ref, klo_ref, khi_ref, v_ref):
    m = mod_ref[0]
    h = _norm_mod(x_ref[0], g_ref[...], m[1:2], m[0:1])
    r = jnp.dot(h.astype(BF16), w_ref[...], preferred_element_type=F32)
    c = c_ref[...]
    s1 = s1_ref[...]
    s2 = s2_ref[...]
    seg = seg_ref[...]
    quarter = HEAD_DIM // 2

    def norm_rope(xr, gain):
        ss = xr * xr
        hi = ss.astype(BF16)
        lo = (ss - hi.astype(F32)).astype(BF16)
        tot = (jnp.dot(hi, seg, preferred_element_type=F32) + jnp.dot(lo, seg, preferred_element_type=F32))
        n = xr * lax.rsqrt(tot * (1.0 / HEAD_DIM) + EPS) * gain
        return (n * c + pltpu.roll(n, LANES - quarter, axis=1) * s1 + pltpu.roll(n, quarter, axis=1) * s2)

    qg = qg_ref[...]
    kg = kg_ref[...]
    n_q = N_HEADS * HEAD_DIM // LANES
    for p in range(n_q):
        q_ref[0, p] = norm_rope(r[:, p * LANES:(p + 1) * LANES], qg).astype(BF16)
    low = lax.broadcasted_iota(jnp.int32, (r.shape[0], LANES), 1) < HEAD_DIM
    for g in range(N_KV_HEADS):
        kk = norm_rope(r[:, (n_q + g) * LANES:(n_q + g + 1) * LANES], kg)
        klo_ref[0, g] = jnp.where(low, kk, 0.0).astype(BF16)
        khi_ref[0, g] = jnp.where(low, 0.0, kk).astype(BF16)
        vv = r[:, (n_q + N_KV_HEADS + g) * LANES:(n_q + N_KV_HEADS + g + 1) * LANES]
        v_ref[0, g] = jnp.where(low, vv, 1.0).astype(BF16)


def _flash_kernel(q_ref, qn_ref, klo_ref, khi_ref, v_ref, o_ref, sa_ref, sb_ref, m_ref, acc_ref, *, n_kv):
    rows = KV_GROUP * Q_TILE
    m_ref[...] = jnp.full(m_ref.shape, -jnp.inf, F32)
    acc_ref[...] = jnp.zeros(acc_ref.shape, F32)

    def scores(s_ref, src_ref, j):
        start = pl.multiple_of(j * KV_TILE, KV_TILE)
        qq = src_ref[0].reshape(rows // 2, LANES)
        for half, kk_ref in enumerate((klo_ref, khi_ref)):
            kj = kk_ref[0, 0, pl.ds(start, KV_TILE), :]
            s_ref[half * (rows // 2):(half + 1) * (rows // 2)] = lax.dot_general(
                qq, kj, (((1,), (1,)), ((), ())), preferred_element_type=F32)

    def consume(s_ref, j):
        s = s_ref[...]
        start = pl.multiple_of(j * KV_TILE, KV_TILE)
        vj = v_ref[0, 0, pl.ds(start, KV_TILE), :]
        m_prev = m_ref[...]
        m_next = jnp.maximum(m_prev, jnp.max(s, axis=-1, keepdims=True))
        alpha = jnp.exp(m_prev - m_next)
        p = jnp.concatenate([jnp.exp(s[:, c * LANES:(c + 1) * LANES] - m_next)
                             for c in range(KV_TILE // LANES)], axis=1)
        acc_ref[...] = alpha * acc_ref[...] + jnp.dot(p.astype(BF16), vj, preferred_element_type=F32)
        m_ref[...] = m_next

    @pl.when(pl.program_id(2) == 0)
    def _():
        scores(sa_ref, q_ref, 0)

    def body(jj, carry):
        scores(sb_ref, q_ref, 2 * jj + 1)
        consume(sa_ref, 2 * jj)
        scores(sa_ref, q_ref, 2 * jj + 2)
        consume(sb_ref, 2 * jj + 1)
        return carry

    lax.fori_loop(0, n_kv // 2 - 1, body, 0)
    scores(sb_ref, q_ref, n_kv - 1)
    consume(sa_ref, n_kv - 2)
    scores(sa_ref, qn_ref, 0)
    consume(sb_ref, n_kv - 1)
    acc = acc_ref[...]
    o = acc[:, :HEAD_DIM] / acc[:, HEAD_DIM:HEAD_DIM + 1]
    o_ref[0] = jnp.concatenate(
        [o[g * Q_TILE:(g + 1) * Q_TILE] for g in ((j % 2) * (KV_GROUP // 2) + j // 2 for j in range(KV_GROUP))],
        axis=-1).astype(o_ref.dtype)


def _proj_res_kernel(a_ref, w_ref, x_ref, mod_ref, o_ref, *, gate_row):
    y = jnp.dot(a_ref[0].astype(BF16), w_ref[...], preferred_element_type=F32)
    o_ref[0] = x_ref[0] + mod_ref[0][gate_row:gate_row + 1] * y


def _proj_res(a, w, x, mod_l, gate_row):
    bsz, seq, d = x.shape
    kdim = a.shape[-1]
    bm = min(ROW_TILE, seq)
    return pl.pallas_call(
        functools.partial(_proj_res_kernel, gate_row=gate_row),
        grid=(bsz, seq // bm),
        in_specs=[
            pl.BlockSpec((1, bm, kdim), lambda b, i: (b, i, 0)),
            _resident((kdim, d)),
            pl.BlockSpec((1, bm, d), lambda b, i: (b, i, 0)),
            pl.BlockSpec((1, 6, d), lambda b, i: (b, 0, 0)),
        ],
        out_specs=pl.BlockSpec((1, bm, d), lambda b, i: (b, i, 0)),
        out_shape=jax.ShapeDtypeStruct(x.shape, F32),
        compiler_params=_params("parallel", "parallel"),
        name="proj_res",
    )(a, w.astype(BF16), x, mod_l)


def _rope_tables(seq):
    rows = seq // GRID_W
    n_freq = HEAD_DIM // 4
    inv = 1.0 / (ROPE_THETA ** (jnp.arange(n_freq, dtype=F32) / n_freq))
    r = jnp.arange(rows, dtype=F32)
    col = jnp.arange(GRID_W, dtype=F32)
    ang_r = jnp.broadcast_to(r[:, None, None] * inv, (rows, GRID_W, n_freq))
    ang_c = jnp.broadcast_to(col[None, :, None] * inv, (rows, GRID_W, n_freq))
    ang = jnp.concatenate([ang_r, ang_c], axis=-1).reshape(seq, 2 * n_freq)
    return jnp.cos(ang), jnp.sin(ang)


def _attention_layer(x, mod_l, norm_g, w_qkv, w_o, q_gain, k_gain, cos, sin):
    bsz, seq, d = x.shape
    bm = min(ROW_TILE, seq)
    perm = np.concatenate([np.arange(0, HEAD_DIM, 2), np.arange(1, HEAD_DIM, 2)])
    head_cols = lambda hh: hh * HEAD_DIM + perm
    k0 = N_HEADS * HEAD_DIM
    v0 = k0 + N_KV_HEADS * HEAD_DIM
    q_cols = np.concatenate([head_cols(hh) for hh in range(N_HEADS)])
    k_cols = np.concatenate([np.tile(k0 + head_cols(g), 2) for g in range(N_KV_HEADS)])
    w_v = jnp.pad(w_qkv[:, v0:].reshape(d, N_KV_HEADS, HEAD_DIM), ((0, 0), (0, 0), (0, LANES - HEAD_DIM)))
    w = jnp.concatenate([w_qkv[:, q_cols], w_qkv[:, k_cols], w_v.reshape(d, N_KV_HEADS * LANES)],
                        axis=1).astype(BF16)
    n_out = w.shape[1]
    reps = LANES // (HEAD_DIM // 2)
    zero = jnp.zeros_like(sin)
    c_t = jnp.tile(cos, (1, reps))
    s1_t = jnp.tile(jnp.concatenate([-sin, zero], axis=1), (1, reps // 2))
    s2_t = jnp.tile(jnp.concatenate([zero, sin], axis=1), (1, reps // 2))
    seg = jnp.asarray(np.kron(np.eye(LANES // HEAD_DIM), np.ones((HEAD_DIM, HEAD_DIM))), dtype=BF16)
    tab = pl.BlockSpec((bm, LANES), lambda b, i: (i, 0))
    n_pairs = N_HEADS * HEAD_DIM // LANES
    kv_spec = pl.BlockSpec((1, N_KV_HEADS, bm, LANES), lambda b, i: (b, 0, i, 0))
    kv_shape = jax.ShapeDtypeStruct((bsz, N_KV_HEADS, seq, LANES), BF16)
    q, k_lo, k_hi, v = pl.pallas_call(
        _qkv_kernel,
        grid=(bsz, seq // bm),
        in_specs=[
            pl.BlockSpec((1, bm, d), lambda b, i: (b, i, 0)),
            pl.BlockSpec((1, 6, d), lambda b, i: (b, 0, 0)),
            _resident((1, d)),
            _resident((d, n_out)),
            _resident((LANES, LANES)),
            tab,
            tab,
            tab,
            _resident((1, LANES)),
            _resident((1, LANES)),
        ],
        out_specs=[
            pl.BlockSpec((1, n_pairs, bm, LANES), lambda b, i: (b, 0, i, 0)),
            kv_spec,
            kv_spec,
            kv_spec,
        ],
        out_shape=[
            jax.ShapeDtypeStruct((bsz, n_pairs, seq, LANES), BF16),
            kv_shape,
            kv_shape,
            kv_shape,
        ],
        compiler_params=_params("parallel", "parallel"),
        name="attn_qkv",
    )(x, mod_l, norm_g.reshape(1, d), w, seg, c_t, s1_t, s2_t,
      jnp.tile(q_gain[perm] * HEAD_DIM ** -0.5, 2).reshape(1, LANES), jnp.tile(k_gain[perm], 2).reshape(1, LANES))

    o = _flash_attention(q, k_lo, k_hi, v)
    return _proj_res(o, w_o, x, mod_l, 2)


def _flash_attention(q, k_lo, k_hi, v):
    bsz, _, seq, _ = q.shape
    assert (seq // KV_TILE) % 2 == 0, seq
    rows = KV_GROUP * Q_TILE
    n_q = seq // Q_TILE
    kv_spec = pl.BlockSpec((1, 1, seq, LANES), lambda b, g, i: (b, g, 0, 0))
    return pl.pallas_call(
        functools.partial(_flash_kernel, n_kv=seq // KV_TILE),
        grid=(bsz, N_KV_HEADS, n_q),
        in_specs=[
            pl.BlockSpec((1, KV_GROUP // 2, Q_TILE, LANES), lambda b, g, i: (b, g, i, 0)),
            pl.BlockSpec((1, KV_GROUP // 2, Q_TILE, LANES), lambda b, g, i: (b, g, jnp.minimum(i + 1, n_q - 1), 0)),
            kv_spec,
            kv_spec,
            kv_spec,
        ],
        out_specs=pl.BlockSpec((1, Q_TILE, KV_GROUP * HEAD_DIM), lambda b, g, i: (b, i, g)),
        out_shape=jax.ShapeDtypeStruct((bsz, seq, N_HEADS * HEAD_DIM), BF16),
        scratch_shapes=[
            pltpu.VMEM((rows, KV_TILE), F32),
            pltpu.VMEM((rows, KV_TILE), F32),
            pltpu.VMEM((rows, LANES), F32),
            pltpu.VMEM((rows, 2 * HEAD_DIM), F32),
        ],
        compiler_params=_params("parallel", "parallel", "arbitrary"),
        name="attn_flash",
    )(q, q, k_lo, k_hi, v)


def _halo_rows(i, bm, seq):
    t = i * bm - HALO + lax.broadcasted_iota(jnp.int32, (bm + 2 * HALO, 1), 0)
    return (t >= 0) & (t < seq)


def _fill_window(hs_ref, xp_ref, x_ref, xn_ref, g, scale, shift, bm):
    hs_ref[0:HALO] = _norm_mod(xp_ref[0], g, scale, shift).astype(BF16)
    hs_ref[HALO:HALO + bm] = _norm_mod(x_ref[0], g, scale, shift).astype(BF16)
    hs_ref[HALO + bm:] = _norm_mod(xn_ref[0], g, scale, shift).astype(BF16)


def _conv3(win_ref, cw, cb, bm):
    return (win_ref[pl.ds(HALO - 1, bm), :] * cw[0:1] + win_ref[pl.ds(HALO, bm), :] * cw[1:2]
            + win_ref[pl.ds(HALO + 1, bm), :] * cw[2:3] + cb)


def _ffn_kernel(xp_ref, x_ref, xn_ref, mod_ref, g_ref, wu_ref, cw_ref, cb_ref, wd_ref, fg_ref,
                o_ref, hs_ref, gs_ref, acc_ref, *, bm, seq, final_norm):
    m = mod_ref[0]
    _fill_window(hs_ref, xp_ref, x_ref, xn_ref, g_ref[...], m[4:5], m[3:4], bm)
    valid = _halo_rows(pl.program_id(1), bm, seq)
    acc_ref[...] = jnp.zeros(acc_ref.shape, F32)
    for j in range(D_FF // FF_CHUNK):
        c0 = j * FF_CHUNK
        gate = jnp.dot(hs_ref[...], wu_ref[:, c0:c0 + FF_CHUNK], preferred_element_type=F32)
        gs_ref[...] = jnp.where(valid, gate, 0.0)
        gate = _conv3(gs_ref, cw_ref[:, c0:c0 + FF_CHUNK], cb_ref[:, c0:c0 + FF_CHUNK], bm)
        val = jnp.dot(hs_ref[HALO:HALO + bm], wu_ref[:, D_FF + c0:D_FF + c0 + FF_CHUNK],
                      preferred_element_type=F32)
        a = (gate * jax.nn.sigmoid(gate)) * val
        acc_ref[...] += jnp.dot(a.astype(BF16), wd_ref[c0:c0 + FF_CHUNK, :], preferred_element_type=F32)
    y = x_ref[0] + m[5:6] * acc_ref[...]
    if final_norm:
        y = y * lax.rsqrt(jnp.mean(y * y, axis=-1, keepdims=True) + EPS) * fg_ref[...]
    o_ref[0] = y


def _halo_specs(bm, seq, d):
    nh = seq // HALO
    per = bm // HALO
    return [
        pl.BlockSpec((1, HALO, d), lambda b, i: (b, jnp.maximum(i * per - 1, 0), 0)),
        pl.BlockSpec((1, bm, d), lambda b, i: (b, i, 0)),
        pl.BlockSpec((1, HALO, d), lambda b, i: (b, jnp.minimum((i + 1) * per, nh - 1), 0)),
    ]


def _conv_ffn_layer(x, mod_l, norm_g, w_up, conv_w, conv_b, w_down, final_g, final_norm):
    bsz, seq, d = x.shape
    bm = min(ROW_TILE, seq)
    return pl.pallas_call(
        functools.partial(_ffn_kernel, bm=bm, seq=seq, final_norm=final_norm),
        grid=(bsz, seq // bm),
        in_specs=_halo_specs(bm, seq, d) + [
            pl.BlockSpec((1, 6, d), lambda b, i: (b, 0, 0)),
            _resident((1, d)),
            _resident((d, 2 * D_FF)),
            _resident((3, D_FF)),
            _resident((1, D_FF)),
            _resident((D_FF, d)),
            _resident((1, d)),
        ],
        out_specs=pl.BlockSpec((1, bm, d), lambda b, i: (b, i, 0)),
        out_shape=jax.ShapeDtypeStruct(x.shape, F32),
        scratch_shapes=[
            pltpu.VMEM((bm + 2 * HALO, d), BF16),
            pltpu.VMEM((bm + 2 * HALO, FF_CHUNK), F32),
            pltpu.VMEM((bm, d), F32),
        ],
        compiler_params=_params("parallel", "parallel"),
        name="conv_ffn",
    )(x, x, x, mod_l, norm_g.reshape(1, d), w_up.astype(BF16), conv_w, conv_b.reshape(1, D_FF),
      w_down.astype(BF16), final_g.reshape(1, d))


def _hy_in_kernel(xp_ref, x_ref, xn_ref, mod_ref, g_ref, w_ref, cw_ref, cb_ref,
                  v_ref, x1_ref, x2_ref, hs_ref, us_ref, *, bm, seq):
    m = mod_ref[0]
    _fill_window(hs_ref, xp_ref, x_ref, xn_ref, g_ref[...], m[1:2], m[0:1], bm)
    valid = _halo_rows(pl.program_id(1), bm, seq)
    outs = (v_ref, x1_ref, x2_ref)
    for j in range(3 * D_MODEL // HY_CHUNK):
        c0 = j * HY_CHUNK
        u = jnp.dot(hs_ref[...], w_ref[:, c0:c0 + HY_CHUNK], preferred_element_type=F32)
        us_ref[...] = jnp.where(valid, u, 0.0)
        y = _conv3(us_ref, cw_ref[:, c0:c0 + HY_CHUNK], cb_ref[:, c0:c0 + HY_CHUNK], bm)
        off = c0 % D_MODEL
        outs[c0 // D_MODEL][0, :, off:off + HY_CHUNK] = y


def _hy_filter_kernel(z_ref, w1_ref, b1_ref, w2_ref, b2_ref, w3_ref, fr_ref, dl_ref, k_ref, n_ref,
                      *, bm, seq):
    i = pl.program_id(0)
    z = z_ref[...]
    fr = fr_ref[...]
    a = jnp.sin(fr * (jnp.dot(z, w1_ref[...], precision=HIGHEST, preferred_element_type=F32) + b1_ref[...]))
    a = jnp.sin(fr * (jnp.dot(a, w2_ref[...], precision=HIGHEST, preferred_element_type=F32) + b2_ref[...]))
    decay = jnp.exp(-z[:, 0:1] * dl_ref[...])
    t = i * bm + lax.broadcasted_iota(jnp.int32, (bm, 1), 0)
    decay = jnp.where(t == seq, 0.0, decay)

    @pl.when(i == 0)
    def _():
        n_ref[...] = jnp.zeros(n_ref.shape, F32)

    for o in range(HY_ORDER):
        f = jnp.dot(a, w3_ref[0, o], precision=HIGHEST, preferred_element_type=F32) * decay
        k_ref[o] = f
        n_ref[o:o + 1] += jnp.sum(jnp.abs(f), axis=0, keepdims=True)


def _dft_dot(w_ref, x):
    return jnp.dot(w_ref[...], x.astype(BF16), preferred_element_type=F32)


def _dft_in_kernel(x_ref, w_ref, o_ref, *, n1):
    for s in range(DFT_STEP):
        r = _dft_dot(w_ref, jnp.concatenate([x_ref[0, :, s, :], x_ref[1, :, s, :]], axis=0))
        o_ref[0, s, 0] = r[:n1]
        o_ref[0, s, 1] = r[n1:]


def _dft_filter_in_kernel(x_ref, w_ref, o_ref, *, n1):
    for s in range(DFT_STEP):
        r = _dft_dot(w_ref, jnp.concatenate([x_ref[0, :n1 // 2, s, :], x_ref[0, n1 // 2:, s, :]], axis=0))
        o_ref[0, s, 0] = r[:n1]
        o_ref[0, s, 1] = r[n1:]


def _twiddle(ar, ai, twr, twi):
    return ar * twr - ai * twi, ar * twi + ai * twr


def _dft_filter_mid_kernel(a_ref, twr_ref, twi_ref, w_ref, n_ref, o_ref):
    inv = 1.0 / n_ref[0]
    for j in range(DFT_STEP):
        xr, xi = _twiddle(a_ref[0, :, 0, j, :], a_ref[0, :, 1, j, :], twr_ref[j], twi_ref[j])
        z = _dft_dot(w_ref, jnp.concatenate([xr, xi], axis=0))
        o_ref[0, 0, j] = z[:DFT_N2] * inv
        o_ref[0, 1, j] = z[DFT_N2:] * inv


def _dft_mid_kernel(a_ref, kf_ref, twr_ref, twi_ref, w_ref, wc_ref, o_ref):
    for j in range(DFT_STEP):
        twr = twr_ref[j]
        twi = twi_ref[j]
        xr, xi = _twiddle(a_ref[0, :, 0, j, :], a_ref[0, :, 1, j, :], twr, twi)
        z = _dft_dot(w_ref, jnp.concatenate([xr, xi], axis=0))
        yr, yi = _twiddle(z[:DFT_N2], z[DFT_N2:], kf_ref[0, 0, j], kf_ref[0, 1, j])
        b = _dft_dot(wc_ref, jnp.concatenate([yr, yi], axis=0))
        br, bi = _twiddle(b[:DFT_N2], b[DFT_N2:], twr, -twi)
        o_ref[0, j, 0] = br
        o_ref[0, j, 1] = bi


def _dft_out_kernel(b_ref, w_ref, gate_ref, zc_ref, skip_ref, o_ref, *, half):
    skip = skip_ref[...]
    for s in range(DFT_STEP):
        y = _dft_dot(w_ref, jnp.concatenate([b_ref[0, :, 0, s, :], b_ref[0, :, 1, s, :]], axis=0))
        o_ref[0, :, s, :] = gate_ref[0, :, s, :] * (y[:half] + skip * zc_ref[0, :, s, :])
        o_ref[1, :, s, :] = gate_ref[1, :, s, :] * (y[half:] + skip * zc_ref[1, :, s, :])


def _dft_constants(seq):
    n = 2 * seq
    n2 = DFT_N2
    n1 = n // n2
    half = n1 // 2
    k1 = np.arange(n1, dtype=np.float64)
    f1 = np.exp(-2j * np.pi * np.outer(k1, k1) / n1)
    k2 = np.arange(n2, dtype=np.float64)
    f2 = np.exp(-2j * np.pi * np.outer(k2, k2) / n2)
    tw = np.exp(-2j * np.pi * np.outer(k1, k2) / n)

    def c2(m):
        return np.block([[m.real, -m.imag], [m.imag, m.real]])

    w_in = c2(f1[:, :half])
    w_fin = np.concatenate([f1.real, f1.imag], axis=0)
    w_mid = c2(f2)
    w_mid_c = c2(np.conj(f2))
    w_out = c2(np.conj(f1)[:half, :]) / n
    as32 = lambda a: jnp.asarray(a, dtype=F32)
    as16 = lambda a: jnp.asarray(a, dtype=F32).astype(BF16)
    return dict(n1=n1, half=half, w_in=as16(w_in), w_fin=as16(w_fin), w_mid=as16(w_mid),
                w_mid_c=as16(w_mid_c), w_out=as16(w_out),
                twr=as32(tw.real).reshape(n1, n2, 1), twi=as32(tw.imag).reshape(n1, n2, 1))


def _hyena_pos_features(seq):
    t = jnp.linspace(0.0, 1.0, seq, dtype=F32)[:, None]
    bands = (HY_EMB_DIM - 1) // 2
    w = 2.0 * math.pi * jnp.arange(seq, dtype=F32) / seq
    f = jnp.linspace(1e-4, bands - 1, bands, dtype=F32)
    ang = w[:, None] * f[None, :]
    z = jnp.concatenate([t, jnp.cos(ang), -jnp.sin(ang)], axis=-1)
    deltas = jnp.abs(jnp.linspace(math.log(HY_TARGET) / HY_SLOW_DECAY,
                                  math.log(HY_TARGET) / HY_FAST_DECAY, D_MODEL, dtype=F32))
    return z, deltas


def _hyena_filter_spectrum(seq, consts, f_w1, f_b1, f_w2, f_b2, f_w3, f_freq):
    d = D_MODEL
    n = 2 * seq
    n1, n2 = consts["n1"], DFT_N2
    z, deltas = _hyena_pos_features(seq)
    z2 = jnp.concatenate([z, z[:1], z[1:][::-1]], axis=0)
    kpad = HY_FILTER_WIDTH - HY_EMB_DIM
    z2 = jnp.pad(z2, ((0, 0), (0, kpad)))
    w1 = jnp.pad(f_w1, ((0, kpad), (0, 0)))
    bm = min(ROW_TILE, seq)
    nblk = n // bm
    k2, norm = pl.pallas_call(
        functools.partial(_hy_filter_kernel, bm=bm, seq=seq),
        grid=(nblk,),
        in_specs=[
            pl.BlockSpec((bm, HY_FILTER_WIDTH), lambda i: (i, 0)),
            _resident((HY_FILTER_WIDTH, HY_FILTER_WIDTH)),
            _resident((1, HY_FILTER_WIDTH)),
            _resident((HY_FILTER_WIDTH, HY_FILTER_WIDTH)),
            _resident((1, HY_FILTER_WIDTH)),
            pl.BlockSpec((1, HY_ORDER, HY_FILTER_WIDTH, d), lambda i: ((2 * i) // nblk, 0, 0, 0)),
            _resident((1, HY_FILTER_WIDTH)),
            _resident((1, d)),
        ],
        out_specs=[
            pl.BlockSpec((HY_ORDER, bm, d), lambda i: (0, i, 0)),
            pl.BlockSpec((HY_ORDER, d), lambda i: (0, 0)),
        ],
        out_shape=[
            jax.ShapeDtypeStruct((HY_ORDER, n, d), F32),
            jax.ShapeDtypeStruct((HY_ORDER, d), F32),
        ],
        compiler_params=_params("arbitrary"),
        name="hy_filter",
    )(z2, w1, f_b1.reshape(1, -1), f_w2, f_b2.reshape(1, -1),
      f_w3.reshape(HY_FILTER_WIDTH, HY_ORDER, 2, d).transpose(2, 1, 0, 3), f_freq.reshape(1, -1),
      deltas.reshape(1, d))

    st, ct = DFT_STEP, DFT_LANES
    a = pl.pallas_call(
        functools.partial(_dft_filter_in_kernel, n1=n1),
        grid=(HY_ORDER, n2 // st),
        in_specs=[
            pl.BlockSpec((1, n1, st, d), lambda o, j: (o, 0, j, 0)),
            _resident((2 * n1, n1)),
        ],
        out_specs=pl.BlockSpec((1, st, 2, n1, d), lambda o, j: (o, j, 0, 0, 0)),
        out_shape=jax.ShapeDtypeStruct((HY_ORDER, n2, 2, n1, d), F32),
        compiler_params=_params("parallel", "parallel"),
        name="hy_filter_dft_in",
    )(k2.reshape(HY_ORDER, n1, n2, d), consts["w_fin"])

    return pl.pallas_call(
        _dft_filter_mid_kernel,
        grid=(HY_ORDER, n1 // st, d // ct),
        in_specs=[
            pl.BlockSpec((1, n2, 2, st, ct), lambda o, k, c: (o, 0, 0, k, c)),
            pl.BlockSpec((st, n2, 1), lambda o, k, c: (k, 0, 0)),
            pl.BlockSpec((st, n2, 1), lambda o, k, c: (k, 0, 0)),
            _resident((2 * n2, 2 * n2)),
            pl.BlockSpec((1, 1, ct), lambda o, k, c: (o, 0, c)),
        ],
        out_specs=pl.BlockSpec((1, 2, st, n2, ct), lambda o, k, c: (o, 0, k, 0, c)),
        out_shape=jax.ShapeDtypeStruct((HY_ORDER, 2, n1, n2, d), F32),
        compiler_params=_params("parallel", "parallel", "parallel"),
        name="hy_filter_dft_mid",
    )(a, consts["twr"], consts["twi"], consts["w_mid"], norm.reshape(HY_ORDER, 1, d))


def _long_conv_gate(zc, gate, kf, order, skip, consts):
    bsz, seq, d = zc.shape
    n1, half, n2 = consts["n1"], consts["half"], DFT_N2
    pairs = bsz // 2
    st, ct = DFT_STEP, DFT_LANES
    zv = zc.reshape(bsz, half, n2, d)
    a = pl.pallas_call(
        functools.partial(_dft_in_kernel, n1=n1),
        grid=(pairs, n2 // st),
        in_specs=[
            pl.BlockSpec((2, half, st, d), lambda p, j: (p, 0, j, 0)),
            _resident((2 * n1, n1)),
        ],
        out_specs=pl.BlockSpec((1, st, 2, n1, d), lambda p, j: (p, j, 0, 0, 0)),
        out_shape=jax.ShapeDtypeStruct((pairs, n2, 2, n1, d), F32),
        compiler_params=_params("parallel", "parallel"),
        name="hy_dft_in",
    )(zv, consts["w_in"])

    b = pl.pallas_call(
        _dft_mid_kernel,
        grid=(n1 // st, d // ct, pairs),
        in_specs=[
            pl.BlockSpec((1, n2, 2, st, ct), lambda k, c, p: (p, 0, 0, k, c)),
            pl.BlockSpec((1, 2, st, n2, ct), lambda k, c, p: (order, 0, k, 0, c)),
            pl.BlockSpec((st, n2, 1), lambda k, c, p: (k, 0, 0)),
            pl.BlockSpec((st, n2, 1), lambda k, c, p: (k, 0, 0)),
            _resident((2 * n2, 2 * n2)),
            _resident((2 * n2, 2 * n2)),
        ],
        out_specs=pl.BlockSpec((1, st, 2, n2, ct), lambda k, c, p: (p, k, 0, 0, c)),
        out_shape=jax.ShapeDtypeStruct((pairs, n1, 2, n2, d), F32),
        compiler_params=_params("parallel", "parallel", "arbitrary"),
        name="hy_dft_mid",
    )(a, kf, consts["twr"], consts["twi"], consts["w_mid"], consts["w_mid_c"])

    sig = pl.BlockSpec((2, half, st, ct), lambda p, j, c: (p, 0, j, c))
    out = pl.pallas_call(
        functools.partial(_dft_out_kernel, half=half),
        grid=(pairs, n2 // st, d // ct),
        in_specs=[
            pl.BlockSpec((1, n1, 2, st, ct), lambda p, j, c: (p, 0, 0, j, c)),
            _resident((n1, 2 * n1)),
            sig,
            sig,
            pl.BlockSpec((1, ct), lambda p, j, c: (0, c)),
        ],
        out_specs=sig,
        out_shape=jax.ShapeDtypeStruct((bsz, half, n2, d), F32),
        compiler_params=_params("parallel", "parallel", "parallel"),
        name="hy_dft_out",
    )(b, consts["w_out"], gate.reshape(bsz, half, n2, d), zv, skip.reshape(1, d))
    return out.reshape(bsz, seq, d)


def _hyena_layer(x, mod_l, norm_g, w_in, conv_w, conv_b, kf, skip, w_out, consts):
    bsz, seq, d = x.shape
    bm = min(ROW_TILE, seq)
    blk = pl.BlockSpec((1, bm, d), lambda b, i: (b, i, 0))
    shp = jax.ShapeDtypeStruct(x.shape, F32)
    v, x1, x2 = pl.pallas_call(
        functools.partial(_hy_in_kernel, bm=bm, seq=seq),
        grid=(bsz, seq // bm),
        in_specs=_halo_specs(bm, seq, d) + [
            pl.BlockSpec((1, 6, d), lambda b, i: (b, 0, 0)),
            _resident((1, d)),
            _resident((d, 3 * d)),
            _resident((3, 3 * d)),
            _resident((1, 3 * d)),
        ],
        out_specs=[blk, blk, blk],
        out_shape=[shp, shp, shp],
        scratch_shapes=[
            pltpu.VMEM((bm + 2 * HALO, d), BF16),
            pltpu.VMEM((bm + 2 * HALO, HY_CHUNK), F32),
        ],
        compiler_params=_params("parallel", "parallel"),
        name="hy_in",
    )(x, x, x, mod_l, norm_g.reshape(1, d), w_in.astype(BF16), conv_w, conv_b.reshape(1, 3 * d))
    zc = v
    for o, gate in enumerate((x1, x2)):
        zc = _long_conv_gate(zc, gate, kf, o, skip[o], consts)
    return _proj_res(zc, w_out, x, mod_l, 2)


def _s5_prep_kernel(x_ref, mod_ref, g_ref, o_ref):
    m = mod_ref[0]
    o_ref[...] = _norm_mod(x_ref[0], g_ref[...], m[1:2], m[0:1])


def _s5_scan_kernel(h_ref, bm_ref, cm_ref, lam_ref, y_ref, s_ref, carry_ref, *, rows):
    d = pl.program_id(0)
    c = pl.program_id(1)
    n_slab = D_MODEL // S5_SLAB
    width = S5_SLAB // S5_GROUP * S5_STATE
    n_tiles = rows // SUBLANES

    @pl.when(c == 0)
    def _():
        carry_ref[...] = jnp.zeros(carry_ref.shape, F32)

    take_rolled = lax.broadcasted_iota(jnp.int32, (SUBLANES, width), 0) // (SUBLANES // 2) == d

    for k in range(n_slab):
        u = h_ref[:, k * S5_SLAB:(k + 1) * S5_SLAB]
        s_ref[...] = jnp.dot(u.astype(BF16), bm_ref[0, k], preferred_element_type=F32)
        l1r = lam_ref[0, k, 0]
        l1i = lam_ref[0, k, 1]
        l2r = lam_ref[0, k, 2]
        l2i = lam_ref[0, k, 3]

        def step(i, carry):
            cr, ci = carry
            ii = jnp.where(d == 0, i, n_tiles - 1 - i)
            r0 = pl.multiple_of(ii * SUBLANES, SUBLANES)
            br = s_ref[pl.ds(r0, SUBLANES), 0:width]
            bi = s_ref[pl.ds(r0, SUBLANES), width:2 * width]
            sr = pltpu.roll(br, SUBLANES // 2, axis=0)
            si = pltpu.roll(bi, SUBLANES // 2, axis=0)
            vr = br + (l1r * sr - l1i * si)
            vi = bi + (l1r * si + l1i * sr)
            pr = jnp.where(take_rolled, pltpu.roll(cr, SUBLANES // 2, axis=0), cr)
            pi = jnp.where(take_rolled, pltpu.roll(ci, SUBLANES // 2, axis=0), ci)
            xr = vr + (l2r * pr - l2i * pi)
            xi = vi + (l2r * pi + l2i * pr)
            s_ref[pl.ds(r0, SUBLANES), 0:width] = xr
            s_ref[pl.ds(r0, SUBLANES), width:2 * width] = xi
            return xr, xi

        cr, ci = lax.fori_loop(0, n_tiles, step, (carry_ref[k, 0], carry_ref[k, 1]))
        carry_ref[k, 0] = cr
        carry_ref[k, 1] = ci
        y_ref[0, :, k * S5_SLAB:(k + 1) * S5_SLAB] = jnp.dot(
            s_ref[...].astype(BF16), cm_ref[0, k], preferred_element_type=F32)


def _s5_glu_kernel(yf_ref, yb_ref, x_ref, mod_ref, g_ref, dsk_ref, w_ref, o_ref):
    m = mod_ref[0]
    x = x_ref[0]
    h = _norm_mod(x, g_ref[...], m[1:2], m[0:1])
    y = jax.nn.gelu(yf_ref[0] + yb_ref[0] + dsk_ref[...] * h)
    g = jnp.dot(y.astype(BF16), w_ref[...], preferred_element_type=F32)
    d = x.shape[-1]
    o_ref[0] = x + m[2:3] * (g[:, :d] * jax.nn.sigmoid(g[:, d:]))


def _s5_tables(a_re, a_im, log_dt, b_re, b_im, c_re, c_im):
    n_slab = D_MODEL // S5_SLAB
    gps = S5_SLAB // S5_GROUP
    lam = lax.complex(jnp.minimum(a_re.astype(F32), -1e-4), a_im.astype(F32))
    dt = jnp.exp(log_dt.astype(F32))[..., None]
    lam_bar = jnp.exp(lam * dt)
    b_bar = ((lam_bar - 1.0) / lam)[..., None] * lax.complex(b_re.astype(F32), b_im.astype(F32))
    eye = jnp.eye(gps, dtype=F32)
    bb = b_bar.reshape(2, n_slab, gps, S5_STATE, S5_GROUP)

    def b_block(part):
        m = jnp.einsum("dkgph,gj->dkghjp", part, eye)
        return m.reshape(2, n_slab, S5_SLAB, gps * S5_STATE)

    b_mat = jnp.concatenate([b_block(bb.real), b_block(bb.imag)], axis=-1)
    cc = lax.complex(c_re.astype(F32), c_im.astype(F32)).reshape(2, n_slab, gps, S5_GROUP, S5_STATE)

    def c_block(part):
        m = jnp.einsum("dkghp,gj->dkgpjh", part, eye)
        return m.reshape(2, n_slab, gps * S5_STATE, S5_SLAB)

    c_mat = jnp.concatenate([c_block(cc.real), -c_block(cc.imag)], axis=-2)
    lam1 = lam_bar.reshape(2, n_slab, 1, gps * S5_STATE)
    lam2 = lam1 * lam1
    zero = jnp.zeros_like(lam1)
    half = SUBLANES // 2

    def rows(lo, hi):
        return jnp.concatenate([jnp.broadcast_to(lo, lo.shape[:2] + (half,) + lo.shape[3:]),
                                jnp.broadcast_to(hi, hi.shape[:2] + (half,) + hi.shape[3:])], axis=2)

    first = jnp.stack([rows(zero, lam1)[0], rows(lam1, zero)[1]])
    second = jnp.stack([rows(lam1, lam2)[0], rows(lam2, lam1)[1]])
    lam_t = jnp.stack([first.real, first.imag, second.real, second.imag], axis=2)
    return b_mat, c_mat, lam_t


def _s5_layer(x, mod_l, norm_g, a_re, a_im, log_dt, b_re, b_im, c_re, c_im, d_skip, w_glu):
    bsz, seq, d = x.shape
    assert 2 * bsz == SUBLANES, bsz
    bm = min(ROW_TILE, seq)
    n_slab = d // S5_SLAB
    width = S5_SLAB // S5_GROUP * S5_STATE
    h2 = pl.pallas_call(
        _s5_prep_kernel,
        grid=(bsz, seq // bm),
        in_specs=[
            pl.BlockSpec((1, bm, d), lambda b, i: (b, i, 0)),
            pl.BlockSpec((1, 6, d), lambda b, i: (b, 0, 0)),
            _resident((1, d)),
        ],
        out_specs=pl.BlockSpec((bm, d), lambda b, i: (i, b)),
        out_shape=jax.ShapeDtypeStruct((seq, bsz * d), F32),
        compiler_params=_params("parallel", "parallel"),
        name="s5_prep",
    )(x, mod_l, norm_g.reshape(1, d))

    b_mat, c_mat, lam_t = _s5_tables(a_re, a_im, log_dt, b_re, b_im, c_re, c_im)
    steps = min(S5_STEPS, seq)
    rows = steps * bsz
    nc = seq // steps
    chunk = lambda dd, c: c + dd * (nc - 1 - 2 * c)
    y = pl.pallas_call(
        functools.partial(_s5_scan_kernel, rows=rows),
        grid=(2, nc),
        in_specs=[
            pl.BlockSpec((rows, d), lambda dd, c: (chunk(dd, c), 0)),
            pl.BlockSpec((1, n_slab, S5_SLAB, 2 * width), lambda dd, c: (dd, 0, 0, 0)),
            pl.BlockSpec((1, n_slab, 2 * width, S5_SLAB), lambda dd, c: (dd, 0, 0, 0)),
            pl.BlockSpec((1, n_slab, 4, SUBLANES, width), lambda dd, c: (dd, 0, 0, 0, 0)),
        ],
        out_specs=pl.BlockSpec((1, rows, d), lambda dd, c: (dd, chunk(dd, c), 0)),
        out_shape=jax.ShapeDtypeStruct((2, seq * bsz, d), F32),
        scratch_shapes=[
            pltpu.VMEM((rows, 2 * width), F32),
            pltpu.VMEM((n_slab, 2, SUBLANES, width), F32),
        ],
        compiler_params=_params("arbitrary", "arbitrary"),
        name="s5_scan",
    )(h2.reshape(seq * bsz, d), b_mat.astype(BF16), c_mat.astype(BF16), lam_t)

    yv = y.reshape(2, seq, bsz * d)
    return pl.pallas_call(
        _s5_glu_kernel,
        grid=(bsz, seq // bm),
        in_specs=[
            pl.BlockSpec((1, bm, d), lambda b, i: (0, i, b)),
            pl.BlockSpec((1, bm, d), lambda b, i: (1, i, b)),
            pl.BlockSpec((1, bm, d), lambda b, i: (b, i, 0)),
            pl.BlockSpec((1, 6, d), lambda b, i: (b, 0, 0)),
            _resident((1, d)),
            _resident((1, d)),
            _resident((d, 2 * d)),
        ],
        out_specs=pl.BlockSpec((1, bm, d), lambda b, i: (b, i, 0)),
        out_shape=jax.ShapeDtypeStruct(x.shape, F32),
        compiler_params=_params("parallel", "parallel"),
        name="s5_glu",
    )(yv, yv, x, mod_l, norm_g.reshape(1, d), d_skip.reshape(1, d), w_glu.astype(BF16))


def kernel(x, c, ada_w, ada_b, norm1_g, norm2_g, final_g, attn_w_qkv, attn_w_o, attn_q_gain, attn_k_gain, hy_w_in, hy_conv_w, hy_conv_b, hy_f_w1, hy_f_b1, hy_f_w2, hy_f_b2, hy_f_w3, hy_f_freq, hy_skip, hy_w_out, s5_A_re, s5_A_im, s5_log_dt, s5_B_re, s5_B_im, s5_C_re, s5_C_im, s5_D, s5_w_glu, ffn_w_up, ffn_conv_w, ffn_conv_b, ffn_w_down):
    seq = x.shape[1]
    mod = _ada_mod(c, ada_w, ada_b)
    cos, sin = _rope_tables(seq)
    consts = _dft_constants(seq)
    for i in range(DEPTH):
        m, j = i % N_MIXERS, i // N_MIXERS
        if m == 0:
            x = _attention_layer(x, mod[i], norm1_g[i], attn_w_qkv[j], attn_w_o[j],
                                 attn_q_gain[j], attn_k_gain[j], cos, sin)
        elif m == 1:
            kf = _hyena_filter_spectrum(seq, consts, hy_f_w1[j], hy_f_b1[j], hy_f_w2[j], hy_f_b2[j],
                                        hy_f_w3[j], hy_f_freq[j])
            x = _hyena_layer(x, mod[i], norm1_g[i], hy_w_in[j], hy_conv_w[j], hy_conv_b[j], kf,
                             hy_skip[j], hy_w_out[j], consts)
        else:
            x = _s5_layer(x, mod[i], norm1_g[i], s5_A_re[j], s5_A_im[j], s5_log_dt[j], s5_B_re[j],
                          s5_B_im[j], s5_C_re[j], s5_C_im[j], s5_D[j], s5_w_glu[j])
        x = _conv_ffn_layer(x, mod[i], norm2_g[i], ffn_w_up[i], ffn_conv_w[i], ffn_conv_b[i],
                            ffn_w_down[i], final_g, final_norm=(i == DEPTH - 1))
    return x
```

```python
import functools
import math

import numpy as np
import jax
import jax.numpy as jnp
from jax import lax
from jax.experimental import pallas as pl
from jax.experimental.pallas import tpu as pltpu

D_MODEL = 1024
DEPTH = 4
N_MIXERS = 3
GRID_W = 64
EPS = 1e-6
N_HEADS = 16
HEAD_DIM = D_MODEL // N_HEADS
N_KV_HEADS = 4
KV_GROUP = N_HEADS // N_KV_HEADS
ROPE_THETA = 10000.0
HY_ORDER = 2
HY_EMB_DIM = 33
HY_FILTER_WIDTH = 64
HY_FAST_DECAY = 0.3
HY_SLOW_DECAY = 1.5
HY_TARGET = 1e-2
S5_GROUP = 16
S5_GROUPS = D_MODEL // S5_GROUP
S5_STATE = 64
D_FF = 2816

F32 = jnp.float32
BF16 = jnp.bfloat16
HIGHEST = lax.Precision.HIGHEST

LANES = 128
SUBLANES = 8
BF16_ROWS = 16
VMEM_LIMIT = 56 * 1024 * 1024

ROW_TILE = 512
HALO = BF16_ROWS
HY_CHUNK = 512
Q_TILE = 256
KV_TILE = 1024
DFT_N2 = 128
DFT_STEP = SUBLANES
DFT_LANES = 512
S5_SLAB = 128
S5_STEPS = 128


def _params(*sem):
    return pltpu.CompilerParams(dimension_semantics=sem, vmem_limit_bytes=VMEM_LIMIT)


def _resident(shape):
    zeros = (0,) * len(shape)
    return pl.BlockSpec(shape, lambda *_: zeros, pipeline_mode=pl.Buffered(1))


def _norm_mod(x, g, scale, shift):
    y = x * lax.rsqrt(jnp.mean(x * x, axis=-1, keepdims=True) + EPS)
    return (y * g) * (1.0 + scale) + shift


def _ada_kernel(c_ref, w_ref, b_ref, o_ref):
    c = c_ref[...]
    a = c * jax.nn.sigmoid(c)
    o_ref[0] = jnp.dot(a, w_ref[0], precision=HIGHEST, preferred_element_type=F32) + b_ref[0]


def _ada_mod(c, ada_w, ada_b):
    bsz, d = c.shape
    rows = -(-bsz // SUBLANES) * SUBLANES
    cp = jnp.pad(c, ((0, rows - bsz), (0, 0)))
    tn = 1536
    out = pl.pallas_call(
        _ada_kernel,
        grid=(DEPTH, 6 * d // tn),
        in_specs=[
            pl.BlockSpec((rows, d), lambda i, j: (0, 0)),
            pl.BlockSpec((1, d, tn), lambda i, j: (i, 0, j)),
            pl.BlockSpec((1, 1, tn), lambda i, j: (i, 0, j)),
        ],
        out_specs=pl.BlockSpec((1, rows, tn), lambda i, j: (i, 0, j)),
        out_shape=jax.ShapeDtypeStruct((DEPTH, rows, 6 * d), F32),
        compiler_params=_params("arbitrary", "arbitrary"),
        name="ada_mod",
    )(cp, ada_w, ada_b.reshape(DEPTH, 1, 6 * d))
    return out[:, :bsz].reshape(DEPTH, bsz, 6, d)


def _qkv_kernel(x_ref, mod_ref, g_ref, w_ref, seg_ref, c_ref, s1_ref, s2_ref, qg_ref, kg_ref,
                q_ref, klo_ref, khi_ref, v_ref):
    m = mod_ref[0]
    h = _norm_mod(x_ref[0], g_ref[...], m[1:2], m[0:1])
    r = jnp.dot(h.astype(BF16), w_ref[...], preferred_element_type=F32)
    c = c_ref[...]
    s1 = s1_ref[...]
    s2 = s2_ref[...]
    seg = seg_ref[...]
    quarter = HEAD_DIM // 2

    def norm_rope(xr, gain):
        ss = xr * xr
        hi = ss.astype(BF16)
        lo = (ss - hi.astype(F32)).astype(BF16)
        tot = (jnp.dot(hi, seg, preferred_element_type=F32) + jnp.dot(lo, seg, preferred_element_type=F32))
        n = xr * lax.rsqrt(tot * (1.0 / HEAD_DIM) + EPS) * gain
        return (n * c + pltpu.roll(n, LANES - quarter, axis=1) * s1 + pltpu.roll(n, quarter, axis=1) * s2)

    qg = qg_ref[...]
    kg = kg_ref[...]
    n_q = N_HEADS * HEAD_DIM // LANES
    for p in range(n_q):
        q_ref[0, p] = norm_rope(r[:, p * LANES:(p + 1) * LANES], qg).astype(BF16)
    low = lax.broadcasted_iota(jnp.int32, (r.shape[0], LANES), 1) < HEAD_DIM
    for g in range(N_KV_HEADS):
        kk = norm_rope(r[:, (n_q + g) * LANES:(n_q + g + 1) * LANES], kg)
        klo_ref[0, g] = jnp.where(low, kk, 0.0).astype(BF16)
        khi_ref[0, g] = jnp.where(low, 0.0, kk).astype(BF16)
        vv = r[:, (n_q + N_KV_HEADS + g) * LANES:(n_q + N_KV_HEADS + g + 1) * LANES]
        v_ref[0, g] = jnp.where(low, vv, 1.0).astype(BF16)


def _flash_kernel(q_ref, qn_ref, klo_ref, khi_ref, v_ref, o_ref, sa_ref, sb_ref, m_ref, acc_ref, *, n_kv):
    rows = KV_GROUP * Q_TILE
    m_ref[...] = jnp.full(m_ref.shape, -jnp.inf, F32)
    acc_ref[...] = jnp.zeros(acc_ref.shape, F32)

    def scores(s_ref, src_ref, j):
        start = pl.multiple_of(j * KV_TILE, KV_TILE)
        qq = src_ref[0].reshape(rows // 2, LANES)
        for half, kk_ref in enumerate((klo_ref, khi_ref)):
            kj = kk_ref[0, 0, pl.ds(start, KV_TILE), :]
            s_ref[half * (rows // 2):(half + 1) * (rows // 2)] = lax.dot_general(
                qq, kj, (((1,), (1,)), ((), ())), preferred_element_type=F32)

    def consume(s_ref, j):
        s = s_ref[...]
        start = pl.multiple_of(j * KV_TILE, KV_TILE)
        vj = v_ref[0, 0, pl.ds(start, KV_TILE), :]
        m_prev = m_ref[...]
        m_next = jnp.maximum(m_prev, jnp.max(s, axis=-1, keepdims=True))
        alpha = jnp.exp(m_prev - m_next)
        p = jnp.concatenate([jnp.exp(s[:, c * LANES:(c + 1) * LANES] - m_next)
                             for c in range(KV_TILE // LANES)], axis=1)
        acc_ref[...] = alpha * acc_ref[...] + jnp.dot(p.astype(BF16), vj, preferred_element_type=F32)
        m_ref[...] = m_next

    @pl.when(pl.program_id(2) == 0)
    def _():
        scores(sa_ref, q_ref, 0)

    def body(jj, carry):
        scores(sb_ref, q_ref, 2 * jj + 1)
        consume(sa_ref, 2 * jj)
        scores(sa_ref, q_ref, 2 * jj + 2)
        consume(sb_ref, 2 * jj + 1)
        return carry

    lax.fori_loop(0, n_kv // 2 - 1, body, 0)
    scores(sb_ref, q_ref, n_kv - 1)
    consume(sa_ref, n_kv - 2)
    scores(sa_ref, qn_ref, 0)
    consume(sb_ref, n_kv - 1)
    acc = acc_ref[...]
    o = acc[:, :HEAD_DIM] / acc[:, HEAD_DIM:HEAD_DIM + 1]
    o_ref[0] = jnp.concatenate(
        [o[g * Q_TILE:(g + 1) * Q_TILE] for g in ((j % 2) * (KV_GROUP // 2) + j // 2 for j in range(KV_GROUP))],
        axis=-1).astype(o_ref.dtype)


def _proj_res_kernel(a_ref, w_ref, x_ref, mod_ref, o_ref, *, gate_row):
    y = jnp.dot(a_ref[0].astype(BF16), w_ref[...], preferred_element_type=F32)
    o_ref[0] = x_ref[0] + mod_ref[0][gate_row:gate_row + 1] * y


def _proj_res(a, w, x, mod_l, gate_row):
    bsz, seq, d = x.shape
    kdim = a.shape[-1]
    bm = min(ROW_TILE, seq)
    return pl.pallas_call(
        functools.partial(_proj_res_kernel, gate_row=gate_row),
        grid=(bsz, seq // bm),
        in_specs=[
            pl.BlockSpec((1, bm, kdim), lambda b, i: (b, i, 0)),
            _resident((kdim, d)),
            pl.BlockSpec((1, bm, d), lambda b, i: (b, i, 0)),
            pl.BlockSpec((1, 6, d), lambda b, i: (b, 0, 0)),
        ],
        out_specs=pl.BlockSpec((1, bm, d), lambda b, i: (b, i, 0)),
        out_shape=jax.ShapeDtypeStruct(x.shape, F32),
        compiler_params=_params("parallel", "parallel"),
        name="proj_res",
    )(a, w.astype(BF16), x, mod_l)


def _rope_tables(seq):
    rows = seq // GRID_W
    n_freq = HEAD_DIM // 4
    inv = 1.0 / (ROPE_THETA ** (jnp.arange(n_freq, dtype=F32) / n_freq))
    r = jnp.arange(rows, dtype=F32)
    col = jnp.arange(GRID_W, dtype=F32)
    ang_r = jnp.broadcast_to(r[:, None, None] * inv, (rows, GRID_W, n_freq))
    ang_c = jnp.broadcast_to(col[None, :, None] * inv, (rows, GRID_W, n_freq))
    ang = jnp.concatenate([ang_r, ang_c], axis=-1).reshape(seq, 2 * n_freq)
    return jnp.cos(ang), jnp.sin(ang)


def _attention_layer(x, mod_l, norm_g, w_qkv, w_o, q_gain, k_gain, cos, sin):
    bsz, seq, d = x.shape
    bm = min(ROW_TILE, seq)
    perm = np.concatenate([np.arange(0, HEAD_DIM, 2), np.arange(1, HEAD_DIM, 2)])
    head_cols = lambda hh: hh * HEAD_DIM + perm
    k0 = N_HEADS * HEAD_DIM
    v0 = k0 + N_KV_HEADS * HEAD_DIM
    q_cols = np.concatenate([head_cols(hh) for hh in range(N_HEADS)])
    k_cols = np.concatenate([np.tile(k0 + head_cols(g), 2) for g in range(N_KV_HEADS)])
    w_v = jnp.pad(w_qkv[:, v0:].reshape(d, N_KV_HEADS, HEAD_DIM), ((0, 0), (0, 0), (0, LANES - HEAD_DIM)))
    w = jnp.concatenate([w_qkv[:, q_cols], w_qkv[:, k_cols], w_v.reshape(d, N_KV_HEADS * LANES)],
                        axis=1).astype(BF16)
    n_out = w.shape[1]
    reps = LANES // (HEAD_DIM // 2)
    zero = jnp.zeros_like(sin)
    c_t = jnp.tile(cos, (1, reps))
    s1_t = jnp.tile(jnp.concatenate([-sin, zero], axis=1), (1, reps // 2))
    s2_t = jnp.tile(jnp.concatenate([zero, sin], axis=1), (1, reps // 2))
    seg = jnp.asarray(np.kron(np.eye(LANES // HEAD_DIM), np.ones((HEAD_DIM, HEAD_DIM))), dtype=BF16)
    tab = pl.BlockSpec((bm, LANES), lambda b, i: (i, 0))
    n_pairs = N_HEADS * HEAD_DIM // LANES
    kv_spec = pl.BlockSpec((1, N_KV_HEADS, bm, LANES), lambda b, i: (b, 0, i, 0))
    kv_shape = jax.ShapeDtypeStruct((bsz, N_KV_HEADS, seq, LANES), BF16)
    q, k_lo, k_hi, v = pl.pallas_call(
        _qkv_kernel,
        grid=(bsz, seq // bm),
        in_specs=[
            pl.BlockSpec((1, bm, d), lambda b, i: (b, i, 0)),
            pl.BlockSpec((1, 6, d), lambda b, i: (b, 0, 0)),
            _resident((1, d)),
            _resident((d, n_out)),
            _resident((LANES, LANES)),
            tab,
            tab,
            tab,
            _resident((1, LANES)),
            _resident((1, LANES)),
        ],
        out_specs=[
            pl.BlockSpec((1, n_pairs, bm, LANES), lambda b, i: (b, 0, i, 0)),
            kv_spec,
            kv_spec,
            kv_spec,
        ],
        out_shape=[
            jax.ShapeDtypeStruct((bsz, n_pairs, seq, LANES), BF16),
            kv_shape,
            kv_shape,
            kv_shape,
        ],
        compiler_params=_params("parallel", "parallel"),
        name="attn_qkv",
    )(x, mod_l, norm_g.reshape(1, d), w, seg, c_t, s1_t, s2_t,
      jnp.tile(q_gain[perm] * HEAD_DIM ** -0.5, 2).reshape(1, LANES), jnp.tile(k_gain[perm], 2).reshape(1, LANES))

    o = _flash_attention(q, k_lo, k_hi, v)
    return _proj_res(o, w_o, x, mod_l, 2)


def _flash_attention(q, k_lo, k_hi, v):
    bsz, _, seq, _ = q.shape
    assert (seq // KV_TILE) % 2 == 0, seq
    rows = KV_GROUP * Q_TILE
    n_q = seq // Q_TILE
    kv_spec = pl.BlockSpec((1, 1, seq, LANES), lambda b, g, i: (b, g, 0, 0))
    return pl.pallas_call(
        functools.partial(_flash_kernel, n_kv=seq // KV_TILE),
        grid=(bsz, N_KV_HEADS, n_q),
        in_specs=[
            pl.BlockSpec((1, KV_GROUP // 2, Q_TILE, LANES), lambda b, g, i: (b, g, i, 0)),
            pl.BlockSpec((1, KV_GROUP // 2, Q_TILE, LANES), lambda b, g, i: (b, g, jnp.minimum(i + 1, n_q - 1), 0)),
            kv_spec,
            kv_spec,
            kv_spec,
        ],
        out_specs=pl.BlockSpec((1, Q_TILE, KV_GROUP * HEAD_DIM), lambda b, g, i: (b, i, g)),
        out_shape=jax.ShapeDtypeStruct((bsz, seq, N_HEADS * HEAD_DIM), BF16),
        scratch_shapes=[
            pltpu.VMEM((rows, KV_TILE), F32),
            pltpu.VMEM((rows, KV_TILE), F32),
            pltpu.VMEM((rows, LANES), F32),
            pltpu.VMEM((rows, 2 * HEAD_DIM), F32),
        ],
        compiler_params=_params("parallel", "parallel", "arbitrary"),
        name="attn_flash",
    )(q, q, k_lo, k_hi, v)


def _halo_rows(i, bm, seq):
    t = i * bm - HALO + lax.broadcasted_iota(jnp.int32, (bm + 2 * HALO, 1), 0)
    return (t >= 0) & (t < seq)


def _fill_window(hs_ref, xp_ref, x_ref, xn_ref, g, scale, shift, bm):
    hs_ref[0:HALO] = _norm_mod(xp_ref[0], g, scale, shift).astype(BF16)
    hs_ref[HALO:HALO + bm] = _norm_mod(x_ref[0], g, scale, shift).astype(BF16)
    hs_ref[HALO + bm:] = _norm_mod(xn_ref[0], g, scale, shift).astype(BF16)


def _conv3(win_ref, cw, cb, bm):
    return (win_ref[pl.ds(HALO - 1, bm), :] * cw[0:1] + win_ref[pl.ds(HALO, bm), :] * cw[1:2]
            + win_ref[pl.ds(HALO + 1, bm), :] * cw[2:3] + cb)


def _ffn_kernel(xp_ref, x_ref, xn_ref, mod_ref, g_ref, wu_ref, cw_ref, cb_ref, wd_ref, fg_ref,
                o_ref, hs_ref, gs_ref, *, bm, seq, final_norm):
    m = mod_ref[0]
    _fill_window(hs_ref, xp_ref, x_ref, xn_ref, g_ref[...], m[4:5], m[3:4], bm)
    valid = _halo_rows(pl.program_id(1), bm, seq)
    gate = jnp.dot(hs_ref[...], wu_ref[:, :D_FF], preferred_element_type=F32)
    gs_ref[...] = jnp.where(valid, gate, 0.0)
    gate = _conv3(gs_ref, cw_ref[...], cb_ref[...], bm)
    val = jnp.dot(hs_ref[HALO:HALO + bm], wu_ref[:, D_FF:], preferred_element_type=F32)
    a = (gate * jax.nn.sigmoid(gate)) * val
    y = x_ref[0] + m[5:6] * jnp.dot(a.astype(BF16), wd_ref[...], preferred_element_type=F32)
    if final_norm:
        y = y * lax.rsqrt(jnp.mean(y * y, axis=-1, keepdims=True) + EPS) * fg_ref[...]
    o_ref[0] = y


def _halo_specs(bm, seq, d):
    nh = seq // HALO
    per = bm // HALO
    return [
        pl.BlockSpec((1, HALO, d), lambda b, i: (b, jnp.maximum(i * per - 1, 0), 0)),
        pl.BlockSpec((1, bm, d), lambda b, i: (b, i, 0)),
        pl.BlockSpec((1, HALO, d), lambda b, i: (b, jnp.minimum((i + 1) * per, nh - 1), 0)),
    ]


def _conv_ffn_layer(x, mod_l, norm_g, w_up, conv_w, conv_b, w_down, final_g, final_norm):
    bsz, seq, d = x.shape
    bm = min(ROW_TILE, seq)
    return pl.pallas_call(
        functools.partial(_ffn_kernel, bm=bm, seq=seq, final_norm=final_norm),
        grid=(bsz, seq // bm),
        in_specs=_halo_specs(bm, seq, d) + [
            pl.BlockSpec((1, 6, d), lambda b, i: (b, 0, 0)),
            _resident((1, d)),
            _resident((d, 2 * D_FF)),
            _resident((3, D_FF)),
            _resident((1, D_FF)),
            _resident((D_FF, d)),
            _resident((1, d)),
        ],
        out_specs=pl.BlockSpec((1, bm, d), lambda b, i: (b, i, 0)),
        out_shape=jax.ShapeDtypeStruct(x.shape, F32),
        scratch_shapes=[
            pltpu.VMEM((bm + 2 * HALO, d), BF16),
            pltpu.VMEM((bm + 2 * HALO, D_FF), F32),
        ],
        compiler_params=_params("parallel", "parallel"),
        name="conv_ffn",
    )(x, x, x, mod_l, norm_g.reshape(1, d), w_up.astype(BF16), conv_w, conv_b.reshape(1, D_FF),
      w_down.astype(BF16), final_g.reshape(1, d))


def _hy_in_kernel(xp_ref, x_ref, xn_ref, mod_ref, g_ref, w_ref, cw_ref, cb_ref,
                  v_ref, x1_ref, x2_ref, hs_ref, us_ref, *, bm, seq):
    m = mod_ref[0]
    _fill_window(hs_ref, xp_ref, x_ref, xn_ref, g_ref[...], m[1:2], m[0:1], bm)
    valid = _halo_rows(pl.program_id(1), bm, seq)
    outs = (v_ref, x1_ref, x2_ref)
    for j in range(3 * D_MODEL // HY_CHUNK):
        c0 = j * HY_CHUNK
        u = jnp.dot(hs_ref[...], w_ref[:, c0:c0 + HY_CHUNK], preferred_element_type=F32)
        us_ref[...] = jnp.where(valid, u, 0.0)
        y = _conv3(us_ref, cw_ref[:, c0:c0 + HY_CHUNK], cb_ref[:, c0:c0 + HY_CHUNK], bm)
        off = c0 % D_MODEL
        outs[c0 // D_MODEL][0, :, off:off + HY_CHUNK] = y


def _hy_filter_kernel(z_ref, w1_ref, b1_ref, w2_ref, b2_ref, w3_ref, fr_ref, dl_ref, k_ref, n_ref,
                      *, bm, seq):
    i = pl.program_id(0)
    z = z_ref[...]
    fr = fr_ref[...]
    a = jnp.sin(fr * (jnp.dot(z, w1_ref[...], precision=HIGHEST, preferred_element_type=F32) + b1_ref[...]))
    a = jnp.sin(fr * (jnp.dot(a, w2_ref[...], precision=HIGHEST, preferred_element_type=F32) + b2_ref[...]))
    decay = jnp.exp(-z[:, 0:1] * dl_ref[...])
    t = i * bm + lax.broadcasted_iota(jnp.int32, (bm, 1), 0)
    decay = jnp.where(t == seq, 0.0, decay)

    @pl.when(i == 0)
    def _():
        n_ref[...] = jnp.zeros(n_ref.shape, F32)

    for o in range(HY_ORDER):
        f = jnp.dot(a, w3_ref[0, o], precision=HIGHEST, preferred_element_type=F32) * decay
        k_ref[o] = f
        n_ref[o:o + 1] += jnp.sum(jnp.abs(f), axis=0, keepdims=True)


def _dft_dot(w_ref, x):
    return jnp.dot(w_ref[...], x.astype(BF16), preferred_element_type=F32)


def _dft_in_kernel(x_ref, w_ref, o_ref, *, n1):
    for s in range(DFT_STEP):
        r = _dft_dot(w_ref, jnp.concatenate([x_ref[0, :, s, :], x_ref[1, :, s, :]], axis=0))
        o_ref[0, s, 0] = r[:n1]
        o_ref[0, s, 1] = r[n1:]


def _dft_filter_in_kernel(x_ref, w_ref, o_ref, *, n1):
    for s in range(DFT_STEP):
        r = _dft_dot(w_ref, jnp.concatenate([x_ref[0, :n1 // 2, s, :], x_ref[0, n1 // 2:, s, :]], axis=0))
        o_ref[0, s, 0] = r[:n1]
        o_ref[0, s, 1] = r[n1:]


def _twiddle(ar, ai, twr, twi):
    return ar * twr - ai * twi, ar * twi + ai * twr


def _dft_filter_mid_kernel(a_ref, w_ref, n_ref, o_ref):
    inv = 1.0 / n_ref[0]
    for j in range(DFT_STEP):
        z = _dft_dot(w_ref.at[j], jnp.concatenate([a_ref[0, :, 0, j, :], a_ref[0, :, 1, j, :]], axis=0))
        o_ref[0, 0, j] = z[:DFT_N2] * inv
        o_ref[0, 1, j] = z[DFT_N2:] * inv


def _dft_mid_kernel(a_ref, kf_ref, w_ref, wc_ref, o_ref):
    for j in range(DFT_STEP):
        z = _dft_dot(w_ref.at[j], jnp.concatenate([a_ref[0, :, 0, j, :], a_ref[0, :, 1, j, :]], axis=0))
        yr, yi = _twiddle(z[:DFT_N2], z[DFT_N2:], kf_ref[0, 0, j], kf_ref[0, 1, j])
        b = _dft_dot(wc_ref.at[j], jnp.concatenate([yr, yi], axis=0))
        o_ref[0, j, 0] = b[:DFT_N2]
        o_ref[0, j, 1] = b[DFT_N2:]


def _dft_out_kernel(b_ref, w_ref, gate_ref, zc_ref, skip_ref, o_ref, *, half):
    skip = skip_ref[...]
    for s in range(DFT_STEP):
        y = _dft_dot(w_ref, jnp.concatenate([b_ref[0, :, 0, s, :], b_ref[0, :, 1, s, :]], axis=0))
        o_ref[0, :, s, :] = gate_ref[0, :, s, :] * (y[:half] + skip * zc_ref[0, :, s, :])
        o_ref[1, :, s, :] = gate_ref[1, :, s, :] * (y[half:] + skip * zc_ref[1, :, s, :])


def _dft_constants(seq):
    n = 2 * seq
    n2 = DFT_N2
    n1 = n // n2
    half = n1 // 2
    k1 = np.arange(n1, dtype=np.float64)
    f1 = np.exp(-2j * np.pi * np.outer(k1, k1) / n1)

    def c2(m):
        return np.block([[m.real, -m.imag], [m.imag, m.real]])

    w_in = c2(f1[:, :half])
    w_fin = np.concatenate([f1.real, f1.imag], axis=0)
    w_out = c2(np.conj(f1)[:half, :]) / n
    as16 = lambda a: jnp.asarray(a, dtype=F32).astype(BF16)
    i1 = jnp.arange(n1, dtype=jnp.int32)[:, None, None]
    i2 = jnp.arange(n2, dtype=jnp.int32)
    phase = (i1 * i2[None, None, :] + n1 * (i2[None, :, None] * i2[None, None, :])) % n
    ang = phase.astype(F32) * (-2.0 * math.pi / n)
    mr, mi = jnp.cos(ang), jnp.sin(ang)
    w_mid = jnp.concatenate([jnp.concatenate([mr, -mi], axis=2), jnp.concatenate([mi, mr], axis=2)], axis=1)
    return dict(n1=n1, half=half, w_in=as16(w_in), w_fin=as16(w_fin), w_out=as16(w_out),
                w_mid=w_mid.astype(BF16), w_mid_c=jnp.swapaxes(w_mid, 1, 2).astype(BF16))


def _hyena_pos_features(seq):
    t = jnp.linspace(0.0, 1.0, seq, dtype=F32)[:, None]
    bands = (HY_EMB_DIM - 1) // 2
    w = 2.0 * math.pi * jnp.arange(seq, dtype=F32) / seq
    f = jnp.linspace(1e-4, bands - 1, bands, dtype=F32)
    ang = w[:, None] * f[None, :]
    z = jnp.concatenate([t, jnp.cos(ang), -jnp.sin(ang)], axis=-1)
    deltas = jnp.abs(jnp.linspace(math.log(HY_TARGET) / HY_SLOW_DECAY,
                                  math.log(HY_TARGET) / HY_FAST_DECAY, D_MODEL, dtype=F32))
    return z, deltas


def _hyena_filter_spectrum(seq, consts, f_w1, f_b1, f_w2, f_b2, f_w3, f_freq):
    d = D_MODEL
    n = 2 * seq
    n1, n2 = consts["n1"], DFT_N2
    z, deltas = _hyena_pos_features(seq)
    z2 = jnp.concatenate([z, z[:1], z[1:][::-1]], axis=0)
    kpad = HY_FILTER_WIDTH - HY_EMB_DIM
    z2 = jnp.pad(z2, ((0, 0), (0, kpad)))
    w1 = jnp.pad(f_w1, ((0, kpad), (0, 0)))
    bm = min(ROW_TILE, seq)
    nblk = n // bm
    k2, norm = pl.pallas_call(
        functools.partial(_hy_filter_kernel, bm=bm, seq=seq),
        grid=(nblk,),
        in_specs=[
            pl.BlockSpec((bm, HY_FILTER_WIDTH), lambda i: (i, 0)),
            _resident((HY_FILTER_WIDTH, HY_FILTER_WIDTH)),
            _resident((1, HY_FILTER_WIDTH)),
            _resident((HY_FILTER_WIDTH, HY_FILTER_WIDTH)),
            _resident((1, HY_FILTER_WIDTH)),
            pl.BlockSpec((1, HY_ORDER, HY_FILTER_WIDTH, d), lambda i: ((2 * i) // nblk, 0, 0, 0)),
            _resident((1, HY_FILTER_WIDTH)),
            _resident((1, d)),
        ],
        out_specs=[
            pl.BlockSpec((HY_ORDER, bm, d), lambda i: (0, i, 0)),
            pl.BlockSpec((HY_ORDER, d), lambda i: (0, 0)),
        ],
        out_shape=[
            jax.ShapeDtypeStruct((HY_ORDER, n, d), F32),
            jax.ShapeDtypeStruct((HY_ORDER, d), F32),
        ],
        compiler_params=_params("arbitrary"),
        name="hy_filter",
    )(z2, w1, f_b1.reshape(1, -1), f_w2, f_b2.reshape(1, -1),
      f_w3.reshape(HY_FILTER_WIDTH, HY_ORDER, 2, d).transpose(2, 1, 0, 3), f_freq.reshape(1, -1),
      deltas.reshape(1, d))

    st, ct = DFT_STEP, DFT_LANES
    a = pl.pallas_call(
        functools.partial(_dft_filter_in_kernel, n1=n1),
        grid=(HY_ORDER, n2 // st),
        in_specs=[
            pl.BlockSpec((1, n1, st, d), lambda o, j: (o, 0, j, 0)),
            _resident((2 * n1, n1)),
        ],
        out_specs=pl.BlockSpec((1, st, 2, n1, d), lambda o, j: (o, j, 0, 0, 0)),
        out_shape=jax.ShapeDtypeStruct((HY_ORDER, n2, 2, n1, d), F32),
        compiler_params=_params("parallel", "parallel"),
        name="hy_filter_dft_in",
    )(k2.reshape(HY_ORDER, n1, n2, d), consts["w_fin"])

    return pl.pallas_call(
        _dft_filter_mid_kernel,
        grid=(HY_ORDER, n1 // st, d // ct),
        in_specs=[
            pl.BlockSpec((1, n2, 2, st, ct), lambda o, k, c: (o, 0, 0, k, c)),
            pl.BlockSpec((st, 2 * n2, 2 * n2), lambda o, k, c: (k, 0, 0)),
            pl.BlockSpec((1, 1, ct), lambda o, k, c: (o, 0, c)),
        ],
        out_specs=pl.BlockSpec((1, 2, st, n2, ct), lambda o, k, c: (o, 0, k, 0, c)),
        out_shape=jax.ShapeDtypeStruct((HY_ORDER, 2, n1, n2, d), F32),
        compiler_params=_params("parallel", "parallel", "parallel"),
        name="hy_filter_dft_mid",
    )(a, consts["w_mid"], norm.reshape(HY_ORDER, 1, d))


def _long_conv_gate(zc, gate, kf, order, skip, consts):
    bsz, seq, d = zc.shape
    n1, half, n2 = consts["n1"], consts["half"], DFT_N2
    pairs = bsz // 2
    st, ct = DFT_STEP, DFT_LANES
    zv = zc.reshape(bsz, half, n2, d)
    a = pl.pallas_call(
        functools.partial(_dft_in_kernel, n1=n1),
        grid=(pairs, n2 // st),
        in_specs=[
            pl.BlockSpec((2, half, st, d), lambda p, j: (p, 0, j, 0)),
            _resident((2 * n1, n1)),
        ],
        out_specs=pl.BlockSpec((1, st, 2, n1, d), lambda p, j: (p, j, 0, 0, 0)),
        out_shape=jax.ShapeDtypeStruct((pairs, n2, 2, n1, d), F32),
        compiler_params=_params("parallel", "parallel"),
        name="hy_dft_in",
    )(zv, consts["w_in"])

    b = pl.pallas_call(
        _dft_mid_kernel,
        grid=(n1 // st, d // ct, pairs),
        in_specs=[
            pl.BlockSpec((1, n2, 2, st, ct), lambda k, c, p: (p, 0, 0, k, c)),
            pl.BlockSpec((1, 2, st, n2, ct), lambda k, c, p: (order, 0, k, 0, c)),
            pl.BlockSpec((st, 2 * n2, 2 * n2), lambda k, c, p: (k, 0, 0)),
            pl.BlockSpec((st, 2 * n2, 2 * n2), lambda k, c, p: (k, 0, 0)),
        ],
        out_specs=pl.BlockSpec((1, st, 2, n2, ct), lambda k, c, p: (p, k, 0, 0, c)),
        out_shape=jax.ShapeDtypeStruct((pairs, n1, 2, n2, d), F32),
        compiler_params=_params("parallel", "parallel", "arbitrary"),
        name="hy_dft_mid",
    )(a, kf, consts["w_mid"], consts["w_mid_c"])

    sig = pl.BlockSpec((2, half, st, ct), lambda p, j, c: (p, 0, j, c))
    out = pl.pallas_call(
        functools.partial(_dft_out_kernel, half=half),
        grid=(pairs, n2 // st, d // ct),
        in_specs=[
            pl.BlockSpec((1, n1, 2, st, ct), lambda p, j, c: (p, 0, 0, j, c)),
            _resident((n1, 2 * n1)),
            sig,
            sig,
            pl.BlockSpec((1, ct), lambda p, j, c: (0, c)),
        ],
        out_specs=sig,
        out_shape=jax.ShapeDtypeStruct((bsz, half, n2, d), F32),
        compiler_params=_params("parallel", "parallel", "parallel"),
        name="hy_dft_out",
    )(b, consts["w_out"], gate.reshape(bsz, half, n2, d), zv, skip.reshape(1, d))
    return out.reshape(bsz, seq, d)


def _hyena_layer(x, mod_l, norm_g, w_in, conv_w, conv_b, kf, skip, w_out, consts):
    bsz, seq, d = x.shape
    bm = min(ROW_TILE, seq)
    blk = pl.BlockSpec((1, bm, d), lambda b, i: (b, i, 0))
    shp = jax.ShapeDtypeStruct(x.shape, F32)
    v, x1, x2 = pl.pallas_call(
        functools.partial(_hy_in_kernel, bm=bm, seq=seq),
        grid=(bsz, seq // bm),
        in_specs=_halo_specs(bm, seq, d) + [
            pl.BlockSpec((1, 6, d), lambda b, i: (b, 0, 0)),
            _resident((1, d)),
            _resident((d, 3 * d)),
            _resident((3, 3 * d)),
            _resident((1, 3 * d)),
        ],
        out_specs=[blk, blk, blk],
        out_shape=[shp, shp, shp],
        scratch_shapes=[
            pltpu.VMEM((bm + 2 * HALO, d), BF16),
            pltpu.VMEM((bm + 2 * HALO, HY_CHUNK), F32),
        ],
        compiler_params=_params("parallel", "parallel"),
        name="hy_in",
    )(x, x, x, mod_l, norm_g.reshape(1, d), w_in.astype(BF16), conv_w, conv_b.reshape(1, 3 * d))
    zc = v
    for o, gate in enumerate((x1, x2)):
        zc = _long_conv_gate(zc, gate, kf, o, skip[o], consts)
    return _proj_res(zc, w_out, x, mod_l, 2)


def _s5_prep_kernel(x_ref, mod_ref, g_ref, o_ref):
    m = mod_ref[0]
    o_ref[...] = _norm_mod(x_ref[0], g_ref[...], m[1:2], m[0:1]).astype(o_ref.dtype)


def _s5_scan_kernel(h_ref, bm_ref, cm_ref, lam_ref, y_ref, sa_ref, sb_ref, sc_ref, carry_ref, *, rows, reverse):
    n_slab = D_MODEL // S5_SLAB
    width = S5_SLAB // S5_GROUP * S5_STATE
    n_tiles = rows // SUBLANES
    half = SUBLANES // 2
    bufs = (sa_ref, sb_ref, sc_ref)

    @pl.when(pl.program_id(0) == 0)
    def _():
        carry_ref[...] = jnp.zeros(carry_ref.shape, F32)

    low = lax.broadcasted_iota(jnp.int32, (SUBLANES, width), 0) < half
    take_rolled = jnp.logical_not(low) if reverse else low
    low3 = lax.broadcasted_iota(jnp.int32, (n_tiles, SUBLANES, S5_SLAB), 1) < half
    gets_other = low3 if reverse else jnp.logical_not(low3)

    def project_in(k):
        u = h_ref[:, k * S5_SLAB:(k + 1) * S5_SLAB]
        u3 = u.astype(F32).reshape(n_tiles, SUBLANES, S5_SLAB)
        other = jnp.where(gets_other, pltpu.roll(u3, half, axis=1), 0.0).reshape(rows, S5_SLAB)
        lhs = jnp.concatenate([u, other.astype(BF16)], axis=1)
        bufs[k % 3][...] = jnp.dot(lhs, bm_ref[k], preferred_element_type=F32)

    def project_out(k):
        y_ref[:, k * S5_SLAB:(k + 1) * S5_SLAB] = jnp.dot(
            bufs[k % 3][...].astype(BF16), cm_ref[k], preferred_element_type=F32).astype(y_ref.dtype)

    def scan(k):
        s_ref = bufs[k % 3]
        lr = lam_ref[k, 0]
        li = lam_ref[k, 1]
        cr = carry_ref[k, 0]
        ci = carry_ref[k, 1]
        for i in (reversed(range(n_tiles)) if reverse else range(n_tiles)):
            r0 = i * SUBLANES
            vr = s_ref[r0:r0 + SUBLANES, 0:width]
            vi = s_ref[r0:r0 + SUBLANES, width:2 * width]
            pr = jnp.where(take_rolled, pltpu.roll(cr, half, axis=0), cr)
            pi = jnp.where(take_rolled, pltpu.roll(ci, half, axis=0), ci)
            cr = vr + (lr * pr - li * pi)
            ci = vi + (lr * pi + li * pr)
            s_ref[r0:r0 + SUBLANES, 0:width] = cr
            s_ref[r0:r0 + SUBLANES, width:2 * width] = ci
        carry_ref[k, 0] = cr
        carry_ref[k, 1] = ci

    project_in(0)
    for k in range(n_slab):
        if k + 1 < n_slab:
            project_in(k + 1)
        if k >= 1:
            project_out(k - 1)
        scan(k)
    project_out(n_slab - 1)


def _s5_glu_kernel(yf_ref, yb_ref, x_ref, mod_ref, g_ref, dsk_ref, w_ref, o_ref):
    m = mod_ref[0]
    x = x_ref[0]
    h = _norm_mod(x, g_ref[...], m[1:2], m[0:1])
    y = jax.nn.gelu(yf_ref[...].astype(F32) + yb_ref[...].astype(F32) + dsk_ref[...] * h)
    g = jnp.dot(y.astype(BF16), w_ref[...], preferred_element_type=F32)
    d = x.shape[-1]
    o_ref[0] = x + m[2:3] * (g[:, :d] * jax.nn.sigmoid(g[:, d:]))


def _s5_tables(a_re, a_im, log_dt, b_re, b_im, c_re, c_im):
    n_slab = D_MODEL // S5_SLAB
    gps = S5_SLAB // S5_GROUP
    lam = lax.complex(jnp.minimum(a_re.astype(F32), -1e-4), a_im.astype(F32))
    dt = jnp.exp(log_dt.astype(F32))[..., None]
    lam_bar = jnp.exp(lam * dt)
    b_bar = ((lam_bar - 1.0) / lam)[..., None] * lax.complex(b_re.astype(F32), b_im.astype(F32))
    eye = jnp.eye(gps, dtype=F32)

    def b_block(part):
        m = jnp.einsum("dkgph,gj->dkghjp", part, eye)
        return m.reshape(2, n_slab, S5_SLAB, gps * S5_STATE)

    def b_rows(b_c):
        bb = b_c.reshape(2, n_slab, gps, S5_STATE, S5_GROUP)
        return jnp.concatenate([b_block(bb.real), b_block(bb.imag)], axis=-1)

    b_mat = jnp.concatenate([b_rows(b_bar), b_rows(b_bar * lam_bar[..., None])], axis=-2)
    cc = lax.complex(c_re.astype(F32), c_im.astype(F32)).reshape(2, n_slab, gps, S5_GROUP, S5_STATE)

    def c_block(part):
        m = jnp.einsum("dkghp,gj->dkgpjh", part, eye)
        return m.reshape(2, n_slab, gps * S5_STATE, S5_SLAB)

    c_mat = jnp.concatenate([c_block(cc.real), -c_block(cc.imag)], axis=-2)
    lam1 = lam_bar.reshape(2, n_slab, 1, gps * S5_STATE)
    lam2 = lam1 * lam1
    half = SUBLANES // 2

    def rows(lo, hi):
        return jnp.concatenate([jnp.broadcast_to(lo, lo.shape[:2] + (half,) + lo.shape[3:]),
                                jnp.broadcast_to(hi, hi.shape[:2] + (half,) + hi.shape[3:])], axis=2)

    carry = jnp.stack([rows(lam1, lam2)[0], rows(lam2, lam1)[1]])
    lam_t = jnp.stack([carry.real, carry.imag], axis=2)
    return b_mat, c_mat, lam_t


def _s5_layer(x, mod_l, norm_g, a_re, a_im, log_dt, b_re, b_im, c_re, c_im, d_skip, w_glu):
    bsz, seq, d = x.shape
    assert 2 * bsz == SUBLANES, bsz
    bm = min(ROW_TILE, seq)
    n_slab = d // S5_SLAB
    width = S5_SLAB // S5_GROUP * S5_STATE
    h2 = pl.pallas_call(
        _s5_prep_kernel,
        grid=(bsz, seq // bm),
        in_specs=[
            pl.BlockSpec((1, bm, d), lambda b, i: (b, i, 0)),
            pl.BlockSpec((1, 6, d), lambda b, i: (b, 0, 0)),
            _resident((1, d)),
        ],
        out_specs=pl.BlockSpec((bm, d), lambda b, i: (i, b)),
        out_shape=jax.ShapeDtypeStruct((seq, bsz * d), BF16),
        compiler_params=_params("parallel", "parallel"),
        name="s5_prep",
    )(x, mod_l, norm_g.reshape(1, d))

    b_mat, c_mat, lam_t = _s5_tables(a_re, a_im, log_dt, b_re, b_im, c_re, c_im)
    steps = min(S5_STEPS, seq)
    rows = steps * bsz
    nc = seq // steps
    h2 = h2.reshape(seq * bsz, d)
    b_mat = b_mat.astype(BF16)
    c_mat = c_mat.astype(BF16)

    def scan(direction):
        reverse = direction == 1
        chunk = (lambda c: (nc - 1 - c, 0)) if reverse else (lambda c: (c, 0))
        return pl.pallas_call(
            functools.partial(_s5_scan_kernel, rows=rows, reverse=reverse),
            grid=(nc,),
            in_specs=[
                pl.BlockSpec((rows, d), chunk),
                _resident((n_slab, 2 * S5_SLAB, 2 * width)),
                _resident((n_slab, 2 * width, S5_SLAB)),
                _resident((n_slab, 2, SUBLANES, width)),
            ],
            out_specs=pl.BlockSpec((rows, d), chunk),
            out_shape=jax.ShapeDtypeStruct((seq * bsz, d), BF16),
            scratch_shapes=[
                pltpu.VMEM((rows, 2 * width), F32),
                pltpu.VMEM((rows, 2 * width), F32),
                pltpu.VMEM((rows, 2 * width), F32),
                pltpu.VMEM((n_slab, 2, SUBLANES, width), F32),
            ],
            compiler_params=_params("arbitrary"),
            name="s5_scan_bwd" if reverse else "s5_scan_fwd",
        )(h2, b_mat[direction], c_mat[direction], lam_t[direction])

    y_f = scan(0).reshape(seq, bsz * d)
    y_b = scan(1).reshape(seq, bsz * d)
    return pl.pallas_call(
        _s5_glu_kernel,
        grid=(bsz, seq // bm),
        in_specs=[
            pl.BlockSpec((bm, d), lambda b, i: (i, b)),
            pl.BlockSpec((bm, d), lambda b, i: (i, b)),
            pl.BlockSpec((1, bm, d), lambda b, i: (b, i, 0)),
            pl.BlockSpec((1, 6, d), lambda b, i: (b, 0, 0)),
            _resident((1, d)),
            _resident((1, d)),
            _resident((d, 2 * d)),
        ],
        out_specs=pl.BlockSpec((1, bm, d), lambda b, i: (b, i, 0)),
        out_shape=jax.ShapeDtypeStruct(x.shape, F32),
        compiler_params=_params("parallel", "parallel"),
        name="s5_glu",
    )(y_f, y_b, x, mod_l, norm_g.reshape(1, d), d_skip.reshape(1, d), w_glu.astype(BF16))


def kernel(x, c, ada_w, ada_b, norm1_g, norm2_g, final_g, attn_w_qkv, attn_w_o, attn_q_gain, attn_k_gain, hy_w_in, hy_conv_w, hy_conv_b, hy_f_w1, hy_f_b1, hy_f_w2, hy_f_b2, hy_f_w3, hy_f_freq, hy_skip, hy_w_out, s5_A_re, s5_A_im, s5_log_dt, s5_B_re, s5_B_im, s5_C_re, s5_C_im, s5_D, s5_w_glu, ffn_w_up, ffn_conv_w, ffn_conv_b, ffn_w_down):
    seq = x.shape[1]
    mod = _ada_mod(c, ada_w, ada_b)
    cos, sin = _rope_tables(seq)
    consts = _dft_constants(seq)
    for i in range(DEPTH):
        m, j = i % N_MIXERS, i // N_MIXERS
        if m == 0:
            x = _attention_layer(x, mod[i], norm1_g[i], attn_w_qkv[j], attn_w_o[j],
                                 attn_q_gain[j], attn_k_gain[j], cos, sin)
        elif m == 1:
            kf = _hyena_filter_spectrum(seq, consts, hy_f_w1[j], hy_f_b1[j], hy_f_w2[j], hy_f_b2[j],
                                        hy_f_w3[j], hy_f_freq[j])
            x = _hyena_layer(x, mod[i], norm1_g[i], hy_w_in[j], hy_conv_w[j], hy_conv_b[j], kf,
                             hy_skip[j], hy_w_out[j], consts)
        else:
            x = _s5_layer(x, mod[i], norm1_g[i], s5_A_re[j], s5_A_im[j], s5_log_dt[j], s5_B_re[j],
                          s5_B_im[j], s5_C_re[j], s5_C_im[j], s5_D[j], s5_w_glu[j])
        x = _conv_ffn_layer(x, mod[i], norm2_g[i], ffn_w_up[i], ffn_conv_w[i], ffn_conv_b[i],
                            ffn_w_down[i], final_g, final_norm=(i == DEPTH - 1))
    return x
```

```python
import functools
import math

import numpy as np
import jax
import jax.numpy as jnp
from jax import lax
from jax.experimental import pallas as pl
from jax.experimental.pallas import tpu as pltpu

D_MODEL = 1024
DEPTH = 4
N_MIXERS = 3
GRID_W = 64
EPS = 1e-6
N_HEADS = 16
HEAD_DIM = D_MODEL // N_HEADS
N_KV_HEADS = 4
KV_GROUP = N_HEADS // N_KV_HEADS
ROPE_THETA = 10000.0
HY_ORDER = 2
HY_EMB_DIM = 33
HY_FILTER_WIDTH = 64
HY_FAST_DECAY = 0.3
HY_SLOW_DECAY = 1.5
HY_TARGET = 1e-2
S5_GROUP = 16
S5_GROUPS = D_MODEL // S5_GROUP
S5_STATE = 64
D_FF = 2816

F32 = jnp.float32
BF16 = jnp.bfloat16
HIGHEST = lax.Precision.HIGHEST

LANES = 128
SUBLANES = 8
BF16_ROWS = 16
VMEM_LIMIT = 56 * 1024 * 1024

ROW_TILE = 512
HALO = BF16_ROWS
HY_CHUNK = 512
Q_TILE = 256
KV_TILE = 1024
DFT_N2 = 128
DFT_STEP = SUBLANES
DFT_LANES = 512
S5_SLAB = 128
S5_STEPS = 128


def _params(*sem):
    return pltpu.CompilerParams(dimension_semantics=sem, vmem_limit_bytes=VMEM_LIMIT)


def _resident(shape):
    zeros = (0,) * len(shape)
    return pl.BlockSpec(shape, lambda *_: zeros, pipeline_mode=pl.Buffered(1))


def _norm_mod(x, g, scale, shift):
    y = x * lax.rsqrt(jnp.mean(x * x, axis=-1, keepdims=True) + EPS)
    return (y * g) * (1.0 + scale) + shift


def _ada_kernel(c_ref, w_ref, b_ref, o_ref):
    c = c_ref[...]
    a = c * jax.nn.sigmoid(c)
    o_ref[0] = jnp.dot(a, w_ref[0], precision=HIGHEST, preferred_element_type=F32) + b_ref[0]


def _ada_mod(c, ada_w, ada_b):
    bsz, d = c.shape
    rows = -(-bsz // SUBLANES) * SUBLANES
    cp = jnp.pad(c, ((0, rows - bsz), (0, 0)))
    tn = 1536
    out = pl.pallas_call(
        _ada_kernel,
        grid=(DEPTH, 6 * d // tn),
        in_specs=[
            pl.BlockSpec((rows, d), lambda i, j: (0, 0)),
            pl.BlockSpec((1, d, tn), lambda i, j: (i, 0, j)),
            pl.BlockSpec((1, 1, tn), lambda i, j: (i, 0, j)),
        ],
        out_specs=pl.BlockSpec((1, rows, tn), lambda i, j: (i, 0, j)),
        out_shape=jax.ShapeDtypeStruct((DEPTH, rows, 6 * d), F32),
        compiler_params=_params("arbitrary", "arbitrary"),
        name="ada_mod",
    )(cp, ada_w, ada_b.reshape(DEPTH, 1, 6 * d))
    return out[:, :bsz].reshape(DEPTH, bsz, 6, d)


def _qkv_kernel(x_ref, mod_ref, g_ref, w_ref, seg_ref, c_ref, s1_ref, s2_ref, qg_ref, kg_ref,
                q_ref, klo_ref, khi_ref, v_ref):
    m = mod_ref[0]
    h = _norm_mod(x_ref[0], g_ref[...], m[1:2], m[0:1])
    r = jnp.dot(h.astype(BF16), w_ref[...], preferred_element_type=F32)
    c = c_ref[...]
    s1 = s1_ref[...]
    s2 = s2_ref[...]
    seg = seg_ref[...]
    quarter = HEAD_DIM // 2

    def norm_rope(xr, gain):
        ss = xr * xr
        hi = ss.astype(BF16)
        lo = (ss - hi.astype(F32)).astype(BF16)
        tot = (jnp.dot(hi, seg, preferred_element_type=F32) + jnp.dot(lo, seg, preferred_element_type=F32))
        n = xr * lax.rsqrt(tot * (1.0 / HEAD_DIM) + EPS) * gain
        return (n * c + pltpu.roll(n, LANES - quarter, axis=1) * s1 + pltpu.roll(n, quarter, axis=1) * s2)

    qg = qg_ref[...]
    kg = kg_ref[...]
    n_q = N_HEADS * HEAD_DIM // LANES
    for p in range(n_q):
        q_ref[0, p] = norm_rope(r[:, p * LANES:(p + 1) * LANES], qg).astype(BF16)
    low = lax.broadcasted_iota(jnp.int32, (r.shape[0], LANES), 1) < HEAD_DIM
    for g in range(N_KV_HEADS):
        kk = norm_rope(r[:, (n_q + g) * LANES:(n_q + g + 1) * LANES], kg)
        klo_ref[0, g] = jnp.where(low, kk, 0.0).astype(BF16)
        khi_ref[0, g] = jnp.where(low, 0.0, kk).astype(BF16)
        vv = r[:, (n_q + N_KV_HEADS + g) * LANES:(n_q + N_KV_HEADS + g + 1) * LANES]
        v_ref[0, g] = jnp.where(low, vv, 1.0).astype(BF16)


def _flash_kernel(q_ref, qn_ref, klo_ref, khi_ref, v_ref, o_ref, sa_ref, sb_ref, m_ref, acc_ref, *, n_kv):
    rows = KV_GROUP * Q_TILE
    m_ref[...] = jnp.full(m_ref.shape, -jnp.inf, F32)
    acc_ref[...] = jnp.zeros(acc_ref.shape, F32)

    def scores(s_ref, src_ref, j):
        start = pl.multiple_of(j * KV_TILE, KV_TILE)
        qq = src_ref[0].reshape(rows // 2, LANES)
        for half, kk_ref in enumerate((klo_ref, khi_ref)):
            kj = kk_ref[0, 0, pl.ds(start, KV_TILE), :]
            s_ref[half * (rows // 2):(half + 1) * (rows // 2)] = lax.dot_general(
                qq, kj, (((1,), (1,)), ((), ())), preferred_element_type=F32)

    def consume(s_ref, j):
        s = s_ref[...]
        start = pl.multiple_of(j * KV_TILE, KV_TILE)
        vj = v_ref[0, 0, pl.ds(start, KV_TILE), :]
        m_prev = m_ref[...]
        m_next = jnp.maximum(m_prev, jnp.max(s, axis=-1, keepdims=True))
        alpha = jnp.exp(m_prev - m_next)
        p = jnp.concatenate([jnp.exp(s[:, c * LANES:(c + 1) * LANES] - m_next)
                             for c in range(KV_TILE // LANES)], axis=1)
        acc_ref[...] = alpha * acc_ref[...] + jnp.dot(p.astype(BF16), vj, preferred_element_type=F32)
        m_ref[...] = m_next

    @pl.when(pl.program_id(2) == 0)
    def _():
        scores(sa_ref, q_ref, 0)

    def body(jj, carry):
        scores(sb_ref, q_ref, 2 * jj + 1)
        consume(sa_ref, 2 * jj)
        scores(sa_ref, q_ref, 2 * jj + 2)
        consume(sb_ref, 2 * jj + 1)
        return carry

    lax.fori_loop(0, n_kv // 2 - 1, body, 0)
    scores(sb_ref, q_ref, n_kv - 1)
    consume(sa_ref, n_kv - 2)
    scores(sa_ref, qn_ref, 0)
    consume(sb_ref, n_kv - 1)
    acc = acc_ref[...]
    o = acc[:, :HEAD_DIM] / acc[:, HEAD_DIM:HEAD_DIM + 1]
    o_ref[0] = jnp.concatenate(
        [o[g * Q_TILE:(g + 1) * Q_TILE] for g in ((j % 2) * (KV_GROUP // 2) + j // 2 for j in range(KV_GROUP))],
        axis=-1).astype(o_ref.dtype)


def _rope_tables(seq):
    rows = seq // GRID_W
    n_freq = HEAD_DIM // 4
    inv = 1.0 / (ROPE_THETA ** (jnp.arange(n_freq, dtype=F32) / n_freq))
    r = jnp.arange(rows, dtype=F32)
    col = jnp.arange(GRID_W, dtype=F32)
    ang_r = jnp.broadcast_to(r[:, None, None] * inv, (rows, GRID_W, n_freq))
    ang_c = jnp.broadcast_to(col[None, :, None] * inv, (rows, GRID_W, n_freq))
    ang = jnp.concatenate([ang_r, ang_c], axis=-1).reshape(seq, 2 * n_freq)
    return jnp.cos(ang), jnp.sin(ang)


def _attention_layer(x, mod_l, norm_g, w_qkv, q_gain, k_gain, cos, sin):
    bsz, seq, d = x.shape
    bm = min(ROW_TILE, seq)
    perm = np.concatenate([np.arange(0, HEAD_DIM, 2), np.arange(1, HEAD_DIM, 2)])
    head_cols = lambda hh: hh * HEAD_DIM + perm
    k0 = N_HEADS * HEAD_DIM
    v0 = k0 + N_KV_HEADS * HEAD_DIM
    q_cols = np.concatenate([head_cols(hh) for hh in range(N_HEADS)])
    k_cols = np.concatenate([np.tile(k0 + head_cols(g), 2) for g in range(N_KV_HEADS)])
    w_v = jnp.pad(w_qkv[:, v0:].reshape(d, N_KV_HEADS, HEAD_DIM), ((0, 0), (0, 0), (0, LANES - HEAD_DIM)))
    w = jnp.concatenate([w_qkv[:, q_cols], w_qkv[:, k_cols], w_v.reshape(d, N_KV_HEADS * LANES)],
                        axis=1).astype(BF16)
    n_out = w.shape[1]
    reps = LANES // (HEAD_DIM // 2)
    zero = jnp.zeros_like(sin)
    c_t = jnp.tile(cos, (1, reps))
    s1_t = jnp.tile(jnp.concatenate([-sin, zero], axis=1), (1, reps // 2))
    s2_t = jnp.tile(jnp.concatenate([zero, sin], axis=1), (1, reps // 2))
    seg = jnp.asarray(np.kron(np.eye(LANES // HEAD_DIM), np.ones((HEAD_DIM, HEAD_DIM))), dtype=BF16)
    tab = pl.BlockSpec((bm, LANES), lambda b, i: (i, 0))
    n_pairs = N_HEADS * HEAD_DIM // LANES
    kv_spec = pl.BlockSpec((1, N_KV_HEADS, bm, LANES), lambda b, i: (b, 0, i, 0))
    kv_shape = jax.ShapeDtypeStruct((bsz, N_KV_HEADS, seq, LANES), BF16)
    q, k_lo, k_hi, v = pl.pallas_call(
        _qkv_kernel,
        grid=(bsz, seq // bm),
        in_specs=[
            pl.BlockSpec((1, bm, d), lambda b, i: (b, i, 0)),
            pl.BlockSpec((1, 6, d), lambda b, i: (b, 0, 0)),
            _resident((1, d)),
            _resident((d, n_out)),
            _resident((LANES, LANES)),
            tab,
            tab,
            tab,
            _resident((1, LANES)),
            _resident((1, LANES)),
        ],
        out_specs=[
            pl.BlockSpec((1, n_pairs, bm, LANES), lambda b, i: (b, 0, i, 0)),
            kv_spec,
            kv_spec,
            kv_spec,
        ],
        out_shape=[
            jax.ShapeDtypeStruct((bsz, n_pairs, seq, LANES), BF16),
            kv_shape,
            kv_shape,
            kv_shape,
        ],
        compiler_params=_params("parallel", "parallel"),
        name="attn_qkv",
    )(x, mod_l, norm_g.reshape(1, d), w, seg, c_t, s1_t, s2_t,
      jnp.tile(q_gain[perm] * HEAD_DIM ** -0.5, 2).reshape(1, LANES), jnp.tile(k_gain[perm], 2).reshape(1, LANES))

    return _flash_attention(q, k_lo, k_hi, v)


def _flash_attention(q, k_lo, k_hi, v):
    bsz, _, seq, _ = q.shape
    assert (seq // KV_TILE) % 2 == 0, seq
    rows = KV_GROUP * Q_TILE
    n_q = seq // Q_TILE
    kv_spec = pl.BlockSpec((1, 1, seq, LANES), lambda b, g, i: (b, g, 0, 0))
    return pl.pallas_call(
        functools.partial(_flash_kernel, n_kv=seq // KV_TILE),
        grid=(bsz, N_KV_HEADS, n_q),
        in_specs=[
            pl.BlockSpec((1, KV_GROUP // 2, Q_TILE, LANES), lambda b, g, i: (b, g, i, 0)),
            pl.BlockSpec((1, KV_GROUP // 2, Q_TILE, LANES), lambda b, g, i: (b, g, jnp.minimum(i + 1, n_q - 1), 0)),
            kv_spec,
            kv_spec,
            kv_spec,
        ],
        out_specs=pl.BlockSpec((1, Q_TILE, KV_GROUP * HEAD_DIM), lambda b, g, i: (b, i, g)),
        out_shape=jax.ShapeDtypeStruct((bsz, seq, N_HEADS * HEAD_DIM), BF16),
        scratch_shapes=[
            pltpu.VMEM((rows, KV_TILE), F32),
            pltpu.VMEM((rows, KV_TILE), F32),
            pltpu.VMEM((rows, LANES), F32),
            pltpu.VMEM((rows, 2 * HEAD_DIM), F32),
        ],
        compiler_params=_params("parallel", "parallel", "arbitrary"),
        name="attn_flash",
    )(q, q, k_lo, k_hi, v)


def _halo_rows(i, bm, seq):
    t = i * bm - HALO + lax.broadcasted_iota(jnp.int32, (bm + 2 * HALO, 1), 0)
    return (t >= 0) & (t < seq)


def _fill_window(hs_ref, xp_ref, x_ref, xn_ref, g, scale, shift, bm):
    hs_ref[0:HALO] = _norm_mod(xp_ref[0], g, scale, shift).astype(BF16)
    hs_ref[HALO:HALO + bm] = _norm_mod(x_ref[0], g, scale, shift).astype(BF16)
    hs_ref[HALO + bm:] = _norm_mod(xn_ref[0], g, scale, shift).astype(BF16)


def _conv3(win_ref, cw, cb, bm):
    return (win_ref[pl.ds(HALO - 1, bm), :] * cw[0:1] + win_ref[pl.ds(HALO, bm), :] * cw[1:2]
            + win_ref[pl.ds(HALO + 1, bm), :] * cw[2:3] + cb)


def _ffn_tail(x_main, m, hs_ref, gs_ref, wu_ref, cw_ref, cb_ref, wd_ref, fg_ref, o_ref, *, bm, seq, final_norm):
    valid = _halo_rows(pl.program_id(1), bm, seq)
    gate = jnp.dot(hs_ref[...], wu_ref[:, :D_FF], preferred_element_type=F32)
    gs_ref[...] = jnp.where(valid, gate, 0.0)
    gate = _conv3(gs_ref, cw_ref[...], cb_ref[...], bm)
    val = jnp.dot(hs_ref[HALO:HALO + bm], wu_ref[:, D_FF:], preferred_element_type=F32)
    a = (gate * jax.nn.sigmoid(gate)) * val
    y = x_main + m[5:6] * jnp.dot(a.astype(BF16), wd_ref[...], preferred_element_type=F32)
    if final_norm:
        y = y * lax.rsqrt(jnp.mean(y * y, axis=-1, keepdims=True) + EPS) * fg_ref[...]
    o_ref[0] = y


def _ffn_kernel(xp_ref, x_ref, xn_ref, mod_ref, g_ref, wu_ref, cw_ref, cb_ref, wd_ref, fg_ref,
                o_ref, hs_ref, gs_ref, *, bm, seq, final_norm):
    m = mod_ref[0]
    _fill_window(hs_ref, xp_ref, x_ref, xn_ref, g_ref[...], m[4:5], m[3:4], bm)
    _ffn_tail(x_ref[0], m, hs_ref, gs_ref, wu_ref, cw_ref, cb_ref, wd_ref, fg_ref, o_ref,
              bm=bm, seq=seq, final_norm=final_norm)


def _proj_ffn_kernel(ap_ref, a_ref, an_ref, wp_ref, xp_ref, x_ref, xn_ref, mod_ref, g_ref, wu_ref, cw_ref,
                     cb_ref, wd_ref, fg_ref, o_ref, hs_ref, gs_ref, aw_ref, xw_ref, *, bm, seq, final_norm):
    m = mod_ref[0]
    aw_ref[0:HALO] = ap_ref[0].astype(BF16)
    aw_ref[HALO:HALO + bm] = a_ref[0].astype(BF16)
    aw_ref[HALO + bm:] = an_ref[0].astype(BF16)
    xw_ref[0:HALO] = xp_ref[0]
    xw_ref[HALO:HALO + bm] = x_ref[0]
    xw_ref[HALO + bm:] = xn_ref[0]
    xw_ref[...] = xw_ref[...] + m[2:3] * jnp.dot(aw_ref[...], wp_ref[...], preferred_element_type=F32)
    hs_ref[...] = _norm_mod(xw_ref[...], g_ref[...], m[4:5], m[3:4]).astype(BF16)
    _ffn_tail(xw_ref[HALO:HALO + bm], m, hs_ref, gs_ref, wu_ref, cw_ref, cb_ref, wd_ref, fg_ref, o_ref,
              bm=bm, seq=seq, final_norm=final_norm)


def _halo_specs(bm, seq, d):
    nh = seq // HALO
    per = bm // HALO
    return [
        pl.BlockSpec((1, HALO, d), lambda b, i: (b, jnp.maximum(i * per - 1, 0), 0)),
        pl.BlockSpec((1, bm, d), lambda b, i: (b, i, 0)),
        pl.BlockSpec((1, HALO, d), lambda b, i: (b, jnp.minimum((i + 1) * per, nh - 1), 0)),
    ]


def _conv_ffn_layer(x, mod_l, norm_g, w_up, conv_w, conv_b, w_down, final_g, final_norm, mixer_out=None):
    bsz, seq, d = x.shape
    bm = min(ROW_TILE, seq)
    win = bm + 2 * HALO
    ffn_specs = _halo_specs(bm, seq, d) + [
        pl.BlockSpec((1, 6, d), lambda b, i: (b, 0, 0)),
        _resident((1, d)),
        _resident((d, 2 * D_FF)),
        _resident((3, D_FF)),
        _resident((1, D_FF)),
        _resident((D_FF, d)),
        _resident((1, d)),
    ]
    ffn_args = (x, x, x, mod_l, norm_g.reshape(1, d), w_up.astype(BF16), conv_w, conv_b.reshape(1, D_FF),
                w_down.astype(BF16), final_g.reshape(1, d))
    scratch = [pltpu.VMEM((win, d), BF16), pltpu.VMEM((win, D_FF), F32)]
    if mixer_out is None:
        body, specs, args = _ffn_kernel, ffn_specs, ffn_args
    else:
        a, w = mixer_out
        kdim = a.shape[-1]
        body = _proj_ffn_kernel
        specs = _halo_specs(bm, seq, kdim) + [_resident((kdim, d))] + ffn_specs
        args = (a, a, a, w.astype(BF16)) + ffn_args
        scratch = scratch + [pltpu.VMEM((win, kdim), BF16), pltpu.VMEM((win, d), F32)]
    return pl.pallas_call(
        functools.partial(body, bm=bm, seq=seq, final_norm=final_norm),
        grid=(bsz, seq // bm),
        in_specs=specs,
        out_specs=pl.BlockSpec((1, bm, d), lambda b, i: (b, i, 0)),
        out_shape=jax.ShapeDtypeStruct(x.shape, F32),
        scratch_shapes=scratch,
        compiler_params=_params("parallel", "parallel"),
        name="conv_ffn",
    )(*args)


def _hy_in_kernel(xp_ref, x_ref, xn_ref, mod_ref, g_ref, w_ref, cw_ref, cb_ref,
                  v_ref, x1_ref, x2_ref, hs_ref, us_ref, *, bm, seq):
    m = mod_ref[0]
    _fill_window(hs_ref, xp_ref, x_ref, xn_ref, g_ref[...], m[1:2], m[0:1], bm)
    valid = _halo_rows(pl.program_id(1), bm, seq)
    outs = (v_ref, x1_ref, x2_ref)
    for j in range(3 * D_MODEL // HY_CHUNK):
        c0 = j * HY_CHUNK
        u = jnp.dot(hs_ref[...], w_ref[:, c0:c0 + HY_CHUNK], preferred_element_type=F32)
        us_ref[...] = jnp.where(valid, u, 0.0)
        y = _conv3(us_ref, cw_ref[:, c0:c0 + HY_CHUNK], cb_ref[:, c0:c0 + HY_CHUNK], bm)
        off = c0 % D_MODEL
        outs[c0 // D_MODEL][0, :, off:off + HY_CHUNK] = y


def _hy_filter_kernel(z_ref, w1_ref, b1_ref, w2_ref, b2_ref, w3_ref, fr_ref, dl_ref, k_ref, n_ref,
                      *, bm, seq):
    i = pl.program_id(0)
    z = z_ref[...]
    fr = fr_ref[...]
    a = jnp.sin(fr * (jnp.dot(z, w1_ref[...], precision=HIGHEST, preferred_element_type=F32) + b1_ref[...]))
    a = jnp.sin(fr * (jnp.dot(a, w2_ref[...], precision=HIGHEST, preferred_element_type=F32) + b2_ref[...]))
    decay = jnp.exp(-z[:, 0:1] * dl_ref[...])
    t = i * bm + lax.broadcasted_iota(jnp.int32, (bm, 1), 0)
    decay = jnp.where(t == seq, 0.0, decay)

    @pl.when(i == 0)
    def _():
        n_ref[...] = jnp.zeros(n_ref.shape, F32)

    for o in range(HY_ORDER):
        f = jnp.dot(a, w3_ref[0, o], precision=HIGHEST, preferred_element_type=F32) * decay
        k_ref[o] = f
        n_ref[o:o + 1] += jnp.sum(jnp.abs(f), axis=0, keepdims=True)


def _dft_dot(w_ref, x):
    return jnp.dot(w_ref[...], x.astype(BF16), preferred_element_type=F32)


def _dft_in_kernel(x_ref, w_ref, o_ref, *, n1):
    for s in range(DFT_STEP):
        r = _dft_dot(w_ref, jnp.concatenate([x_ref[0, :, s, :], x_ref[1, :, s, :]], axis=0))
        o_ref[0, s, 0] = r[:n1]
        o_ref[0, s, 1] = r[n1:]


def _dft_filter_in_kernel(x_ref, w_ref, o_ref, *, n1):
    for s in range(DFT_STEP):
        r = _dft_dot(w_ref, jnp.concatenate([x_ref[0, :n1 // 2, s, :], x_ref[0, n1 // 2:, s, :]], axis=0))
        o_ref[0, s, 0] = r[:n1]
        o_ref[0, s, 1] = r[n1:]


def _twiddle(ar, ai, twr, twi):
    return ar * twr - ai * twi, ar * twi + ai * twr


def _dft_filter_mid_kernel(a_ref, w_ref, n_ref, o_ref):
    inv = 1.0 / n_ref[0]
    for j in range(DFT_STEP):
        z = _dft_dot(w_ref.at[j], jnp.concatenate([a_ref[0, :, 0, j, :], a_ref[0, :, 1, j, :]], axis=0))
        o_ref[0, 0, j] = z[:DFT_N2] * inv
        o_ref[0, 1, j] = z[DFT_N2:] * inv


def _dft_mid_kernel(a_ref, kf_ref, w_ref, wc_ref, o_ref):
    for j in range(DFT_STEP):
        z = _dft_dot(w_ref.at[j], jnp.concatenate([a_ref[0, :, 0, j, :], a_ref[0, :, 1, j, :]], axis=0))
        yr, yi = _twiddle(z[:DFT_N2], z[DFT_N2:], kf_ref[0, 0, j], kf_ref[0, 1, j])
        b = _dft_dot(wc_ref.at[j], jnp.concatenate([yr, yi], axis=0))
        o_ref[0, j, 0] = b[:DFT_N2]
        o_ref[0, j, 1] = b[DFT_N2:]


def _dft_out_kernel(b_ref, w_ref, gate_ref, zc_ref, skip_ref, o_ref, *, half):
    skip = skip_ref[...]
    for s in range(DFT_STEP):
        y = _dft_dot(w_ref, jnp.concatenate([b_ref[0, :, 0, s, :], b_ref[0, :, 1, s, :]], axis=0))
        o_ref[0, :, s, :] = gate_ref[0, :, s, :] * (y[:half] + skip * zc_ref[0, :, s, :])
        o_ref[1, :, s, :] = gate_ref[1, :, s, :] * (y[half:] + skip * zc_ref[1, :, s, :])


def _dft_constants(seq):
    n = 2 * seq
    n2 = DFT_N2
    n1 = n // n2
    half = n1 // 2
    k1 = np.arange(n1, dtype=np.float64)
    f1 = np.exp(-2j * np.pi * np.outer(k1, k1) / n1)

    def c2(m):
        return np.block([[m.real, -m.imag], [m.imag, m.real]])

    w_in = c2(f1[:, :half])
    w_fin = np.concatenate([f1.real, f1.imag], axis=0)
    w_out = c2(np.conj(f1)[:half, :]) / n
    as16 = lambda a: jnp.asarray(a, dtype=F32).astype(BF16)
    i1 = jnp.arange(n1, dtype=jnp.int32)[:, None, None]
    i2 = jnp.arange(n2, dtype=jnp.int32)
    phase = (i1 * i2[None, None, :] + n1 * (i2[None, :, None] * i2[None, None, :])) % n
    ang = phase.astype(F32) * (-2.0 * math.pi / n)
    mr, mi = jnp.cos(ang), jnp.sin(ang)
    w_mid = jnp.concatenate([jnp.concatenate([mr, -mi], axis=2), jnp.concatenate([mi, mr], axis=2)], axis=1)
    return dict(n1=n1, half=half, w_in=as16(w_in), w_fin=as16(w_fin), w_out=as16(w_out),
                w_mid=w_mid.astype(BF16), w_mid_c=jnp.swapaxes(w_mid, 1, 2).astype(BF16))


def _hyena_pos_features(seq):
    t = jnp.linspace(0.0, 1.0, seq, dtype=F32)[:, None]
    bands = (HY_EMB_DIM - 1) // 2
    w = 2.0 * math.pi * jnp.arange(seq, dtype=F32) / seq
    f = jnp.linspace(1e-4, bands - 1, bands, dtype=F32)
    ang = w[:, None] * f[None, :]
    z = jnp.concatenate([t, jnp.cos(ang), -jnp.sin(ang)], axis=-1)
    deltas = jnp.abs(jnp.linspace(math.log(HY_TARGET) / HY_SLOW_DECAY,
                                  math.log(HY_TARGET) / HY_FAST_DECAY, D_MODEL, dtype=F32))
    return z, deltas


def _hyena_filter_spectrum(seq, consts, f_w1, f_b1, f_w2, f_b2, f_w3, f_freq):
    d = D_MODEL
    n = 2 * seq
    n1, n2 = consts["n1"], DFT_N2
    z, deltas = _hyena_pos_features(seq)
    z2 = jnp.concatenate([z, z[:1], z[1:][::-1]], axis=0)
    kpad = HY_FILTER_WIDTH - HY_EMB_DIM
    z2 = jnp.pad(z2, ((0, 0), (0, kpad)))
    w1 = jnp.pad(f_w1, ((0, kpad), (0, 0)))
    bm = min(ROW_TILE, seq)
    nblk = n // bm
    k2, norm = pl.pallas_call(
        functools.partial(_hy_filter_kernel, bm=bm, seq=seq),
        grid=(nblk,),
        in_specs=[
            pl.BlockSpec((bm, HY_FILTER_WIDTH), lambda i: (i, 0)),
            _resident((HY_FILTER_WIDTH, HY_FILTER_WIDTH)),
            _resident((1, HY_FILTER_WIDTH)),
            _resident((HY_FILTER_WIDTH, HY_FILTER_WIDTH)),
            _resident((1, HY_FILTER_WIDTH)),
            pl.BlockSpec((1, HY_ORDER, HY_FILTER_WIDTH, d), lambda i: ((2 * i) // nblk, 0, 0, 0)),
            _resident((1, HY_FILTER_WIDTH)),
            _resident((1, d)),
        ],
        out_specs=[
            pl.BlockSpec((HY_ORDER, bm, d), lambda i: (0, i, 0)),
            pl.BlockSpec((HY_ORDER, d), lambda i: (0, 0)),
        ],
        out_shape=[
            jax.ShapeDtypeStruct((HY_ORDER, n, d), F32),
            jax.ShapeDtypeStruct((HY_ORDER, d), F32),
        ],
        compiler_params=_params("arbitrary"),
        name="hy_filter",
    )(z2, w1, f_b1.reshape(1, -1), f_w2, f_b2.reshape(1, -1),
      f_w3.reshape(HY_FILTER_WIDTH, HY_ORDER, 2, d).transpose(2, 1, 0, 3), f_freq.reshape(1, -1),
      deltas.reshape(1, d))

    st, ct = DFT_STEP, DFT_LANES
    a = pl.pallas_call(
        functools.partial(_dft_filter_in_kernel, n1=n1),
        grid=(HY_ORDER, n2 // st),
        in_specs=[
            pl.BlockSpec((1, n1, st, d), lambda o, j: (o, 0, j, 0)),
            _resident((2 * n1, n1)),
        ],
        out_specs=pl.BlockSpec((1, st, 2, n1, d), lambda o, j: (o, j, 0, 0, 0)),
        out_shape=jax.ShapeDtypeStruct((HY_ORDER, n2, 2, n1, d), F32),
        compiler_params=_params("parallel", "parallel"),
        name="hy_filter_dft_in",
    )(k2.reshape(HY_ORDER, n1, n2, d), consts["w_fin"])

    return pl.pallas_call(
        _dft_filter_mid_kernel,
        grid=(HY_ORDER, n1 // st, d // ct),
        in_specs=[
            pl.BlockSpec((1, n2, 2, st, ct), lambda o, k, c: (o, 0, 0, k, c)),
            pl.BlockSpec((st, 2 * n2, 2 * n2), lambda o, k, c: (k, 0, 0)),
            pl.BlockSpec((1, 1, ct), lambda o, k, c: (o, 0, c)),
        ],
        out_specs=pl.BlockSpec((1, 2, st, n2, ct), lambda o, k, c: (o, 0, k, 0, c)),
        out_shape=jax.ShapeDtypeStruct((HY_ORDER, 2, n1, n2, d), F32),
        compiler_params=_params("parallel", "parallel", "parallel"),
        name="hy_filter_dft_mid",
    )(a, consts["w_mid"], norm.reshape(HY_ORDER, 1, d))


def _long_conv_gate(zc, gate, kf, order, skip, consts):
    bsz, seq, d = zc.shape
    n1, half, n2 = consts["n1"], consts["half"], DFT_N2
    pairs = bsz // 2
    st, ct = DFT_STEP, DFT_LANES
    zv = zc.reshape(bsz, half, n2, d)
    a = pl.pallas_call(
        functools.partial(_dft_in_kernel, n1=n1),
        grid=(pairs, n2 // st),
        in_specs=[
            pl.BlockSpec((2, half, st, d), lambda p, j: (p, 0, j, 0)),
            _resident((2 * n1, n1)),
        ],
        out_specs=pl.BlockSpec((1, st, 2, n1, d), lambda p, j: (p, j, 0, 0, 0)),
        out_shape=jax.ShapeDtypeStruct((pairs, n2, 2, n1, d), F32),
        compiler_params=_params("parallel", "parallel"),
        name="hy_dft_in",
    )(zv, consts["w_in"])

    b = pl.pallas_call(
        _dft_mid_kernel,
        grid=(n1 // st, d // ct, pairs),
        in_specs=[
            pl.BlockSpec((1, n2, 2, st, ct), lambda k, c, p: (p, 0, 0, k, c)),
            pl.BlockSpec((1, 2, st, n2, ct), lambda k, c, p: (order, 0, k, 0, c)),
            pl.BlockSpec((st, 2 * n2, 2 * n2), lambda k, c, p: (k, 0, 0)),
            pl.BlockSpec((st, 2 * n2, 2 * n2), lambda k, c, p: (k, 0, 0)),
        ],
        out_specs=pl.BlockSpec((1, st, 2, n2, ct), lambda k, c, p: (p, k, 0, 0, c)),
        out_shape=jax.ShapeDtypeStruct((pairs, n1, 2, n2, d), F32),
        compiler_params=_params("parallel", "parallel", "arbitrary"),
        name="hy_dft_mid",
    )(a, kf, consts["w_mid"], consts["w_mid_c"])

    sig = pl.BlockSpec((2, half, st, ct), lambda p, j, c: (p, 0, j, c))
    out = pl.pallas_call(
        functools.partial(_dft_out_kernel, half=half),
        grid=(pairs, n2 // st, d // ct),
        in_specs=[
            pl.BlockSpec((1, n1, 2, st, ct), lambda p, j, c: (p, 0, 0, j, c)),
            _resident((n1, 2 * n1)),
            sig,
            sig,
            pl.BlockSpec((1, ct), lambda p, j, c: (0, c)),
        ],
        out_specs=sig,
        out_shape=jax.ShapeDtypeStruct((bsz, half, n2, d), F32),
        compiler_params=_params("parallel", "parallel", "parallel"),
        name="hy_dft_out",
    )(b, consts["w_out"], gate.reshape(bsz, half, n2, d), zv, skip.reshape(1, d))
    return out.reshape(bsz, seq, d)


def _hyena_layer(x, mod_l, norm_g, w_in, conv_w, conv_b, kf, skip, consts):
    bsz, seq, d = x.shape
    bm = min(ROW_TILE, seq)
    blk = pl.BlockSpec((1, bm, d), lambda b, i: (b, i, 0))
    shp = jax.ShapeDtypeStruct(x.shape, F32)
    v, x1, x2 = pl.pallas_call(
        functools.partial(_hy_in_kernel, bm=bm, seq=seq),
        grid=(bsz, seq // bm),
        in_specs=_halo_specs(bm, seq, d) + [
            pl.BlockSpec((1, 6, d), lambda b, i: (b, 0, 0)),
            _resident((1, d)),
            _resident((d, 3 * d)),
            _resident((3, 3 * d)),
            _resident((1, 3 * d)),
        ],
        out_specs=[blk, blk, blk],
        out_shape=[shp, shp, shp],
        scratch_shapes=[
            pltpu.VMEM((bm + 2 * HALO, d), BF16),
            pltpu.VMEM((bm + 2 * HALO, HY_CHUNK), F32),
        ],
        compiler_params=_params("parallel", "parallel"),
        name="hy_in",
    )(x, x, x, mod_l, norm_g.reshape(1, d), w_in.astype(BF16), conv_w, conv_b.reshape(1, 3 * d))
    zc = v
    for o, gate in enumerate((x1, x2)):
        zc = _long_conv_gate(zc, gate, kf, o, skip[o], consts)
    return zc


def _s5_prep_kernel(x_ref, mod_ref, g_ref, p_ref, o_ref):
    hs = []
    for b in range(x_ref.shape[0]):
        m = mod_ref[b]
        hs.append(_norm_mod(x_ref[b], g_ref[...], m[1:2], m[0:1]).astype(BF16))
    o_ref[...] = jnp.dot(p_ref[...], jnp.concatenate(hs, axis=0),
                         preferred_element_type=F32).astype(o_ref.dtype)


def _s5_scan_kernel(h_ref, bm_ref, cm_ref, lam_ref, pt_ref, o_ref, sa_ref, sb_ref, sc_ref, y_ref, carry_ref,
                    *, rows, reverse):
    n_slab = D_MODEL // S5_SLAB
    width = S5_SLAB // S5_GROUP * S5_STATE
    n_tiles = rows // SUBLANES
    half = SUBLANES // 2
    bufs = (sa_ref, sb_ref, sc_ref)

    @pl.when(pl.program_id(0) == 0)
    def _():
        carry_ref[...] = jnp.zeros(carry_ref.shape, F32)

    low = lax.broadcasted_iota(jnp.int32, (SUBLANES, width), 0) < half
    take_rolled = jnp.logical_not(low) if reverse else low
    low3 = lax.broadcasted_iota(jnp.int32, (n_tiles, SUBLANES, S5_SLAB), 1) < half
    gets_other = low3 if reverse else jnp.logical_not(low3)

    def project_in(k):
        u = h_ref[:, k * S5_SLAB:(k + 1) * S5_SLAB]
        u3 = u.astype(F32).reshape(n_tiles, SUBLANES, S5_SLAB)
        other = jnp.where(gets_other, pltpu.roll(u3, half, axis=1), 0.0).reshape(rows, S5_SLAB)
        lhs = jnp.concatenate([u, other.astype(BF16)], axis=1)
        bufs[k % 3][...] = jnp.dot(lhs, bm_ref[k], preferred_element_type=F32)

    def project_out(k):
        y_ref[:, k * S5_SLAB:(k + 1) * S5_SLAB] = jnp.dot(
            bufs[k % 3][...].astype(BF16), cm_ref[k], preferred_element_type=F32).astype(y_ref.dtype)

    def scan(k):
        s_ref = bufs[k % 3]
        lr = lam_ref[k, 0]
        li = lam_ref[k, 1]
        cr = carry_ref[k, 0]
        ci = carry_ref[k, 1]
        for i in (reversed(range(n_tiles)) if reverse else range(n_tiles)):
            r0 = i * SUBLANES
            vr = s_ref[r0:r0 + SUBLANES, 0:width]
            vi = s_ref[r0:r0 + SUBLANES, width:2 * width]
            pr = jnp.where(take_rolled, pltpu.roll(cr, half, axis=0), cr)
            pi = jnp.where(take_rolled, pltpu.roll(ci, half, axis=0), ci)
            cr = vr + (lr * pr - li * pi)
            ci = vi + (lr * pi + li * pr)
            s_ref[r0:r0 + SUBLANES, 0:width] = cr
            s_ref[r0:r0 + SUBLANES, width:2 * width] = ci
        carry_ref[k, 0] = cr
        carry_ref[k, 1] = ci

    project_in(0)
    for k in range(n_slab):
        if k + 1 < n_slab:
            project_in(k + 1)
        if k >= 1:
            project_out(k - 1)
        scan(k)
    project_out(n_slab - 1)
    y = jnp.dot(pt_ref[...], y_ref[...], preferred_element_type=F32)
    steps = rows // o_ref.shape[0]
    for b in range(o_ref.shape[0]):
        o_ref[b] = y[b * steps:(b + 1) * steps].astype(o_ref.dtype)


def _s5_glu_kernel(yf_ref, yb_ref, x_ref, mod_ref, g_ref, dsk_ref, w_ref, o_ref):
    m = mod_ref[0]
    x = x_ref[0]
    h = _norm_mod(x, g_ref[...], m[1:2], m[0:1])
    y = jax.nn.gelu(yf_ref[0].astype(F32) + yb_ref[0].astype(F32) + dsk_ref[...] * h)
    g = jnp.dot(y.astype(BF16), w_ref[...], preferred_element_type=F32)
    d = x.shape[-1]
    o_ref[0] = x + m[2:3] * (g[:, :d] * jax.nn.sigmoid(g[:, d:]))


def _s5_tables(a_re, a_im, log_dt, b_re, b_im, c_re, c_im):
    n_slab = D_MODEL // S5_SLAB
    gps = S5_SLAB // S5_GROUP
    lam = lax.complex(jnp.minimum(a_re.astype(F32), -1e-4), a_im.astype(F32))
    dt = jnp.exp(log_dt.astype(F32))[..., None]
    lam_bar = jnp.exp(lam * dt)
    b_bar = ((lam_bar - 1.0) / lam)[..., None] * lax.complex(b_re.astype(F32), b_im.astype(F32))
    eye = jnp.eye(gps, dtype=F32)

    def b_block(part):
        m = jnp.einsum("dkgph,gj->dkghjp", part, eye)
        return m.reshape(2, n_slab, S5_SLAB, gps * S5_STATE)

    def b_rows(b_c):
        bb = b_c.reshape(2, n_slab, gps, S5_STATE, S5_GROUP)
        return jnp.concatenate([b_block(bb.real), b_block(bb.imag)], axis=-1)

    b_mat = jnp.concatenate([b_rows(b_bar), b_rows(b_bar * lam_bar[..., None])], axis=-2)
    cc = lax.complex(c_re.astype(F32), c_im.astype(F32)).reshape(2, n_slab, gps, S5_GROUP, S5_STATE)

    def c_block(part):
        m = jnp.einsum("dkghp,gj->dkgpjh", part, eye)
        return m.reshape(2, n_slab, gps * S5_STATE, S5_SLAB)

    c_mat = jnp.concatenate([c_block(cc.real), -c_block(cc.imag)], axis=-2)
    lam1 = lam_bar.reshape(2, n_slab, 1, gps * S5_STATE)
    lam2 = lam1 * lam1
    half = SUBLANES // 2

    def rows(lo, hi):
        return jnp.concatenate([jnp.broadcast_to(lo, lo.shape[:2] + (half,) + lo.shape[3:]),
                                jnp.broadcast_to(hi, hi.shape[:2] + (half,) + hi.shape[3:])], axis=2)

    carry = jnp.stack([rows(lam1, lam2)[0], rows(lam2, lam1)[1]])
    lam_t = jnp.stack([carry.real, carry.imag], axis=2)
    return b_mat, c_mat, lam_t


def _s5_layer(x, mod_l, norm_g, a_re, a_im, log_dt, b_re, b_im, c_re, c_im, d_skip, w_glu):
    bsz, seq, d = x.shape
    assert 2 * bsz == SUBLANES, bsz
    bm = min(ROW_TILE, seq)
    n_slab = d // S5_SLAB
    width = S5_SLAB // S5_GROUP * S5_STATE
    b_mat, c_mat, lam_t = _s5_tables(a_re, a_im, log_dt, b_re, b_im, c_re, c_im)
    steps = min(S5_STEPS, seq)
    rows = steps * bsz
    nc = seq // steps
    b_mat = b_mat.astype(BF16)
    c_mat = c_mat.astype(BF16)
    r = np.arange(rows)
    perm_np = np.zeros((rows, rows), np.float32)
    perm_np[r, (r % bsz) * steps + r // bsz] = 1.0
    perm = jnp.asarray(perm_np, dtype=BF16)

    h2 = pl.pallas_call(
        _s5_prep_kernel,
        grid=(nc,),
        in_specs=[
            pl.BlockSpec((bsz, steps, d), lambda i: (0, i, 0)),
            _resident((bsz, 6, d)),
            _resident((1, d)),
            _resident((rows, rows)),
        ],
        out_specs=pl.BlockSpec((rows, d), lambda i: (i, 0)),
        out_shape=jax.ShapeDtypeStruct((seq * bsz, d), BF16),
        compiler_params=_params("parallel"),
        name="s5_prep",
    )(x, mod_l, norm_g.reshape(1, d), perm)

    def scan(direction):
        reverse = direction == 1
        chunk = (lambda c: nc - 1 - c) if reverse else (lambda c: c)
        return pl.pallas_call(
            functools.partial(_s5_scan_kernel, rows=rows, reverse=reverse),
            grid=(nc,),
            in_specs=[
                pl.BlockSpec((rows, d), lambda c: (chunk(c), 0)),
                _resident((n_slab, 2 * S5_SLAB, 2 * width)),
                _resident((n_slab, 2 * width, S5_SLAB)),
                _resident((n_slab, 2, SUBLANES, width)),
                _resident((rows, rows)),
            ],
            out_specs=pl.BlockSpec((bsz, steps, d), lambda c: (0, chunk(c), 0)),
            out_shape=jax.ShapeDtypeStruct((bsz, seq, d), BF16),
            scratch_shapes=[
                pltpu.VMEM((rows, 2 * width), F32),
                pltpu.VMEM((rows, 2 * width), F32),
                pltpu.VMEM((rows, 2 * width), F32),
                pltpu.VMEM((rows, d), BF16),
                pltpu.VMEM((n_slab, 2, SUBLANES, width), F32),
            ],
            compiler_params=_params("arbitrary"),
            name="s5_scan_bwd" if reverse else "s5_scan_fwd",
        )(h2, b_mat[direction], c_mat[direction], lam_t[direction], perm.T)

    y_f = scan(0)
    y_b = scan(1)
    return pl.pallas_call(
        _s5_glu_kernel,
        grid=(bsz, seq // bm),
        in_specs=[
            pl.BlockSpec((1, bm, d), lambda b, i: (b, i, 0)),
            pl.BlockSpec((1, bm, d), lambda b, i: (b, i, 0)),
            pl.BlockSpec((1, bm, d), lambda b, i: (b, i, 0)),
            pl.BlockSpec((1, 6, d), lambda b, i: (b, 0, 0)),
            _resident((1, d)),
            _resident((1, d)),
            _resident((d, 2 * d)),
        ],
        out_specs=pl.BlockSpec((1, bm, d), lambda b, i: (b, i, 0)),
        out_shape=jax.ShapeDtypeStruct(x.shape, F32),
        compiler_params=_params("parallel", "parallel"),
        name="s5_glu",
    )(y_f, y_b, x, mod_l, norm_g.reshape(1, d), d_skip.reshape(1, d), w_glu.astype(BF16))


def kernel(x, c, ada_w, ada_b, norm1_g, norm2_g, final_g, attn_w_qkv, attn_w_o, attn_q_gain, attn_k_gain, hy_w_in, hy_conv_w, hy_conv_b, hy_f_w1, hy_f_b1, hy_f_w2, hy_f_b2, hy_f_w3, hy_f_freq, hy_skip, hy_w_out, s5_A_re, s5_A_im, s5_log_dt, s5_B_re, s5_B_im, s5_C_re, s5_C_im, s5_D, s5_w_glu, ffn_w_up, ffn_conv_w, ffn_conv_b, ffn_w_down):
    seq = x.shape[1]
    mod = _ada_mod(c, ada_w, ada_b)
    cos, sin = _rope_tables(seq)
    consts = _dft_constants(seq)
    for i in range(DEPTH):
        m, j = i % N_MIXERS, i // N_MIXERS
        mixer_out = None
        if m == 0:
            o = _attention_layer(x, mod[i], norm1_g[i], attn_w_qkv[j], attn_q_gain[j], attn_k_gain[j], cos, sin)
            mixer_out = (o, attn_w_o[j])
        elif m == 1:
            kf = _hyena_filter_spectrum(seq, consts, hy_f_w1[j], hy_f_b1[j], hy_f_w2[j], hy_f_b2[j],
                                        hy_f_w3[j], hy_f_freq[j])
            zc = _hyena_layer(x, mod[i], norm1_g[i], hy_w_in[j], hy_conv_w[j], hy_conv_b[j], kf,
                              hy_skip[j], consts)
            mixer_out = (zc, hy_w_out[j])
        else:
            x = _s5_layer(x, mod[i], norm1_g[i], s5_A_re[j], s5_A_im[j], s5_log_dt[j], s5_B_re[j],
                          s5_B_im[j], s5_C_re[j], s5_C_im[j], s5_D[j], s5_w_glu[j])
        x = _conv_ffn_layer(x, mod[i], norm2_g[i], ffn_w_up[i], ffn_conv_w[i], ffn_conv_b[i],
                            ffn_w_down[i], final_g, final_norm=(i == DEPTH - 1), mixer_out=mixer_out)
    return x
```

```python
import functools
import math

import numpy as np
import jax
import jax.numpy as jnp
from jax import lax
from jax.experimental import pallas as pl
from jax.experimental.pallas import tpu as pltpu

D_MODEL = 1024
DEPTH = 4
N_MIXERS = 3
GRID_W = 64
EPS = 1e-6
N_HEADS = 16
HEAD_DIM = D_MODEL // N_HEADS
N_KV_HEADS = 4
KV_GROUP = N_HEADS // N_KV_HEADS
ROPE_THETA = 10000.0
HY_ORDER = 2
HY_EMB_DIM = 33
HY_FILTER_WIDTH = 64
HY_FAST_DECAY = 0.3
HY_SLOW_DECAY = 1.5
HY_TARGET = 1e-2
S5_GROUP = 16
S5_GROUPS = D_MODEL // S5_GROUP
S5_STATE = 64
D_FF = 2816

F32 = jnp.float32
BF16 = jnp.bfloat16
HIGHEST = lax.Precision.HIGHEST

LANES = 128
SUBLANES = 8
BF16_ROWS = 16
VMEM_LIMIT = 56 * 1024 * 1024

ROW_TILE = 512
HALO = BF16_ROWS
HY_CHUNK = 512
Q_TILE = 256
KV_TILE = 1024
MAX_FIXED_SHIFT = 40.0
DFT_N2 = 128
DFT_STEP = SUBLANES
DFT_LANES = 512
S5_SLAB = 128
S5_STEPS = 128


def _params(*sem):
    return pltpu.CompilerParams(dimension_semantics=sem, vmem_limit_bytes=VMEM_LIMIT)


def _resident(shape):
    zeros = (0,) * len(shape)
    return pl.BlockSpec(shape, lambda *_: zeros, pipeline_mode=pl.Buffered(1))


def _norm_mod(x, g, scale, shift):
    y = x * lax.rsqrt(jnp.mean(x * x, axis=-1, keepdims=True) + EPS)
    return (y * g) * (1.0 + scale) + shift


def _ada_kernel(c_ref, w_ref, b_ref, o_ref):
    c = c_ref[...]
    a = c * jax.nn.sigmoid(c)
    o_ref[0] = jnp.dot(a, w_ref[0], precision=HIGHEST, preferred_element_type=F32) + b_ref[0]


def _ada_mod(c, ada_w, ada_b):
    bsz, d = c.shape
    rows = -(-bsz // SUBLANES) * SUBLANES
    cp = jnp.pad(c, ((0, rows - bsz), (0, 0)))
    tn = 1536
    out = pl.pallas_call(
        _ada_kernel,
        grid=(DEPTH, 6 * d // tn),
        in_specs=[
            pl.BlockSpec((rows, d), lambda i, j: (0, 0)),
            pl.BlockSpec((1, d, tn), lambda i, j: (i, 0, j)),
            pl.BlockSpec((1, 1, tn), lambda i, j: (i, 0, j)),
        ],
        out_specs=pl.BlockSpec((1, rows, tn), lambda i, j: (i, 0, j)),
        out_shape=jax.ShapeDtypeStruct((DEPTH, rows, 6 * d), F32),
        compiler_params=_params("arbitrary", "arbitrary"),
        name="ada_mod",
    )(cp, ada_w, ada_b.reshape(DEPTH, 1, 6 * d))
    return out[:, :bsz].reshape(DEPTH, bsz, 6, d)


def _qkv_kernel(x_ref, mod_ref, g_ref, w_ref, seg_ref, c_ref, s1_ref, s2_ref, qg_ref, kg_ref,
                q_ref, klo_ref, khi_ref, v_ref):
    m = mod_ref[0]
    h = _norm_mod(x_ref[0], g_ref[...], m[1:2], m[0:1])
    r = jnp.dot(h.astype(BF16), w_ref[...], preferred_element_type=F32)
    c = c_ref[...]
    s1 = s1_ref[...]
    s2 = s2_ref[...]
    seg = seg_ref[...]
    quarter = HEAD_DIM // 2

    def norm_rope(xr, gain):
        ss = xr * xr
        hi = ss.astype(BF16)
        lo = (ss - hi.astype(F32)).astype(BF16)
        tot = (jnp.dot(hi, seg, preferred_element_type=F32) + jnp.dot(lo, seg, preferred_element_type=F32))
        n = xr * lax.rsqrt(tot * (1.0 / HEAD_DIM) + EPS) * gain
        return (n * c + pltpu.roll(n, LANES - quarter, axis=1) * s1 + pltpu.roll(n, quarter, axis=1) * s2)

    qg = qg_ref[...]
    kg = kg_ref[...]
    n_q = N_HEADS * HEAD_DIM // LANES
    for p in range(n_q):
        q_ref[0, p] = norm_rope(r[:, p * LANES:(p + 1) * LANES], qg).astype(BF16)
    low = lax.broadcasted_iota(jnp.int32, (r.shape[0], LANES), 1) < HEAD_DIM
    for g in range(N_KV_HEADS):
        kk = norm_rope(r[:, (n_q + g) * LANES:(n_q + g + 1) * LANES], kg)
        klo_ref[0, g] = jnp.where(low, kk, 0.0).astype(BF16)
        khi_ref[0, g] = jnp.where(low, 0.0, kk).astype(BF16)
        vv = r[:, (n_q + N_KV_HEADS + g) * LANES:(n_q + N_KV_HEADS + g + 1) * LANES]
        v_ref[0, g] = jnp.where(low, vv, 1.0).astype(BF16)


def _flash_kernel(q_ref, qn_ref, klo_ref, khi_ref, v_ref, shift_ref, o_ref, sa_ref, sb_ref, m_ref, acc_ref,
                  *, n_kv, fixed_shift):
    rows = KV_GROUP * Q_TILE
    m_ref[...] = jnp.full(m_ref.shape, -jnp.inf, F32)
    acc_ref[...] = jnp.zeros(acc_ref.shape, F32)

    def scores(s_ref, src_ref, j):
        start = pl.multiple_of(j * KV_TILE, KV_TILE)
        qq = src_ref[0].reshape(rows // 2, LANES)
        for half, kk_ref in enumerate((klo_ref, khi_ref)):
            kj = kk_ref[0, 0, pl.ds(start, KV_TILE), :]
            s_ref[half * (rows // 2):(half + 1) * (rows // 2)] = lax.dot_general(
                qq, kj, (((1,), (1,)), ((), ())), preferred_element_type=F32)

    def consume(s_ref, j):
        s = s_ref[...]
        start = pl.multiple_of(j * KV_TILE, KV_TILE)
        vj = v_ref[0, 0, pl.ds(start, KV_TILE), :]
        if fixed_shift:
            shift = shift_ref[...]
            p = jnp.concatenate([jnp.exp(s[:, c * LANES:(c + 1) * LANES] - shift)
                                 for c in range(KV_TILE // LANES)], axis=1)
            acc_ref[...] += jnp.dot(p.astype(BF16), vj, preferred_element_type=F32)
            return
        m_prev = m_ref[...]
        m_next = jnp.maximum(m_prev, jnp.max(s, axis=-1, keepdims=True))
        alpha = jnp.exp(m_prev - m_next)
        p = jnp.concatenate([jnp.exp(s[:, c * LANES:(c + 1) * LANES] - m_next)
                             for c in range(KV_TILE // LANES)], axis=1)
        acc_ref[...] = alpha * acc_ref[...] + jnp.dot(p.astype(BF16), vj, preferred_element_type=F32)
        m_ref[...] = m_next

    @pl.when(pl.program_id(2) == 0)
    def _():
        scores(sa_ref, q_ref, 0)

    def body(jj, carry):
        scores(sb_ref, q_ref, 2 * jj + 1)
        consume(sa_ref, 2 * jj)
        scores(sa_ref, q_ref, 2 * jj + 2)
        consume(sb_ref, 2 * jj + 1)
        return carry

    lax.fori_loop(0, n_kv // 2 - 1, body, 0)
    scores(sb_ref, q_ref, n_kv - 1)
    consume(sa_ref, n_kv - 2)
    scores(sa_ref, qn_ref, 0)
    consume(sb_ref, n_kv - 1)
    acc = acc_ref[...]
    o = acc[:, :HEAD_DIM] / acc[:, HEAD_DIM:HEAD_DIM + 1]
    o_ref[0] = jnp.concatenate(
        [o[g * Q_TILE:(g + 1) * Q_TILE] for g in ((j % 2) * (KV_GROUP // 2) + j // 2 for j in range(KV_GROUP))],
        axis=-1).astype(o_ref.dtype)


def _rope_tables(seq):
    rows = seq // GRID_W
    n_freq = HEAD_DIM // 4
    inv = 1.0 / (ROPE_THETA ** (jnp.arange(n_freq, dtype=F32) / n_freq))
    r = jnp.arange(rows, dtype=F32)
    col = jnp.arange(GRID_W, dtype=F32)
    ang_r = jnp.broadcast_to(r[:, None, None] * inv, (rows, GRID_W, n_freq))
    ang_c = jnp.broadcast_to(col[None, :, None] * inv, (rows, GRID_W, n_freq))
    ang = jnp.concatenate([ang_r, ang_c], axis=-1).reshape(seq, 2 * n_freq)
    return jnp.cos(ang), jnp.sin(ang)


def _attention_layer(x, mod_l, norm_g, w_qkv, q_gain, k_gain, cos, sin):
    bsz, seq, d = x.shape
    bm = min(ROW_TILE, seq)
    perm = np.concatenate([np.arange(0, HEAD_DIM, 2), np.arange(1, HEAD_DIM, 2)])
    head_cols = lambda hh: hh * HEAD_DIM + perm
    k0 = N_HEADS * HEAD_DIM
    v0 = k0 + N_KV_HEADS * HEAD_DIM
    q_cols = np.concatenate([head_cols(hh) for hh in range(N_HEADS)])
    k_cols = np.concatenate([np.tile(k0 + head_cols(g), 2) for g in range(N_KV_HEADS)])
    w_v = jnp.pad(w_qkv[:, v0:].reshape(d, N_KV_HEADS, HEAD_DIM), ((0, 0), (0, 0), (0, LANES - HEAD_DIM)))
    w = jnp.concatenate([w_qkv[:, q_cols], w_qkv[:, k_cols], w_v.reshape(d, N_KV_HEADS * LANES)],
                        axis=1).astype(BF16)
    n_out = w.shape[1]
    reps = LANES // (HEAD_DIM // 2)
    zero = jnp.zeros_like(sin)
    c_t = jnp.tile(cos, (1, reps))
    s1_t = jnp.tile(jnp.concatenate([-sin, zero], axis=1), (1, reps // 2))
    s2_t = jnp.tile(jnp.concatenate([zero, sin], axis=1), (1, reps // 2))
    seg = jnp.asarray(np.kron(np.eye(LANES // HEAD_DIM), np.ones((HEAD_DIM, HEAD_DIM))), dtype=BF16)
    tab = pl.BlockSpec((bm, LANES), lambda b, i: (i, 0))
    n_pairs = N_HEADS * HEAD_DIM // LANES
    kv_spec = pl.BlockSpec((1, N_KV_HEADS, bm, LANES), lambda b, i: (b, 0, i, 0))
    kv_shape = jax.ShapeDtypeStruct((bsz, N_KV_HEADS, seq, LANES), BF16)
    q, k_lo, k_hi, v = pl.pallas_call(
        _qkv_kernel,
        grid=(bsz, seq // bm),
        in_specs=[
            pl.BlockSpec((1, bm, d), lambda b, i: (b, i, 0)),
            pl.BlockSpec((1, 6, d), lambda b, i: (b, 0, 0)),
            _resident((1, d)),
            _resident((d, n_out)),
            _resident((LANES, LANES)),
            tab,
            tab,
            tab,
            _resident((1, LANES)),
            _resident((1, LANES)),
        ],
        out_specs=[
            pl.BlockSpec((1, n_pairs, bm, LANES), lambda b, i: (b, 0, i, 0)),
            kv_spec,
            kv_spec,
            kv_spec,
        ],
        out_shape=[
            jax.ShapeDtypeStruct((bsz, n_pairs, seq, LANES), BF16),
            kv_shape,
            kv_shape,
            kv_shape,
        ],
        compiler_params=_params("parallel", "parallel"),
        name="attn_qkv",
    )(x, mod_l, norm_g.reshape(1, d), w, seg, c_t, s1_t, s2_t,
      jnp.tile(q_gain[perm] * HEAD_DIM ** -0.5, 2).reshape(1, LANES), jnp.tile(k_gain[perm], 2).reshape(1, LANES))

    score_bound = HEAD_DIM ** 0.5 * jnp.max(jnp.abs(q_gain)) * jnp.max(jnp.abs(k_gain))
    return _flash_attention(q, k_lo, k_hi, v, score_bound)


def _flash_attention(q, k_lo, k_hi, v, score_bound):
    bsz, _, seq, _ = q.shape
    assert (seq // KV_TILE) % 2 == 0, seq
    rows = KV_GROUP * Q_TILE
    n_q = seq // Q_TILE
    kv_spec = pl.BlockSpec((1, 1, seq, LANES), lambda b, g, i: (b, g, 0, 0))
    shift = jnp.full((1, LANES), score_bound, F32)

    def attend(fixed_shift):
        return pl.pallas_call(
            functools.partial(_flash_kernel, n_kv=seq // KV_TILE, fixed_shift=fixed_shift),
            grid=(bsz, N_KV_HEADS, n_q),
            in_specs=[
                pl.BlockSpec((1, KV_GROUP // 2, Q_TILE, LANES), lambda b, g, i: (b, g, i, 0)),
                pl.BlockSpec((1, KV_GROUP // 2, Q_TILE, LANES),
                             lambda b, g, i: (b, g, jnp.minimum(i + 1, n_q - 1), 0)),
                kv_spec,
                kv_spec,
                kv_spec,
                _resident((1, LANES)),
            ],
            out_specs=pl.BlockSpec((1, Q_TILE, KV_GROUP * HEAD_DIM), lambda b, g, i: (b, i, g)),
            out_shape=jax.ShapeDtypeStruct((bsz, seq, N_HEADS * HEAD_DIM), BF16),
            scratch_shapes=[
                pltpu.VMEM((rows, KV_TILE), F32),
                pltpu.VMEM((rows, KV_TILE), F32),
                pltpu.VMEM((rows, LANES), F32),
                pltpu.VMEM((rows, 2 * HEAD_DIM), F32),
            ],
            compiler_params=_params("parallel", "parallel", "arbitrary"),
            name="attn_flash_bounded" if fixed_shift else "attn_flash",
        )(q, q, k_lo, k_hi, v, shift)

    return lax.cond(score_bound < MAX_FIXED_SHIFT, lambda: attend(True), lambda: attend(False))


def _halo_rows(i, bm, seq):
    t = i * bm - HALO + lax.broadcasted_iota(jnp.int32, (bm + 2 * HALO, 1), 0)
    return (t >= 0) & (t < seq)


def _fill_window(hs_ref, xp_ref, x_ref, xn_ref, g, scale, shift, bm):
    hs_ref[0:HALO] = _norm_mod(xp_ref[0], g, scale, shift).astype(BF16)
    hs_ref[HALO:HALO + bm] = _norm_mod(x_ref[0], g, scale, shift).astype(BF16)
    hs_ref[HALO + bm:] = _norm_mod(xn_ref[0], g, scale, shift).astype(BF16)


def _conv3(win_ref, cw, cb, bm):
    return (win_ref[pl.ds(HALO - 1, bm), :] * cw[0:1] + win_ref[pl.ds(HALO, bm), :] * cw[1:2]
            + win_ref[pl.ds(HALO + 1, bm), :] * cw[2:3] + cb)


def _ffn_tail(x_main, m, hs_ref, gs_ref, wu_ref, cw_ref, cb_ref, wd_ref, fg_ref, o_ref, *, bm, seq, final_norm):
    valid = _halo_rows(pl.program_id(1), bm, seq)
    gate = jnp.dot(hs_ref[...], wu_ref[:, :D_FF], preferred_element_type=F32)
    gs_ref[...] = jnp.where(valid, gate, 0.0)
    gate = _conv3(gs_ref, cw_ref[...], cb_ref[...], bm)
    val = jnp.dot(hs_ref[HALO:HALO + bm], wu_ref[:, D_FF:], preferred_element_type=F32)
    a = (gate * jax.nn.sigmoid(gate)) * val
    y = x_main + m[5:6] * jnp.dot(a.astype(BF16), wd_ref[...], preferred_element_type=F32)
    if final_norm:
        y = y * lax.rsqrt(jnp.mean(y * y, axis=-1, keepdims=True) + EPS) * fg_ref[...]
    o_ref[0] = y


def _ffn_kernel(xp_ref, x_ref, xn_ref, mod_ref, g_ref, wu_ref, cw_ref, cb_ref, wd_ref, fg_ref,
                o_ref, hs_ref, gs_ref, *, bm, seq, final_norm):
    m = mod_ref[0]
    _fill_window(hs_ref, xp_ref, x_ref, xn_ref, g_ref[...], m[4:5], m[3:4], bm)
    _ffn_tail(x_ref[0], m, hs_ref, gs_ref, wu_ref, cw_ref, cb_ref, wd_ref, fg_ref, o_ref,
              bm=bm, seq=seq, final_norm=final_norm)


def _proj_ffn_kernel(ap_ref, a_ref, an_ref, wp_ref, xp_ref, x_ref, xn_ref, mod_ref, g_ref, wu_ref, cw_ref,
                     cb_ref, wd_ref, fg_ref, o_ref, hs_ref, gs_ref, aw_ref, xw_ref, *, bm, seq, final_norm):
    m = mod_ref[0]
    aw_ref[0:HALO] = ap_ref[0].astype(BF16)
    aw_ref[HALO:HALO + bm] = a_ref[0].astype(BF16)
    aw_ref[HALO + bm:] = an_ref[0].astype(BF16)
    xw_ref[0:HALO] = xp_ref[0]
    xw_ref[HALO:HALO + bm] = x_ref[0]
    xw_ref[HALO + bm:] = xn_ref[0]
    xw_ref[...] = xw_ref[...] + m[2:3] * jnp.dot(aw_ref[...], wp_ref[...], preferred_element_type=F32)
    hs_ref[...] = _norm_mod(xw_ref[...], g_ref[...], m[4:5], m[3:4]).astype(BF16)
    _ffn_tail(xw_ref[HALO:HALO + bm], m, hs_ref, gs_ref, wu_ref, cw_ref, cb_ref, wd_ref, fg_ref, o_ref,
              bm=bm, seq=seq, final_norm=final_norm)


def _halo_specs(bm, seq, d):
    nh = seq // HALO
    per = bm // HALO
    return [
        pl.BlockSpec((1, HALO, d), lambda b, i: (b, jnp.maximum(i * per - 1, 0), 0)),
        pl.BlockSpec((1, bm, d), lambda b, i: (b, i, 0)),
        pl.BlockSpec((1, HALO, d), lambda b, i: (b, jnp.minimum((i + 1) * per, nh - 1), 0)),
    ]


def _conv_ffn_layer(x, mod_l, norm_g, w_up, conv_w, conv_b, w_down, final_g, final_norm, mixer_out=None):
    bsz, seq, d = x.shape
    bm = min(ROW_TILE, seq)
    win = bm + 2 * HALO
    ffn_specs = _halo_specs(bm, seq, d) + [
        pl.BlockSpec((1, 6, d), lambda b, i: (b, 0, 0)),
        _resident((1, d)),
        _resident((d, 2 * D_FF)),
        _resident((3, D_FF)),
        _resident((1, D_FF)),
        _resident((D_FF, d)),
        _resident((1, d)),
    ]
    ffn_args = (x, x, x, mod_l, norm_g.reshape(1, d), w_up.astype(BF16), conv_w, conv_b.reshape(1, D_FF),
                w_down.astype(BF16), final_g.reshape(1, d))
    scratch = [pltpu.VMEM((win, d), BF16), pltpu.VMEM((win, D_FF), F32)]
    if mixer_out is None:
        body, specs, args = _ffn_kernel, ffn_specs, ffn_args
    else:
        a, w = mixer_out
        kdim = a.shape[-1]
        body = _proj_ffn_kernel
        specs = _halo_specs(bm, seq, kdim) + [_resident((kdim, d))] + ffn_specs
        args = (a, a, a, w.astype(BF16)) + ffn_args
        scratch = scratch + [pltpu.VMEM((win, kdim), BF16), pltpu.VMEM((win, d), F32)]
    return pl.pallas_call(
        functools.partial(body, bm=bm, seq=seq, final_norm=final_norm),
        grid=(bsz, seq // bm),
        in_specs=specs,
        out_specs=pl.BlockSpec((1, bm, d), lambda b, i: (b, i, 0)),
        out_shape=jax.ShapeDtypeStruct(x.shape, F32),
        scratch_shapes=scratch,
        compiler_params=_params("parallel", "parallel"),
        name="conv_ffn",
    )(*args)


def _hy_in_kernel(xp_ref, x_ref, xn_ref, mod_ref, g_ref, w_ref, cw_ref, cb_ref,
                  v_ref, x1_ref, x2_ref, hs_ref, us_ref, *, bm, seq):
    m = mod_ref[0]
    _fill_window(hs_ref, xp_ref, x_ref, xn_ref, g_ref[...], m[1:2], m[0:1], bm)
    valid = _halo_rows(pl.program_id(1), bm, seq)
    outs = (v_ref, x1_ref, x2_ref)
    for j in range(3 * D_MODEL // HY_CHUNK):
        c0 = j * HY_CHUNK
        u = jnp.dot(hs_ref[...], w_ref[:, c0:c0 + HY_CHUNK], preferred_element_type=F32)
        us_ref[...] = jnp.where(valid, u, 0.0)
        y = _conv3(us_ref, cw_ref[:, c0:c0 + HY_CHUNK], cb_ref[:, c0:c0 + HY_CHUNK], bm)
        off = c0 % D_MODEL
        outs[c0 // D_MODEL][0, :, off:off + HY_CHUNK] = y


def _hy_filter_kernel(z_ref, w1_ref, b1_ref, w2_ref, b2_ref, w3_ref, fr_ref, dl_ref, k_ref, n_ref,
                      *, bm, seq):
    i = pl.program_id(0)
    z = z_ref[...]
    fr = fr_ref[...]
    a = jnp.sin(fr * (jnp.dot(z, w1_ref[...], precision=HIGHEST, preferred_element_type=F32) + b1_ref[...]))
    a = jnp.sin(fr * (jnp.dot(a, w2_ref[...], precision=HIGHEST, preferred_element_type=F32) + b2_ref[...]))
    decay = jnp.exp(-z[:, 0:1] * dl_ref[...])
    t = i * bm + lax.broadcasted_iota(jnp.int32, (bm, 1), 0)
    decay = jnp.where(t == seq, 0.0, decay)

    @pl.when(i == 0)
    def _():
        n_ref[...] = jnp.zeros(n_ref.shape, F32)

    for o in range(HY_ORDER):
        f = jnp.dot(a, w3_ref[0, o], precision=HIGHEST, preferred_element_type=F32) * decay
        k_ref[o] = f
        n_ref[o:o + 1] += jnp.sum(jnp.abs(f), axis=0, keepdims=True)


def _dft_dot(w_ref, x):
    return jnp.dot(w_ref[...], x.astype(BF16), preferred_element_type=F32)


def _dft_in_kernel(x_ref, w_ref, o_ref, *, n1):
    for s in range(DFT_STEP):
        r = _dft_dot(w_ref, jnp.concatenate([x_ref[0, :, s, :], x_ref[1, :, s, :]], axis=0))
        o_ref[0, s, 0] = r[:n1]
        o_ref[0, s, 1] = r[n1:]


def _dft_filter_in_kernel(x_ref, w_ref, o_ref, *, n1):
    for s in range(DFT_STEP):
        r = _dft_dot(w_ref, jnp.concatenate([x_ref[0, :n1 // 2, s, :], x_ref[0, n1 // 2:, s, :]], axis=0))
        o_ref[0, s, 0] = r[:n1]
        o_ref[0, s, 1] = r[n1:]


def _twiddle(ar, ai, twr, twi):
    return ar * twr - ai * twi, ar * twi + ai * twr


def _dft_filter_mid_kernel(a_ref, w_ref, n_ref, o_ref):
    inv = 1.0 / n_ref[0]
    for j in range(DFT_STEP):
        z = _dft_dot(w_ref.at[j], jnp.concatenate([a_ref[0, :, 0, j, :], a_ref[0, :, 1, j, :]], axis=0))
        o_ref[0, 0, j] = z[:DFT_N2] * inv
        o_ref[0, 1, j] = z[DFT_N2:] * inv


def _dft_mid_kernel(a_ref, kf_ref, w_ref, wc_ref, o_ref):
    for j in range(DFT_STEP):
        z = _dft_dot(w_ref.at[j], jnp.concatenate([a_ref[0, :, 0, j, :], a_ref[0, :, 1, j, :]], axis=0))
        yr, yi = _twiddle(z[:DFT_N2], z[DFT_N2:], kf_ref[0, 0, j], kf_ref[0, 1, j])
        b = _dft_dot(wc_ref.at[j], jnp.concatenate([yr, yi], axis=0))
        o_ref[0, j, 0] = b[:DFT_N2]
        o_ref[0, j, 1] = b[DFT_N2:]


def _dft_out_kernel(b_ref, w_ref, gate_ref, zc_ref, skip_ref, o_ref, *, half):
    skip = skip_ref[...]
    for s in range(DFT_STEP):
        y = _dft_dot(w_ref, jnp.concatenate([b_ref[0, :, 0, s, :], b_ref[0, :, 1, s, :]], axis=0))
        o_ref[0, :, s, :] = gate_ref[0, :, s, :] * (y[:half] + skip * zc_ref[0, :, s, :])
        o_ref[1, :, s, :] = gate_ref[1, :, s, :] * (y[half:] + skip * zc_ref[1, :, s, :])


def _dft_constants(seq):
    n = 2 * seq
    n2 = DFT_N2
    n1 = n // n2
    half = n1 // 2
    k1 = np.arange(n1, dtype=np.float64)
    f1 = np.exp(-2j * np.pi * np.outer(k1, k1) / n1)

    def c2(m):
        return np.block([[m.real, -m.imag], [m.imag, m.real]])

    w_in = c2(f1[:, :half])
    w_fin = np.concatenate([f1.real, f1.imag], axis=0)
    w_out = c2(np.conj(f1)[:half, :]) / n
    as16 = lambda a: jnp.asarray(a, dtype=F32).astype(BF16)
    i1 = jnp.arange(n1, dtype=jnp.int32)[:, None, None]
    i2 = jnp.arange(n2, dtype=jnp.int32)
    phase = (i1 * i2[None, None, :] + n1 * (i2[None, :, None] * i2[None, None, :])) % n
    ang = phase.astype(F32) * (-2.0 * math.pi / n)
    mr, mi = jnp.cos(ang), jnp.sin(ang)
    w_mid = jnp.concatenate([jnp.concatenate([mr, -mi], axis=2), jnp.concatenate([mi, mr], axis=2)], axis=1)
    return dict(n1=n1, half=half, w_in=as16(w_in), w_fin=as16(w_fin), w_out=as16(w_out),
                w_mid=w_mid.astype(BF16), w_mid_c=jnp.swapaxes(w_mid, 1, 2).astype(BF16))


def _hyena_pos_features(seq):
    t = jnp.linspace(0.0, 1.0, seq, dtype=F32)[:, None]
    bands = (HY_EMB_DIM - 1) // 2
    w = 2.0 * math.pi * jnp.arange(seq, dtype=F32) / seq
    f = jnp.linspace(1e-4, bands - 1, bands, dtype=F32)
    ang = w[:, None] * f[None, :]
    z = jnp.concatenate([t, jnp.cos(ang), -jnp.sin(ang)], axis=-1)
    deltas = jnp.abs(jnp.linspace(math.log(HY_TARGET) / HY_SLOW_DECAY,
                                  math.log(HY_TARGET) / HY_FAST_DECAY, D_MODEL, dtype=F32))
    return z, deltas


def _hyena_filter_spectrum(seq, consts, f_w1, f_b1, f_w2, f_b2, f_w3, f_freq):
    d = D_MODEL
    n = 2 * seq
    n1, n2 = consts["n1"], DFT_N2
    z, deltas = _hyena_pos_features(seq)
    z2 = jnp.concatenate([z, z[:1], z[1:][::-1]], axis=0)
    kpad = HY_FILTER_WIDTH - HY_EMB_DIM
    z2 = jnp.pad(z2, ((0, 0), (0, kpad)))
    w1 = jnp.pad(f_w1, ((0, kpad), (0, 0)))
    bm = min(ROW_TILE, seq)
    nblk = n // bm
    k2, norm = pl.pallas_call(
        functools.partial(_hy_filter_kernel, bm=bm, seq=seq),
        grid=(nblk,),
        in_specs=[
            pl.BlockSpec((bm, HY_FILTER_WIDTH), lambda i: (i, 0)),
            _resident((HY_FILTER_WIDTH, HY_FILTER_WIDTH)),
            _resident((1, HY_FILTER_WIDTH)),
            _resident((HY_FILTER_WIDTH, HY_FILTER_WIDTH)),
            _resident((1, HY_FILTER_WIDTH)),
            pl.BlockSpec((1, HY_ORDER, HY_FILTER_WIDTH, d), lambda i: ((2 * i) // nblk, 0, 0, 0)),
            _resident((1, HY_FILTER_WIDTH)),
            _resident((1, d)),
        ],
        out_specs=[
            pl.BlockSpec((HY_ORDER, bm, d), lambda i: (0, i, 0)),
            pl.BlockSpec((HY_ORDER, d), lambda i: (0, 0)),
        ],
        out_shape=[
            jax.ShapeDtypeStruct((HY_ORDER, n, d), F32),
            jax.ShapeDtypeStruct((HY_ORDER, d), F32),
        ],
        compiler_params=_params("arbitrary"),
        name="hy_filter",
    )(z2, w1, f_b1.reshape(1, -1), f_w2, f_b2.reshape(1, -1),
      f_w3.reshape(HY_FILTER_WIDTH, HY_ORDER, 2, d).transpose(2, 1, 0, 3), f_freq.reshape(1, -1),
      deltas.reshape(1, d))

    st, ct = DFT_STEP, DFT_LANES
    a = pl.pallas_call(
        functools.partial(_dft_filter_in_kernel, n1=n1),
        grid=(HY_ORDER, n2 // st),
        in_specs=[
            pl.BlockSpec((1, n1, st, d), lambda o, j: (o, 0, j, 0)),
            _resident((2 * n1, n1)),
        ],
        out_specs=pl.BlockSpec((1, st, 2, n1, d), lambda o, j: (o, j, 0, 0, 0)),
        out_shape=jax.ShapeDtypeStruct((HY_ORDER, n2, 2, n1, d), F32),
        compiler_params=_params("parallel", "parallel"),
        name="hy_filter_dft_in",
    )(k2.reshape(HY_ORDER, n1, n2, d), consts["w_fin"])

    return pl.pallas_call(
        _dft_filter_mid_kernel,
        grid=(HY_ORDER, n1 // st, d // ct),
        in_specs=[
            pl.BlockSpec((1, n2, 2, st, ct), lambda o, k, c: (o, 0, 0, k, c)),
            pl.BlockSpec((st, 2 * n2, 2 * n2), lambda o, k, c: (k, 0, 0)),
            pl.BlockSpec((1, 1, ct), lambda o, k, c: (o, 0, c)),
        ],
        out_specs=pl.BlockSpec((1, 2, st, n2, ct), lambda o, k, c: (o, 0, k, 0, c)),
        out_shape=jax.ShapeDtypeStruct((HY_ORDER, 2, n1, n2, d), F32),
        compiler_params=_params("parallel", "parallel", "parallel"),
        name="hy_filter_dft_mid",
    )(a, consts["w_mid"], norm.reshape(HY_ORDER, 1, d))


def _long_conv_gate(zc, gate, kf, order, skip, consts):
    bsz, seq, d = zc.shape
    n1, half, n2 = consts["n1"], consts["half"], DFT_N2
    pairs = bsz // 2
    st, ct = DFT_STEP, DFT_LANES
    zv = zc.reshape(bsz, half, n2, d)
    a = pl.pallas_call(
        functools.partial(_dft_in_kernel, n1=n1),
        grid=(pairs, n2 // st),
        in_specs=[
            pl.BlockSpec((2, half, st, d), lambda p, j: (p, 0, j, 0)),
            _resident((2 * n1, n1)),
        ],
        out_specs=pl.BlockSpec((1, st, 2, n1, d), lambda p, j: (p, j, 0, 0, 0)),
        out_shape=jax.ShapeDtypeStruct((pairs, n2, 2, n1, d), F32),
        compiler_params=_params("parallel", "parallel"),
        name="hy_dft_in",
    )(zv, consts["w_in"])

    b = pl.pallas_call(
        _dft_mid_kernel,
        grid=(n1 // st, d // ct, pairs),
        in_specs=[
            pl.BlockSpec((1, n2, 2, st, ct), lambda k, c, p: (p, 0, 0, k, c)),
            pl.BlockSpec((1, 2, st, n2, ct), lambda k, c, p: (order, 0, k, 0, c)),
            pl.BlockSpec((st, 2 * n2, 2 * n2), lambda k, c, p: (k, 0, 0)),
            pl.BlockSpec((st, 2 * n2, 2 * n2), lambda k, c, p: (k, 0, 0)),
        ],
        out_specs=pl.BlockSpec((1, st, 2, n2, ct), lambda k, c, p: (p, k, 0, 0, c)),
        out_shape=jax.ShapeDtypeStruct((pairs, n1, 2, n2, d), F32),
        compiler_params=_params("parallel", "parallel", "arbitrary"),
        name="hy_dft_mid",
    )(a, kf, consts["w_mid"], consts["w_mid_c"])

    sig = pl.BlockSpec((2, half, st, ct), lambda p, j, c: (p, 0, j, c))
    out = pl.pallas_call(
        functools.partial(_dft_out_kernel, half=half),
        grid=(pairs, n2 // st, d // ct),
        in_specs=[
            pl.BlockSpec((1, n1, 2, st, ct), lambda p, j, c: (p, 0, 0, j, c)),
            _resident((n1, 2 * n1)),
            sig,
            sig,
            pl.BlockSpec((1, ct), lambda p, j, c: (0, c)),
        ],
        out_specs=sig,
        out_shape=jax.ShapeDtypeStruct((bsz, half, n2, d), F32),
        compiler_params=_params("parallel", "parallel", "parallel"),
        name="hy_dft_out",
    )(b, consts["w_out"], gate.reshape(bsz, half, n2, d), zv, skip.reshape(1, d))
    return out.reshape(bsz, seq, d)


def _hyena_layer(x, mod_l, norm_g, w_in, conv_w, conv_b, kf, skip, consts):
    bsz, seq, d = x.shape
    bm = min(ROW_TILE, seq)
    blk = pl.BlockSpec((1, bm, d), lambda b, i: (b, i, 0))
    shp = jax.ShapeDtypeStruct(x.shape, F32)
    v, x1, x2 = pl.pallas_call(
        functools.partial(_hy_in_kernel, bm=bm, seq=seq),
        grid=(bsz, seq // bm),
        in_specs=_halo_specs(bm, seq, d) + [
            pl.BlockSpec((1, 6, d), lambda b, i: (b, 0, 0)),
            _resident((1, d)),
            _resident((d, 3 * d)),
            _resident((3, 3 * d)),
            _resident((1, 3 * d)),
        ],
        out_specs=[blk, blk, blk],
        out_shape=[shp, shp, shp],
        scratch_shapes=[
            pltpu.VMEM((bm + 2 * HALO, d), BF16),
            pltpu.VMEM((bm + 2 * HALO, HY_CHUNK), F32),
        ],
        compiler_params=_params("parallel", "parallel"),
        name="hy_in",
    )(x, x, x, mod_l, norm_g.reshape(1, d), w_in.astype(BF16), conv_w, conv_b.reshape(1, 3 * d))
    zc = v
    for o, gate in enumerate((x1, x2)):
        zc = _long_conv_gate(zc, gate, kf, o, skip[o], consts)
    return zc


def _s5_prep_kernel(x_ref, mod_ref, g_ref, p_ref, o_ref):
    hs = []
    for b in range(x_ref.shape[0]):
        m = mod_ref[b]
        hs.append(_norm_mod(x_ref[b], g_ref[...], m[1:2], m[0:1]).astype(BF16))
    o_ref[...] = jnp.dot(p_ref[...], jnp.concatenate(hs, axis=0),
                         preferred_element_type=F32).astype(o_ref.dtype)


def _s5_scan_kernel(h_ref, bm_ref, cm_ref, lam_ref, pt_ref, o_ref, sa_ref, sb_ref, sc_ref, y_ref, carry_ref,
                    *, rows, reverse):
    n_slab = D_MODEL // S5_SLAB
    width = S5_SLAB // S5_GROUP * S5_STATE
    n_tiles = rows // SUBLANES
    half = SUBLANES // 2
    bufs = (sa_ref, sb_ref, sc_ref)

    @pl.when(pl.program_id(0) == 0)
    def _():
        carry_ref[...] = jnp.zeros(carry_ref.shape, F32)

    low = lax.broadcasted_iota(jnp.int32, (SUBLANES, width), 0) < half
    take_rolled = jnp.logical_not(low) if reverse else low
    low3 = lax.broadcasted_iota(jnp.int32, (n_tiles, SUBLANES, S5_SLAB), 1) < half
    gets_other = low3 if reverse else jnp.logical_not(low3)

    def project_in(k):
        u = h_ref[:, k * S5_SLAB:(k + 1) * S5_SLAB]
        u3 = u.astype(F32).reshape(n_tiles, SUBLANES, S5_SLAB)
        other = jnp.where(gets_other, pltpu.roll(u3, half, axis=1), 0.0).reshape(rows, S5_SLAB)
        lhs = jnp.concatenate([u, other.astype(BF16)], axis=1)
        bufs[k % 3][...] = jnp.dot(lhs, bm_ref[k], preferred_element_type=F32)

    def project_out(k):
        y_ref[:, k * S5_SLAB:(k + 1) * S5_SLAB] = jnp.dot(
            bufs[k % 3][...].astype(BF16), cm_ref[k], preferred_element_type=F32).astype(y_ref.dtype)

    def scan(k):
        s_ref = bufs[k % 3]
        lr = lam_ref[k, 0]
        li = lam_ref[k, 1]
        cr = carry_ref[k, 0]
        ci = carry_ref[k, 1]
        for i in (reversed(range(n_tiles)) if reverse else range(n_tiles)):
            r0 = i * SUBLANES
            vr = s_ref[r0:r0 + SUBLANES, 0:width]
            vi = s_ref[r0:r0 + SUBLANES, width:2 * width]
            pr = jnp.where(take_rolled, pltpu.roll(cr, half, axis=0), cr)
            pi = jnp.where(take_rolled, pltpu.roll(ci, half, axis=0), ci)
            cr = vr + (lr * pr - li * pi)
            ci = vi + (lr * pi + li * pr)
            s_ref[r0:r0 + SUBLANES, 0:width] = cr
            s_ref[r0:r0 + SUBLANES, width:2 * width] = ci
        carry_ref[k, 0] = cr
        carry_ref[k, 1] = ci

    project_in(0)
    for k in range(n_slab):
        if k + 1 < n_slab:
            project_in(k + 1)
        if k >= 1:
            project_out(k - 1)
        scan(k)
    project_out(n_slab - 1)
    y = jnp.dot(pt_ref[...], y_ref[...], preferred_element_type=F32)
    steps = rows // o_ref.shape[0]
    for b in range(o_ref.shape[0]):
        o_ref[b] = y[b * steps:(b + 1) * steps].astype(o_ref.dtype)


def _s5_glu_kernel(yf_ref, yb_ref, x_ref, mod_ref, g_ref, dsk_ref, w_ref, o_ref):
    m = mod_ref[0]
    x = x_ref[0]
    h = _norm_mod(x, g_ref[...], m[1:2], m[0:1])
    y = jax.nn.gelu(yf_ref[0].astype(F32) + yb_ref[0].astype(F32) + dsk_ref[...] * h)
    g = jnp.dot(y.astype(BF16), w_ref[...], preferred_element_type=F32)
    d = x.shape[-1]
    o_ref[0] = x + m[2:3] * (g[:, :d] * jax.nn.sigmoid(g[:, d:]))


def _s5_tables(a_re, a_im, log_dt, b_re, b_im, c_re, c_im):
    n_slab = D_MODEL // S5_SLAB
    gps = S5_SLAB // S5_GROUP
    lam = lax.complex(jnp.minimum(a_re.astype(F32), -1e-4), a_im.astype(F32))
    dt = jnp.exp(log_dt.astype(F32))[..., None]
    lam_bar = jnp.exp(lam * dt)
    b_bar = ((lam_bar - 1.0) / lam)[..., None] * lax.complex(b_re.astype(F32), b_im.astype(F32))
    eye = jnp.eye(gps, dtype=F32)

    def b_block(part):
        m = jnp.einsum("dkgph,gj->dkghjp", part, eye)
        return m.reshape(2, n_slab, S5_SLAB, gps * S5_STATE)

    def b_rows(b_c):
        bb = b_c.reshape(2, n_slab, gps, S5_STATE, S5_GROUP)
        return jnp.concatenate([b_block(bb.real), b_block(bb.imag)], axis=-1)

    b_mat = jnp.concatenate([b_rows(b_bar), b_rows(b_bar * lam_bar[..., None])], axis=-2)
    cc = lax.complex(c_re.astype(F32), c_im.astype(F32)).reshape(2, n_slab, gps, S5_GROUP, S5_STATE)

    def c_block(part):
        m = jnp.einsum("dkghp,gj->dkgpjh", part, eye)
        return m.reshape(2, n_slab, gps * S5_STATE, S5_SLAB)

    c_mat = jnp.concatenate([c_block(cc.real), -c_block(cc.imag)], axis=-2)
    lam1 = lam_bar.reshape(2, n_slab, 1, gps * S5_STATE)
    lam2 = lam1 * lam1
    half = SUBLANES // 2

    def rows(lo, hi):
        return jnp.concatenate([jnp.broadcast_to(lo, lo.shape[:2] + (half,) + lo.shape[3:]),
                                jnp.broadcast_to(hi, hi.shape[:2] + (half,) + hi.shape[3:])], axis=2)

    carry = jnp.stack([rows(lam1, lam2)[0], rows(lam2, lam1)[1]])
    lam_t = jnp.stack([carry.real, carry.imag], axis=2)
    return b_mat, c_mat, lam_t


def _s5_layer(x, mod_l, norm_g, a_re, a_im, log_dt, b_re, b_im, c_re, c_im, d_skip, w_glu):
    bsz, seq, d = x.shape
    assert 2 * bsz == SUBLANES, bsz
    bm = min(ROW_TILE, seq)
    n_slab = d // S5_SLAB
    width = S5_SLAB // S5_GROUP * S5_STATE
    b_mat, c_mat, lam_t = _s5_tables(a_re, a_im, log_dt, b_re, b_im, c_re, c_im)
    steps = min(S5_STEPS, seq)
    rows = steps * bsz
    nc = seq // steps
    b_mat = b_mat.astype(BF16)
    c_mat = c_mat.astype(BF16)
    r = np.arange(rows)
    perm_np = np.zeros((rows, rows), np.float32)
    perm_np[r, (r % bsz) * steps + r // bsz] = 1.0
    perm = jnp.asarray(perm_np, dtype=BF16)

    h2 = pl.pallas_call(
        _s5_prep_kernel,
        grid=(nc,),
        in_specs=[
            pl.BlockSpec((bsz, steps, d), lambda i: (0, i, 0)),
            _resident((bsz, 6, d)),
            _resident((1, d)),
            _resident((rows, rows)),
        ],
        out_specs=pl.BlockSpec((rows, d), lambda i: (i, 0)),
        out_shape=jax.ShapeDtypeStruct((seq * bsz, d), BF16),
        compiler_params=_params("parallel"),
        name="s5_prep",
    )(x, mod_l, norm_g.reshape(1, d), perm)

    def scan(direction):
        reverse = direction == 1
        chunk = (lambda c: nc - 1 - c) if reverse else (lambda c: c)
        return pl.pallas_call(
            functools.partial(_s5_scan_kernel, rows=rows, reverse=reverse),
            grid=(nc,),
            in_specs=[
                pl.BlockSpec((rows, d), lambda c: (chunk(c), 0)),
                _resident((n_slab, 2 * S5_SLAB, 2 * width)),
                _resident((n_slab, 2 * width, S5_SLAB)),
                _resident((n_slab, 2, SUBLANES, width)),
                _resident((rows, rows)),
            ],
            out_specs=pl.BlockSpec((bsz, steps, d), lambda c: (0, chunk(c), 0)),
            out_shape=jax.ShapeDtypeStruct((bsz, seq, d), BF16),
            scratch_shapes=[
                pltpu.VMEM((rows, 2 * width), F32),
                pltpu.VMEM((rows, 2 * width), F32),
                pltpu.VMEM((rows, 2 * width), F32),
                pltpu.VMEM((rows, d), BF16),
                pltpu.VMEM((n_slab, 2, SUBLANES, width), F32),
            ],
            compiler_params=_params("arbitrary"),
            name="s5_scan_bwd" if reverse else "s5_scan_fwd",
        )(h2, b_mat[direction], c_mat[direction], lam_t[direction], perm.T)

    y_f = scan(0)
    y_b = scan(1)
    return pl.pallas_call(
        _s5_glu_kernel,
        grid=(bsz, seq // bm),
        in_specs=[
            pl.BlockSpec((1, bm, d), lambda b, i: (b, i, 0)),
            pl.BlockSpec((1, bm, d), lambda b, i: (b, i, 0)),
            pl.BlockSpec((1, bm, d), lambda b, i: (b, i, 0)),
            pl.BlockSpec((1, 6, d), lambda b, i: (b, 0, 0)),
            _resident((1, d)),
            _resident((1, d)),
            _resident((d, 2 * d)),
        ],
        out_specs=pl.BlockSpec((1, bm, d), lambda b, i: (b, i, 0)),
        out_shape=jax.ShapeDtypeStruct(x.shape, F32),
        compiler_params=_params("parallel", "parallel"),
        name="s5_glu",
    )(y_f, y_b, x, mod_l, norm_g.reshape(1, d), d_skip.reshape(1, d), w_glu.astype(BF16))


def kernel(x, c, ada_w, ada_b, norm1_g, norm2_g, final_g, attn_w_qkv, attn_w_o, attn_q_gain, attn_k_gain, hy_w_in, hy_conv_w, hy_conv_b, hy_f_w1, hy_f_b1, hy_f_w2, hy_f_b2, hy_f_w3, hy_f_freq, hy_skip, hy_w_out, s5_A_re, s5_A_im, s5_log_dt, s5_B_re, s5_B_im, s5_C_re, s5_C_im, s5_D, s5_w_glu, ffn_w_up, ffn_conv_w, ffn_conv_b, ffn_w_down):
    seq = x.shape[1]
    mod = _ada_mod(c, ada_w, ada_b)
    cos, sin = _rope_tables(seq)
    consts = _dft_constants(seq)
    for i in range(DEPTH):
        m, j = i % N_MIXERS, i // N_MIXERS
        mixer_out = None
        if m == 0:
            o = _attention_layer(x, mod[i], norm1_g[i], attn_w_qkv[j], attn_q_gain[j], attn_k_gain[j], cos, sin)
            mixer_out = (o, attn_w_o[j])
        elif m == 1:
            kf = _hyena_filter_spectrum(seq, consts, hy_f_w1[j], hy_f_b1[j], hy_f_w2[j], hy_f_b2[j],
                                        hy_f_w3[j], hy_f_freq[j])
            zc = _hyena_layer(x, mod[i], norm1_g[i], hy_w_in[j], hy_conv_w[j], hy_conv_b[j], kf,
                              hy_skip[j], consts)
            mixer_out = (zc, hy_w_out[j])
        else:
            x = _s5_layer(x, mod[i], norm1_g[i], s5_A_re[j], s5_A_im[j], s5_log_dt[j], s5_B_re[j],
                          s5_B_im[j], s5_C_re[j], s5_C_im[j], s5_D[j], s5_w_glu[j])
        x = _conv_ffn_layer(x, mod[i], norm2_g[i], ffn_w_up[i], ffn_conv_w[i], ffn_conv_b[i],
                            ffn_w_down[i], final_g, final_norm=(i == DEPTH - 1), mixer_out=mixer_out)
    return x
```

```python
import functools
import math

import numpy as np
import jax
import jax.numpy as jnp
from jax import lax
from jax.experimental import pallas as pl
from jax.experimental.pallas import tpu as pltpu

D_MODEL = 1024
DEPTH = 4
N_MIXERS = 3
GRID_W = 64
EPS = 1e-6
N_HEADS = 16
HEAD_DIM = D_MODEL // N_HEADS
N_KV_HEADS = 4
KV_GROUP = N_HEADS // N_KV_HEADS
ROPE_THETA = 10000.0
HY_ORDER = 2
HY_EMB_DIM = 33
HY_FILTER_WIDTH = 64
HY_FAST_DECAY = 0.3
HY_SLOW_DECAY = 1.5
HY_TARGET = 1e-2
S5_GROUP = 16
S5_GROUPS = D_MODEL // S5_GROUP
S5_STATE = 64
D_FF = 2816

F32 = jnp.float32
BF16 = jnp.bfloat16
HIGHEST = lax.Precision.HIGHEST

LANES = 128
SUBLANES = 8
BF16_ROWS = 16
VMEM_LIMIT = 56 * 1024 * 1024

ROW_TILE = 512
HALO = BF16_ROWS
HY_CHUNK = 512
Q_TILE = 512
KV_TILE = 1024
MAX_FIXED_SHIFT = 40.0
DFT_N2 = 128
DFT_STEP = SUBLANES
DFT_LANES = 512
S5_SLAB = 128
S5_STEPS = 128


def _params(*sem):
    return pltpu.CompilerParams(dimension_semantics=sem, vmem_limit_bytes=VMEM_LIMIT)


def _resident(shape):
    zeros = (0,) * len(shape)
    return pl.BlockSpec(shape, lambda *_: zeros, pipeline_mode=pl.Buffered(1))


def _norm_mod(x, g, scale, shift):
    y = x * lax.rsqrt(jnp.mean(x * x, axis=-1, keepdims=True) + EPS)
    return (y * g) * (1.0 + scale) + shift


def _ada_kernel(c_ref, w_ref, b_ref, o_ref):
    c = c_ref[...]
    a = c * jax.nn.sigmoid(c)
    o_ref[0] = jnp.dot(a, w_ref[0], precision=HIGHEST, preferred_element_type=F32) + b_ref[0]


def _ada_mod(c, ada_w, ada_b):
    bsz, d = c.shape
    rows = -(-bsz // SUBLANES) * SUBLANES
    cp = jnp.pad(c, ((0, rows - bsz), (0, 0)))
    tn = 1536
    out = pl.pallas_call(
        _ada_kernel,
        grid=(DEPTH, 6 * d // tn),
        in_specs=[
            pl.BlockSpec((rows, d), lambda i, j: (0, 0)),
            pl.BlockSpec((1, d, tn), lambda i, j: (i, 0, j)),
            pl.BlockSpec((1, 1, tn), lambda i, j: (i, 0, j)),
        ],
        out_specs=pl.BlockSpec((1, rows, tn), lambda i, j: (i, 0, j)),
        out_shape=jax.ShapeDtypeStruct((DEPTH, rows, 6 * d), F32),
        compiler_params=_params("arbitrary", "arbitrary"),
        name="ada_mod",
    )(cp, ada_w, ada_b.reshape(DEPTH, 1, 6 * d))
    return out[:, :bsz].reshape(DEPTH, bsz, 6, d)


def _qkv_kernel(x_ref, mod_ref, g_ref, w_ref, seg_ref, c_ref, s1_ref, s2_ref, qg_ref, kg_ref,
                q_ref, klo_ref, khi_ref, v_ref):
    m = mod_ref[0]
    h = _norm_mod(x_ref[0], g_ref[...], m[1:2], m[0:1])
    r = jnp.dot(h.astype(BF16), w_ref[...], preferred_element_type=F32)
    c = c_ref[...]
    s1 = s1_ref[...]
    s2 = s2_ref[...]
    seg = seg_ref[...]
    quarter = HEAD_DIM // 2

    def norm_rope(xr, gain):
        ss = xr * xr
        hi = ss.astype(BF16)
        lo = (ss - hi.astype(F32)).astype(BF16)
        tot = (jnp.dot(hi, seg, preferred_element_type=F32) + jnp.dot(lo, seg, preferred_element_type=F32))
        n = xr * lax.rsqrt(tot * (1.0 / HEAD_DIM) + EPS) * gain
        return (n * c + pltpu.roll(n, LANES - quarter, axis=1) * s1 + pltpu.roll(n, quarter, axis=1) * s2)

    qg = qg_ref[...]
    kg = kg_ref[...]
    n_q = N_HEADS * HEAD_DIM // LANES
    for p in range(n_q):
        q_ref[0, p] = norm_rope(r[:, p * LANES:(p + 1) * LANES], qg).astype(BF16)
    low = lax.broadcasted_iota(jnp.int32, (r.shape[0], LANES), 1) < HEAD_DIM
    for g in range(N_KV_HEADS):
        kk = norm_rope(r[:, (n_q + g) * LANES:(n_q + g + 1) * LANES], kg)
        klo_ref[0, g] = jnp.where(low, kk, 0.0).astype(BF16)
        khi_ref[0, g] = jnp.where(low, 0.0, kk).astype(BF16)
        vv = r[:, (n_q + N_KV_HEADS + g) * LANES:(n_q + N_KV_HEADS + g + 1) * LANES]
        v_ref[0, g] = jnp.where(low, vv, 1.0).astype(BF16)


def _flash_kernel(q_ref, qn_ref, klo_ref, khi_ref, v_ref, shift_ref, o_ref, sa_ref, sb_ref, m_ref, acc_ref,
                  *, n_kv, fixed_shift):
    rows = KV_GROUP * Q_TILE
    m_ref[...] = jnp.full(m_ref.shape, -jnp.inf, F32)
    acc_ref[...] = jnp.zeros(acc_ref.shape, F32)

    def scores(s_ref, src_ref, j):
        start = pl.multiple_of(j * KV_TILE, KV_TILE)
        qq = src_ref[0].reshape(rows // 2, LANES)
        for half, kk_ref in enumerate((klo_ref, khi_ref)):
            kj = kk_ref[0, 0, pl.ds(start, KV_TILE), :]
            s_ref[half * (rows // 2):(half + 1) * (rows // 2)] = lax.dot_general(
                qq, kj, (((1,), (1,)), ((), ())), preferred_element_type=F32)

    def consume(s_ref, j):
        s = s_ref[...]
        start = pl.multiple_of(j * KV_TILE, KV_TILE)
        vj = v_ref[0, 0, pl.ds(start, KV_TILE), :]
        if fixed_shift:
            shift = shift_ref[...]
            p = jnp.concatenate([jnp.exp(s[:, c * LANES:(c + 1) * LANES] - shift)
                                 for c in range(KV_TILE // LANES)], axis=1)
            acc_ref[...] += jnp.dot(p.astype(BF16), vj, preferred_element_type=F32)
            return
        m_prev = m_ref[...]
        m_next = jnp.maximum(m_prev, jnp.max(s, axis=-1, keepdims=True))
        alpha = jnp.exp(m_prev - m_next)
        p = jnp.concatenate([jnp.exp(s[:, c * LANES:(c + 1) * LANES] - m_next)
                             for c in range(KV_TILE // LANES)], axis=1)
        acc_ref[...] = alpha * acc_ref[...] + jnp.dot(p.astype(BF16), vj, preferred_element_type=F32)
        m_ref[...] = m_next

    @pl.when(pl.program_id(2) == 0)
    def _():
        scores(sa_ref, q_ref, 0)

    def body(jj, carry):
        scores(sb_ref, q_ref, 2 * jj + 1)
        consume(sa_ref, 2 * jj)
        scores(sa_ref, q_ref, 2 * jj + 2)
        consume(sb_ref, 2 * jj + 1)
        return carry

    lax.fori_loop(0, n_kv // 2 - 1, body, 0)
    scores(sb_ref, q_ref, n_kv - 1)
    consume(sa_ref, n_kv - 2)
    scores(sa_ref, qn_ref, 0)
    consume(sb_ref, n_kv - 1)
    acc = acc_ref[...]
    o = acc[:, :HEAD_DIM] / acc[:, HEAD_DIM:HEAD_DIM + 1]
    o_ref[0] = jnp.concatenate(
        [o[g * Q_TILE:(g + 1) * Q_TILE] for g in ((j % 2) * (KV_GROUP // 2) + j // 2 for j in range(KV_GROUP))],
        axis=-1).astype(o_ref.dtype)


def _rope_tables(seq):
    rows = seq // GRID_W
    n_freq = HEAD_DIM // 4
    inv = 1.0 / (ROPE_THETA ** (jnp.arange(n_freq, dtype=F32) / n_freq))
    r = jnp.arange(rows, dtype=F32)
    col = jnp.arange(GRID_W, dtype=F32)
    ang_r = jnp.broadcast_to(r[:, None, None] * inv, (rows, GRID_W, n_freq))
    ang_c = jnp.broadcast_to(col[None, :, None] * inv, (rows, GRID_W, n_freq))
    ang = jnp.concatenate([ang_r, ang_c], axis=-1).reshape(seq, 2 * n_freq)
    return jnp.cos(ang), jnp.sin(ang)


def _attention_layer(x, mod_l, norm_g, w_qkv, q_gain, k_gain, cos, sin):
    bsz, seq, d = x.shape
    bm = min(ROW_TILE, seq)
    perm = np.concatenate([np.arange(0, HEAD_DIM, 2), np.arange(1, HEAD_DIM, 2)])
    head_cols = lambda hh: hh * HEAD_DIM + perm
    k0 = N_HEADS * HEAD_DIM
    v0 = k0 + N_KV_HEADS * HEAD_DIM
    q_cols = np.concatenate([head_cols(hh) for hh in range(N_HEADS)])
    k_cols = np.concatenate([np.tile(k0 + head_cols(g), 2) for g in range(N_KV_HEADS)])
    w_v = jnp.pad(w_qkv[:, v0:].reshape(d, N_KV_HEADS, HEAD_DIM), ((0, 0), (0, 0), (0, LANES - HEAD_DIM)))
    w = jnp.concatenate([w_qkv[:, q_cols], w_qkv[:, k_cols], w_v.reshape(d, N_KV_HEADS * LANES)],
                        axis=1).astype(BF16)
    n_out = w.shape[1]
    reps = LANES // (HEAD_DIM // 2)
    zero = jnp.zeros_like(sin)
    c_t = jnp.tile(cos, (1, reps))
    s1_t = jnp.tile(jnp.concatenate([-sin, zero], axis=1), (1, reps // 2))
    s2_t = jnp.tile(jnp.concatenate([zero, sin], axis=1), (1, reps // 2))
    seg = jnp.asarray(np.kron(np.eye(LANES // HEAD_DIM), np.ones((HEAD_DIM, HEAD_DIM))), dtype=BF16)
    tab = pl.BlockSpec((bm, LANES), lambda b, i: (i, 0))
    n_pairs = N_HEADS * HEAD_DIM // LANES
    kv_spec = pl.BlockSpec((1, N_KV_HEADS, bm, LANES), lambda b, i: (b, 0, i, 0))
    kv_shape = jax.ShapeDtypeStruct((bsz, N_KV_HEADS, seq, LANES), BF16)
    q, k_lo, k_hi, v = pl.pallas_call(
        _qkv_kernel,
        grid=(bsz, seq // bm),
        in_specs=[
            pl.BlockSpec((1, bm, d), lambda b, i: (b, i, 0)),
            pl.BlockSpec((1, 6, d), lambda b, i: (b, 0, 0)),
            _resident((1, d)),
            _resident((d, n_out)),
            _resident((LANES, LANES)),
            tab,
            tab,
            tab,
            _resident((1, LANES)),
            _resident((1, LANES)),
        ],
        out_specs=[
            pl.BlockSpec((1, n_pairs, bm, LANES), lambda b, i: (b, 0, i, 0)),
            kv_spec,
            kv_spec,
            kv_spec,
        ],
        out_shape=[
            jax.ShapeDtypeStruct((bsz, n_pairs, seq, LANES), BF16),
            kv_shape,
            kv_shape,
            kv_shape,
        ],
        compiler_params=_params("parallel", "parallel"),
        name="attn_qkv",
    )(x, mod_l, norm_g.reshape(1, d), w, seg, c_t, s1_t, s2_t,
      jnp.tile(q_gain[perm] * HEAD_DIM ** -0.5, 2).reshape(1, LANES), jnp.tile(k_gain[perm], 2).reshape(1, LANES))

    score_bound = HEAD_DIM ** 0.5 * jnp.max(jnp.abs(q_gain)) * jnp.max(jnp.abs(k_gain))
    return _flash_attention(q, k_lo, k_hi, v, score_bound)


def _flash_attention(q, k_lo, k_hi, v, score_bound):
    bsz, _, seq, _ = q.shape
    assert (seq // KV_TILE) % 2 == 0, seq
    rows = KV_GROUP * Q_TILE
    n_q = seq // Q_TILE
    kv_spec = pl.BlockSpec((1, 1, seq, LANES), lambda b, g, i: (b, g, 0, 0))
    shift = jnp.full((1, LANES), score_bound, F32)

    def attend(fixed_shift):
        return pl.pallas_call(
            functools.partial(_flash_kernel, n_kv=seq // KV_TILE, fixed_shift=fixed_shift),
            grid=(bsz, N_KV_HEADS, n_q),
            in_specs=[
                pl.BlockSpec((1, KV_GROUP // 2, Q_TILE, LANES), lambda b, g, i: (b, g, i, 0)),
                pl.BlockSpec((1, KV_GROUP // 2, Q_TILE, LANES),
                             lambda b, g, i: (b, g, jnp.minimum(i + 1, n_q - 1), 0)),
                kv_spec,
                kv_spec,
                kv_spec,
                _resident((1, LANES)),
            ],
            out_specs=pl.BlockSpec((1, Q_TILE, KV_GROUP * HEAD_DIM), lambda b, g, i: (b, i, g)),
            out_shape=jax.ShapeDtypeStruct((bsz, seq, N_HEADS * HEAD_DIM), BF16),
            scratch_shapes=[
                pltpu.VMEM((rows, KV_TILE), F32),
                pltpu.VMEM((rows, KV_TILE), F32),
                pltpu.VMEM((rows, LANES), F32),
                pltpu.VMEM((rows, 2 * HEAD_DIM), F32),
            ],
            compiler_params=_params("parallel", "parallel", "arbitrary"),
            name="attn_flash_bounded" if fixed_shift else "attn_flash",
        )(q, q, k_lo, k_hi, v, shift)

    return lax.cond(score_bound < MAX_FIXED_SHIFT, lambda: attend(True), lambda: attend(False))


def _halo_rows(i, bm, seq):
    t = i * bm - HALO + lax.broadcasted_iota(jnp.int32, (bm + 2 * HALO, 1), 0)
    return (t >= 0) & (t < seq)


def _fill_window(hs_ref, xp_ref, x_ref, xn_ref, g, scale, shift, bm):
    hs_ref[0:HALO] = _norm_mod(xp_ref[0], g, scale, shift).astype(BF16)
    hs_ref[HALO:HALO + bm] = _norm_mod(x_ref[0], g, scale, shift).astype(BF16)
    hs_ref[HALO + bm:] = _norm_mod(xn_ref[0], g, scale, shift).astype(BF16)


def _conv3(win_ref, cw, cb, bm):
    return (win_ref[pl.ds(HALO - 1, bm), :] * cw[0:1] + win_ref[pl.ds(HALO, bm), :] * cw[1:2]
            + win_ref[pl.ds(HALO + 1, bm), :] * cw[2:3] + cb)


def _ffn_tail(x_main, m, hs_ref, gs_ref, wu_ref, cw_ref, cb_ref, wd_ref, fg_ref, o_ref, *, bm, seq, final_norm):
    valid = _halo_rows(pl.program_id(1), bm, seq)
    gate = jnp.dot(hs_ref[...], wu_ref[:, :D_FF], preferred_element_type=F32)
    gs_ref[...] = jnp.where(valid, gate, 0.0)
    gate = _conv3(gs_ref, cw_ref[...], cb_ref[...], bm)
    val = jnp.dot(hs_ref[HALO:HALO + bm], wu_ref[:, D_FF:], preferred_element_type=F32)
    a = (gate * jax.nn.sigmoid(gate)) * val
    y = x_main + m[5:6] * jnp.dot(a.astype(BF16), wd_ref[...], preferred_element_type=F32)
    if final_norm:
        y = y * lax.rsqrt(jnp.mean(y * y, axis=-1, keepdims=True) + EPS) * fg_ref[...]
    o_ref[0] = y


def _ffn_kernel(xp_ref, x_ref, xn_ref, mod_ref, g_ref, wu_ref, cw_ref, cb_ref, wd_ref, fg_ref,
                o_ref, hs_ref, gs_ref, *, bm, seq, final_norm):
    m = mod_ref[0]
    _fill_window(hs_ref, xp_ref, x_ref, xn_ref, g_ref[...], m[4:5], m[3:4], bm)
    _ffn_tail(x_ref[0], m, hs_ref, gs_ref, wu_ref, cw_ref, cb_ref, wd_ref, fg_ref, o_ref,
              bm=bm, seq=seq, final_norm=final_norm)


def _proj_ffn_kernel(ap_ref, a_ref, an_ref, wp_ref, xp_ref, x_ref, xn_ref, mod_ref, g_ref, wu_ref, cw_ref,
                     cb_ref, wd_ref, fg_ref, o_ref, hs_ref, gs_ref, aw_ref, xw_ref, *, bm, seq, final_norm):
    m = mod_ref[0]
    aw_ref[0:HALO] = ap_ref[0].astype(BF16)
    aw_ref[HALO:HALO + bm] = a_ref[0].astype(BF16)
    aw_ref[HALO + bm:] = an_ref[0].astype(BF16)
    xw_ref[0:HALO] = xp_ref[0]
    xw_ref[HALO:HALO + bm] = x_ref[0]
    xw_ref[HALO + bm:] = xn_ref[0]
    xw_ref[...] = xw_ref[...] + m[2:3] * jnp.dot(aw_ref[...], wp_ref[...], preferred_element_type=F32)
    hs_ref[...] = _norm_mod(xw_ref[...], g_ref[...], m[4:5], m[3:4]).astype(BF16)
    _ffn_tail(xw_ref[HALO:HALO + bm], m, hs_ref, gs_ref, wu_ref, cw_ref, cb_ref, wd_ref, fg_ref, o_ref,
              bm=bm, seq=seq, final_norm=final_norm)


def _halo_specs(bm, seq, d):
    nh = seq // HALO
    per = bm // HALO
    return [
        pl.BlockSpec((1, HALO, d), lambda b, i: (b, jnp.maximum(i * per - 1, 0), 0)),
        pl.BlockSpec((1, bm, d), lambda b, i: (b, i, 0)),
        pl.BlockSpec((1, HALO, d), lambda b, i: (b, jnp.minimum((i + 1) * per, nh - 1), 0)),
    ]


def _conv_ffn_layer(x, mod_l, norm_g, w_up, conv_w, conv_b, w_down, final_g, final_norm, mixer_out=None):
    bsz, seq, d = x.shape
    bm = min(ROW_TILE, seq)
    win = bm + 2 * HALO
    ffn_specs = _halo_specs(bm, seq, d) + [
        pl.BlockSpec((1, 6, d), lambda b, i: (b, 0, 0)),
        _resident((1, d)),
        _resident((d, 2 * D_FF)),
        _resident((3, D_FF)),
        _resident((1, D_FF)),
        _resident((D_FF, d)),
        _resident((1, d)),
    ]
    ffn_args = (x, x, x, mod_l, norm_g.reshape(1, d), w_up.astype(BF16), conv_w, conv_b.reshape(1, D_FF),
                w_down.astype(BF16), final_g.reshape(1, d))
    scratch = [pltpu.VMEM((win, d), BF16), pltpu.VMEM((win, D_FF), F32)]
    if mixer_out is None:
        body, specs, args = _ffn_kernel, ffn_specs, ffn_args
    else:
        a, w = mixer_out
        kdim = a.shape[-1]
        body = _proj_ffn_kernel
        specs = _halo_specs(bm, seq, kdim) + [_resident((kdim, d))] + ffn_specs
        args = (a, a, a, w.astype(BF16)) + ffn_args
        scratch = scratch + [pltpu.VMEM((win, kdim), BF16), pltpu.VMEM((win, d), F32)]
    return pl.pallas_call(
        functools.partial(body, bm=bm, seq=seq, final_norm=final_norm),
        grid=(bsz, seq // bm),
        in_specs=specs,
        out_specs=pl.BlockSpec((1, bm, d), lambda b, i: (b, i, 0)),
        out_shape=jax.ShapeDtypeStruct(x.shape, F32),
        scratch_shapes=scratch,
        compiler_params=_params("parallel", "parallel"),
        name="conv_ffn",
    )(*args)


def _hy_in_kernel(xp_ref, x_ref, xn_ref, mod_ref, g_ref, w_ref, cw_ref, cb_ref,
                  v_ref, x1_ref, x2_ref, hs_ref, us_ref, *, bm, seq):
    m = mod_ref[0]
    _fill_window(hs_ref, xp_ref, x_ref, xn_ref, g_ref[...], m[1:2], m[0:1], bm)
    i = pl.program_id(1)
    keep_prev = (i > 0).astype(F32)
    keep_next = (i < seq // bm - 1).astype(F32)
    outs = (v_ref, x1_ref, x2_ref)
    for j in range(3 * D_MODEL // HY_CHUNK):
        c0 = j * HY_CHUNK
        us_ref[...] = jnp.dot(hs_ref[...], w_ref[:, c0:c0 + HY_CHUNK], preferred_element_type=F32)
        us_ref[HALO - SUBLANES:HALO] = us_ref[HALO - SUBLANES:HALO] * keep_prev
        us_ref[HALO + bm:HALO + bm + SUBLANES] = us_ref[HALO + bm:HALO + bm + SUBLANES] * keep_next
        y = _conv3(us_ref, cw_ref[:, c0:c0 + HY_CHUNK], cb_ref[:, c0:c0 + HY_CHUNK], bm)
        off = c0 % D_MODEL
        outs[c0 // D_MODEL][0, :, off:off + HY_CHUNK] = y


def _hy_filter_kernel(z_ref, w1_ref, b1_ref, w2_ref, b2_ref, w3_ref, fr_ref, dl_ref, k_ref, n_ref,
                      *, bm, seq):
    i = pl.program_id(0)
    z = z_ref[...]
    fr = fr_ref[...]
    a = jnp.sin(fr * (jnp.dot(z, w1_ref[...], precision=HIGHEST, preferred_element_type=F32) + b1_ref[...]))
    a = jnp.sin(fr * (jnp.dot(a, w2_ref[...], precision=HIGHEST, preferred_element_type=F32) + b2_ref[...]))
    decay = jnp.exp(-z[:, 0:1] * dl_ref[...])
    t = i * bm + lax.broadcasted_iota(jnp.int32, (bm, 1), 0)
    decay = jnp.where(t == seq, 0.0, decay)

    @pl.when(i == 0)
    def _():
        n_ref[...] = jnp.zeros(n_ref.shape, F32)

    a_hi = a.astype(BF16)
    a_lo = (a - a_hi.astype(F32)).astype(BF16)
    for o in range(HY_ORDER):
        w = w3_ref[0, o]
        w_hi = w.astype(BF16)
        w_lo = (w - w_hi.astype(F32)).astype(BF16)
        f = (jnp.dot(a_hi, w_hi, preferred_element_type=F32) + jnp.dot(a_hi, w_lo, preferred_element_type=F32)
             + jnp.dot(a_lo, w_hi, preferred_element_type=F32)) * decay
        k_ref[o] = f
        n_ref[o:o + 1] += jnp.sum(jnp.abs(f), axis=0, keepdims=True)


def _dft_dot(w_ref, x):
    return jnp.dot(w_ref[...], x.astype(BF16), preferred_element_type=F32)


def _dft_in_kernel(x_ref, w_ref, o_ref, *, n1):
    for s in range(DFT_STEP):
        r = _dft_dot(w_ref, jnp.concatenate([x_ref[0, :, s, :], x_ref[1, :, s, :]], axis=0))
        o_ref[0, s, 0] = r[:n1]
        o_ref[0, s, 1] = r[n1:]


def _dft_filter_in_kernel(x_ref, w_ref, o_ref, *, n1):
    for s in range(DFT_STEP):
        r = _dft_dot(w_ref, jnp.concatenate([x_ref[0, :n1 // 2, s, :], x_ref[0, n1 // 2:, s, :]], axis=0))
        o_ref[0, s, 0] = r[:n1]
        o_ref[0, s, 1] = r[n1:]


def _twiddle(ar, ai, twr, twi):
    return ar * twr - ai * twi, ar * twi + ai * twr


def _dft_filter_mid_kernel(a_ref, w_ref, n_ref, o_ref):
    inv = 1.0 / n_ref[0]
    for j in range(DFT_STEP):
        z = _dft_dot(w_ref.at[j], jnp.concatenate([a_ref[0, :, 0, j, :], a_ref[0, :, 1, j, :]], axis=0))
        o_ref[0, 0, j] = z[:DFT_N2] * inv
        o_ref[0, 1, j] = z[DFT_N2:] * inv


def _dft_mid_kernel(a_ref, kf_ref, w_ref, wc_ref, o_ref):
    for j in range(DFT_STEP):
        z = _dft_dot(w_ref.at[j], jnp.concatenate([a_ref[0, :, 0, j, :], a_ref[0, :, 1, j, :]], axis=0))
        yr, yi = _twiddle(z[:DFT_N2], z[DFT_N2:], kf_ref[0, 0, j], kf_ref[0, 1, j])
        b = _dft_dot(wc_ref.at[j], jnp.concatenate([yr, yi], axis=0))
        o_ref[0, j, 0] = b[:DFT_N2]
        o_ref[0, j, 1] = b[DFT_N2:]


def _dft_out_kernel(b_ref, w_ref, gate_ref, zc_ref, skip_ref, o_ref, *, half):
    skip = skip_ref[...]
    for s in range(DFT_STEP):
        y = _dft_dot(w_ref, jnp.concatenate([b_ref[0, :, 0, s, :], b_ref[0, :, 1, s, :]], axis=0))
        o_ref[0, :, s, :] = gate_ref[0, :, s, :] * (y[:half] + skip * zc_ref[0, :, s, :])
        o_ref[1, :, s, :] = gate_ref[1, :, s, :] * (y[half:] + skip * zc_ref[1, :, s, :])


def _dft_constants(seq):
    n = 2 * seq
    n2 = DFT_N2
    n1 = n // n2
    half = n1 // 2
    k1 = np.arange(n1, dtype=np.float64)
    f1 = np.exp(-2j * np.pi * np.outer(k1, k1) / n1)

    def c2(m):
        return np.block([[m.real, -m.imag], [m.imag, m.real]])

    w_in = c2(f1[:, :half])
    w_fin = np.concatenate([f1.real, f1.imag], axis=0)
    w_out = c2(np.conj(f1)[:half, :]) / n
    as16 = lambda a: jnp.asarray(a, dtype=F32).astype(BF16)
    i1 = jnp.arange(n1, dtype=jnp.int32)[:, None, None]
    i2 = jnp.arange(n2, dtype=jnp.int32)
    phase = (i1 * i2[None, None, :] + n1 * (i2[None, :, None] * i2[None, None, :])) % n
    ang = phase.astype(F32) * (-2.0 * math.pi / n)
    mr, mi = jnp.cos(ang), jnp.sin(ang)
    w_mid = jnp.concatenate([jnp.concatenate([mr, -mi], axis=2), jnp.concatenate([mi, mr], axis=2)], axis=1)
    return dict(n1=n1, half=half, w_in=as16(w_in), w_fin=as16(w_fin), w_out=as16(w_out),
                w_mid=w_mid.astype(BF16), w_mid_c=jnp.swapaxes(w_mid, 1, 2).astype(BF16))


def _hyena_pos_features(seq):
    t = jnp.linspace(0.0, 1.0, seq, dtype=F32)[:, None]
    bands = (HY_EMB_DIM - 1) // 2
    w = 2.0 * math.pi * jnp.arange(seq, dtype=F32) / seq
    f = jnp.linspace(1e-4, bands - 1, bands, dtype=F32)
    ang = w[:, None] * f[None, :]
    z = jnp.concatenate([t, jnp.cos(ang), -jnp.sin(ang)], axis=-1)
    deltas = jnp.abs(jnp.linspace(math.log(HY_TARGET) / HY_SLOW_DECAY,
                                  math.log(HY_TARGET) / HY_FAST_DECAY, D_MODEL, dtype=F32))
    return z, deltas


def _hyena_filter_spectrum(seq, consts, f_w1, f_b1, f_w2, f_b2, f_w3, f_freq):
    d = D_MODEL
    n = 2 * seq
    n1, n2 = consts["n1"], DFT_N2
    z, deltas = _hyena_pos_features(seq)
    z2 = jnp.concatenate([z, z[:1], z[1:][::-1]], axis=0)
    kpad = HY_FILTER_WIDTH - HY_EMB_DIM
    z2 = jnp.pad(z2, ((0, 0), (0, kpad)))
    w1 = jnp.pad(f_w1, ((0, kpad), (0, 0)))
    bm = min(ROW_TILE, seq)
    nblk = n // bm
    k2, norm = pl.pallas_call(
        functools.partial(_hy_filter_kernel, bm=bm, seq=seq),
        grid=(nblk,),
        in_specs=[
            pl.BlockSpec((bm, HY_FILTER_WIDTH), lambda i: (i, 0)),
            _resident((HY_FILTER_WIDTH, HY_FILTER_WIDTH)),
            _resident((1, HY_FILTER_WIDTH)),
            _resident((HY_FILTER_WIDTH, HY_FILTER_WIDTH)),
            _resident((1, HY_FILTER_WIDTH)),
            pl.BlockSpec((1, HY_ORDER, HY_FILTER_WIDTH, d), lambda i: ((2 * i) // nblk, 0, 0, 0)),
            _resident((1, HY_FILTER_WIDTH)),
            _resident((1, d)),
        ],
        out_specs=[
            pl.BlockSpec((HY_ORDER, bm, d), lambda i: (0, i, 0)),
            pl.BlockSpec((HY_ORDER, d), lambda i: (0, 0)),
        ],
        out_shape=[
            jax.ShapeDtypeStruct((HY_ORDER, n, d), F32),
            jax.ShapeDtypeStruct((HY_ORDER, d), F32),
        ],
        compiler_params=_params("arbitrary"),
        name="hy_filter",
    )(z2, w1, f_b1.reshape(1, -1), f_w2, f_b2.reshape(1, -1),
      f_w3.reshape(HY_FILTER_WIDTH, HY_ORDER, 2, d).transpose(2, 1, 0, 3), f_freq.reshape(1, -1),
      deltas.reshape(1, d))

    st, ct = DFT_STEP, DFT_LANES
    a = pl.pallas_call(
        functools.partial(_dft_filter_in_kernel, n1=n1),
        grid=(HY_ORDER, n2 // st),
        in_specs=[
            pl.BlockSpec((1, n1, st, d), lambda o, j: (o, 0, j, 0)),
            _resident((2 * n1, n1)),
        ],
        out_specs=pl.BlockSpec((1, st, 2, n1, d), lambda o, j: (o, j, 0, 0, 0)),
        out_shape=jax.ShapeDtypeStruct((HY_ORDER, n2, 2, n1, d), F32),
        compiler_params=_params("parallel", "parallel"),
        name="hy_filter_dft_in",
    )(k2.reshape(HY_ORDER, n1, n2, d), consts["w_fin"])

    return pl.pallas_call(
        _dft_filter_mid_kernel,
        grid=(HY_ORDER, n1 // st, d // ct),
        in_specs=[
            pl.BlockSpec((1, n2, 2, st, ct), lambda o, k, c: (o, 0, 0, k, c)),
            pl.BlockSpec((st, 2 * n2, 2 * n2), lambda o, k, c: (k, 0, 0)),
            pl.BlockSpec((1, 1, ct), lambda o, k, c: (o, 0, c)),
        ],
        out_specs=pl.BlockSpec((1, 2, st, n2, ct), lambda o, k, c: (o, 0, k, 0, c)),
        out_shape=jax.ShapeDtypeStruct((HY_ORDER, 2, n1, n2, d), F32),
        compiler_params=_params("parallel", "parallel", "parallel"),
        name="hy_filter_dft_mid",
    )(a, consts["w_mid"], norm.reshape(HY_ORDER, 1, d))


def _long_conv_gate(zc, gate, kf, order, skip, consts):
    bsz, seq, d = zc.shape
    n1, half, n2 = consts["n1"], consts["half"], DFT_N2
    pairs = bsz // 2
    st, ct = DFT_STEP, DFT_LANES
    zv = zc.reshape(bsz, half, n2, d)
    a = pl.pallas_call(
        functools.partial(_dft_in_kernel, n1=n1),
        grid=(pairs, n2 // st),
        in_specs=[
            pl.BlockSpec((2, half, st, d), lambda p, j: (p, 0, j, 0)),
            _resident((2 * n1, n1)),
        ],
        out_specs=pl.BlockSpec((1, st, 2, n1, d), lambda p, j: (p, j, 0, 0, 0)),
        out_shape=jax.ShapeDtypeStruct((pairs, n2, 2, n1, d), F32),
        compiler_params=_params("parallel", "parallel"),
        name="hy_dft_in",
    )(zv, consts["w_in"])

    b = pl.pallas_call(
        _dft_mid_kernel,
        grid=(n1 // st, d // ct, pairs),
        in_specs=[
            pl.BlockSpec((1, n2, 2, st, ct), lambda k, c, p: (p, 0, 0, k, c)),
            pl.BlockSpec((1, 2, st, n2, ct), lambda k, c, p: (order, 0, k, 0, c)),
            pl.BlockSpec((st, 2 * n2, 2 * n2), lambda k, c, p: (k, 0, 0)),
            pl.BlockSpec((st, 2 * n2, 2 * n2), lambda k, c, p: (k, 0, 0)),
        ],
        out_specs=pl.BlockSpec((1, st, 2, n2, ct), lambda k, c, p: (p, k, 0, 0, c)),
        out_shape=jax.ShapeDtypeStruct((pairs, n1, 2, n2, d), F32),
        compiler_params=_params("parallel", "parallel", "arbitrary"),
        name="hy_dft_mid",
    )(a, kf, consts["w_mid"], consts["w_mid_c"])

    sig = pl.BlockSpec((2, half, st, ct), lambda p, j, c: (p, 0, j, c))
    out = pl.pallas_call(
        functools.partial(_dft_out_kernel, half=half),
        grid=(pairs, n2 // st, d // ct),
        in_specs=[
            pl.BlockSpec((1, n1, 2, st, ct), lambda p, j, c: (p, 0, 0, j, c)),
            _resident((n1, 2 * n1)),
            sig,
            sig,
            pl.BlockSpec((1, ct), lambda p, j, c: (0, c)),
        ],
        out_specs=sig,
        out_shape=jax.ShapeDtypeStruct((bsz, half, n2, d), F32),
        compiler_params=_params("parallel", "parallel", "parallel"),
        name="hy_dft_out",
    )(b, consts["w_out"], gate.reshape(bsz, half, n2, d), zv, skip.reshape(1, d))
    return out.reshape(bsz, seq, d)


def _hyena_layer(x, mod_l, norm_g, w_in, conv_w, conv_b, kf, skip, consts):
    bsz, seq, d = x.shape
    bm = min(ROW_TILE, seq)
    blk = pl.BlockSpec((1, bm, d), lambda b, i: (b, i, 0))
    shp = jax.ShapeDtypeStruct(x.shape, F32)
    v, x1, x2 = pl.pallas_call(
        functools.partial(_hy_in_kernel, bm=bm, seq=seq),
        grid=(bsz, seq // bm),
        in_specs=_halo_specs(bm, seq, d) + [
            pl.BlockSpec((1, 6, d), lambda b, i: (b, 0, 0)),
            _resident((1, d)),
            _resident((d, 3 * d)),
            _resident((3, 3 * d)),
            _resident((1, 3 * d)),
        ],
        out_specs=[blk, blk, blk],
        out_shape=[shp, shp, shp],
        scratch_shapes=[
            pltpu.VMEM((bm + 2 * HALO, d), BF16),
            pltpu.VMEM((bm + 2 * HALO, HY_CHUNK), F32),
        ],
        compiler_params=_params("parallel", "parallel"),
        name="hy_in",
    )(x, x, x, mod_l, norm_g.reshape(1, d), w_in.astype(BF16), conv_w, conv_b.reshape(1, 3 * d))
    zc = v
    for o, gate in enumerate((x1, x2)):
        zc = _long_conv_gate(zc, gate, kf, o, skip[o], consts)
    return zc


def _s5_prep_kernel(x_ref, mod_ref, g_ref, p_ref, o_ref):
    hs = []
    for b in range(x_ref.shape[0]):
        m = mod_ref[b]
        hs.append(_norm_mod(x_ref[b], g_ref[...], m[1:2], m[0:1]).astype(BF16))
    o_ref[...] = jnp.dot(p_ref[...], jnp.concatenate(hs, axis=0),
                         preferred_element_type=F32).astype(o_ref.dtype)


def _s5_scan_kernel(h_ref, bm_ref, cm_ref, lam_ref, pt_ref, o_ref, sa_ref, sb_ref, sc_ref, y_ref, carry_ref,
                    *, rows, reverse):
    n_slab = D_MODEL // S5_SLAB
    width = S5_SLAB // S5_GROUP * S5_STATE
    n_tiles = rows // SUBLANES
    half = SUBLANES // 2
    bufs = (sa_ref, sb_ref, sc_ref)

    @pl.when(pl.program_id(0) == 0)
    def _():
        carry_ref[...] = jnp.zeros(carry_ref.shape, F32)

    low = lax.broadcasted_iota(jnp.int32, (SUBLANES, width), 0) < half
    take_rolled = jnp.logical_not(low) if reverse else low
    low3 = lax.broadcasted_iota(jnp.int32, (n_tiles, SUBLANES, S5_SLAB), 1) < half
    gets_other = low3 if reverse else jnp.logical_not(low3)

    def project_in(k):
        u = h_ref[:, k * S5_SLAB:(k + 1) * S5_SLAB]
        u3 = u.astype(F32).reshape(n_tiles, SUBLANES, S5_SLAB)
        other = jnp.where(gets_other, pltpu.roll(u3, half, axis=1), 0.0).reshape(rows, S5_SLAB)
        lhs = jnp.concatenate([u, other.astype(BF16)], axis=1)
        bufs[k % 3][...] = jnp.dot(lhs, bm_ref[k], preferred_element_type=F32)

    def project_out(k):
        y_ref[:, k * S5_SLAB:(k + 1) * S5_SLAB] = jnp.dot(
            bufs[k % 3][...].astype(BF16), cm_ref[k], preferred_element_type=F32).astype(y_ref.dtype)

    def scan(k):
        s_ref = bufs[k % 3]
        lr = lam_ref[k, 0]
        li = lam_ref[k, 1]
        cr = carry_ref[k, 0]
        ci = carry_ref[k, 1]
        for i in (reversed(range(n_tiles)) if reverse else range(n_tiles)):
            r0 = i * SUBLANES
            vr = s_ref[r0:r0 + SUBLANES, 0:width]
            vi = s_ref[r0:r0 + SUBLANES, width:2 * width]
            pr = jnp.where(take_rolled, pltpu.roll(cr, half, axis=0), cr)
            pi = jnp.where(take_rolled, pltpu.roll(ci, half, axis=0), ci)
            cr = vr + (lr * pr - li * pi)
            ci = vi + (lr * pi + li * pr)
            s_ref[r0:r0 + SUBLANES, 0:width] = cr
            s_ref[r0:r0 + SUBLANES, width:2 * width] = ci
        carry_ref[k, 0] = cr
        carry_ref[k, 1] = ci

    project_in(0)
    for k in range(n_slab):
        if k + 1 < n_slab:
            project_in(k + 1)
        if k >= 1:
            project_out(k - 1)
        scan(k)
    project_out(n_slab - 1)
    y = jnp.dot(pt_ref[...], y_ref[...], preferred_element_type=F32)
    steps = rows // o_ref.shape[0]
    for b in range(o_ref.shape[0]):
        o_ref[b] = y[b * steps:(b + 1) * steps].astype(o_ref.dtype)


def _s5_glu_kernel(yf_ref, yb_ref, x_ref, mod_ref, g_ref, dsk_ref, w_ref, o_ref):
    m = mod_ref[0]
    x = x_ref[0]
    h = _norm_mod(x, g_ref[...], m[1:2], m[0:1])
    y = jax.nn.gelu(yf_ref[0].astype(F32) + yb_ref[0].astype(F32) + dsk_ref[...] * h)
    g = jnp.dot(y.astype(BF16), w_ref[...], preferred_element_type=F32)
    d = x.shape[-1]
    o_ref[0] = x + m[2:3] * (g[:, :d] * jax.nn.sigmoid(g[:, d:]))


def _s5_tables(a_re, a_im, log_dt, b_re, b_im, c_re, c_im):
    n_slab = D_MODEL // S5_SLAB
    gps = S5_SLAB // S5_GROUP
    lam = lax.complex(jnp.minimum(a_re.astype(F32), -1e-4), a_im.astype(F32))
    dt = jnp.exp(log_dt.astype(F32))[..., None]
    lam_bar = jnp.exp(lam * dt)
    b_bar = ((lam_bar - 1.0) / lam)[..., None] * lax.complex(b_re.astype(F32), b_im.astype(F32))
    eye = jnp.eye(gps, dtype=F32)

    def b_block(part):
        m = jnp.einsum("dkgph,gj->dkghjp", part, eye)
        return m.reshape(2, n_slab, S5_SLAB, gps * S5_STATE)

    def b_rows(b_c):
        bb = b_c.reshape(2, n_slab, gps, S5_STATE, S5_GROUP)
        return jnp.concatenate([b_block(bb.real), b_block(bb.imag)], axis=-1)

    b_mat = jnp.concatenate([b_rows(b_bar), b_rows(b_bar * lam_bar[..., None])], axis=-2)
    cc = lax.complex(c_re.astype(F32), c_im.astype(F32)).reshape(2, n_slab, gps, S5_GROUP, S5_STATE)

    def c_block(part):
        m = jnp.einsum("dkghp,gj->dkgpjh", part, eye)
        return m.reshape(2, n_slab, gps * S5_STATE, S5_SLAB)

    c_mat = jnp.concatenate([c_block(cc.real), -c_block(cc.imag)], axis=-2)
    lam1 = lam_bar.reshape(2, n_slab, 1, gps * S5_STATE)
    lam2 = lam1 * lam1
    half = SUBLANES // 2

    def rows(lo, hi):
        return jnp.concatenate([jnp.broadcast_to(lo, lo.shape[:2] + (half,) + lo.shape[3:]),
                                jnp.broadcast_to(hi, hi.shape[:2] + (half,) + hi.shape[3:])], axis=2)

    carry = jnp.stack([rows(lam1, lam2)[0], rows(lam2, lam1)[1]])
    lam_t = jnp.stack([carry.real, carry.imag], axis=2)
    return b_mat, c_mat, lam_t


def _s5_layer(x, mod_l, norm_g, a_re, a_im, log_dt, b_re, b_im, c_re, c_im, d_skip, w_glu):
    bsz, seq, d = x.shape
    assert 2 * bsz == SUBLANES, bsz
    bm = min(ROW_TILE, seq)
    n_slab = d // S5_SLAB
    width = S5_SLAB // S5_GROUP * S5_STATE
    b_mat, c_mat, lam_t = _s5_tables(a_re, a_im, log_dt, b_re, b_im, c_re, c_im)
    steps = min(S5_STEPS, seq)
    rows = steps * bsz
    nc = seq // steps
    b_mat = b_mat.astype(BF16)
    c_mat = c_mat.astype(BF16)
    r = np.arange(rows)
    perm_np = np.zeros((rows, rows), np.float32)
    perm_np[r, (r % bsz) * steps + r // bsz] = 1.0
    perm = jnp.asarray(perm_np, dtype=BF16)

    h2 = pl.pallas_call(
        _s5_prep_kernel,
        grid=(nc,),
        in_specs=[
            pl.BlockSpec((bsz, steps, d), lambda i: (0, i, 0)),
            _resident((bsz, 6, d)),
            _resident((1, d)),
            _resident((rows, rows)),
        ],
        out_specs=pl.BlockSpec((rows, d), lambda i: (i, 0)),
        out_shape=jax.ShapeDtypeStruct((seq * bsz, d), BF16),
        compiler_params=_params("parallel"),
        name="s5_prep",
    )(x, mod_l, norm_g.reshape(1, d), perm)

    def scan(direction):
        reverse = direction == 1
        chunk = (lambda c: nc - 1 - c) if reverse else (lambda c: c)
        return pl.pallas_call(
            functools.partial(_s5_scan_kernel, rows=rows, reverse=reverse),
            grid=(nc,),
            in_specs=[
                pl.BlockSpec((rows, d), lambda c: (chunk(c), 0)),
                _resident((n_slab, 2 * S5_SLAB, 2 * width)),
                _resident((n_slab, 2 * width, S5_SLAB)),
                _resident((n_slab, 2, SUBLANES, width)),
                _resident((rows, rows)),
            ],
            out_specs=pl.BlockSpec((bsz, steps, d), lambda c: (0, chunk(c), 0)),
            out_shape=jax.ShapeDtypeStruct((bsz, seq, d), BF16),
            scratch_shapes=[
                pltpu.VMEM((rows, 2 * width), F32),
                pltpu.VMEM((rows, 2 * width), F32),
                pltpu.VMEM((rows, 2 * width), F32),
                pltpu.VMEM((rows, d), BF16),
                pltpu.VMEM((n_slab, 2, SUBLANES, width), F32),
            ],
            compiler_params=_params("arbitrary"),
            name="s5_scan_bwd" if reverse else "s5_scan_fwd",
        )(h2, b_mat[direction], c_mat[direction], lam_t[direction], perm.T)

    y_f = scan(0)
    y_b = scan(1)
    return pl.pallas_call(
        _s5_glu_kernel,
        grid=(bsz, seq // bm),
        in_specs=[
            pl.BlockSpec((1, bm, d), lambda b, i: (b, i, 0)),
            pl.BlockSpec((1, bm, d), lambda b, i: (b, i, 0)),
            pl.BlockSpec((1, bm, d), lambda b, i: (b, i, 0)),
            pl.BlockSpec((1, 6, d), lambda b, i: (b, 0, 0)),
            _resident((1, d)),
            _resident((1, d)),
            _resident((d, 2 * d)),
        ],
        out_specs=pl.BlockSpec((1, bm, d), lambda b, i: (b, i, 0)),
        out_shape=jax.ShapeDtypeStruct(x.shape, F32),
        compiler_params=_params("parallel", "parallel"),
        name="s5_glu",
    )(y_f, y_b, x, mod_l, norm_g.reshape(1, d), d_skip.reshape(1, d), w_glu.astype(BF16))


def kernel(x, c, ada_w, ada_b, norm1_g, norm2_g, final_g, attn_w_qkv, attn_w_o, attn_q_gain, attn_k_gain, hy_w_in, hy_conv_w, hy_conv_b, hy_f_w1, hy_f_b1, hy_f_w2, hy_f_b2, hy_f_w3, hy_f_freq, hy_skip, hy_w_out, s5_A_re, s5_A_im, s5_log_dt, s5_B_re, s5_B_im, s5_C_re, s5_C_im, s5_D, s5_w_glu, ffn_w_up, ffn_conv_w, ffn_conv_b, ffn_w_down):
    seq = x.shape[1]
    mod = _ada_mod(c, ada_w, ada_b)
    cos, sin = _rope_tables(seq)
    consts = _dft_constants(seq)
    for i in range(DEPTH):
        m, j = i % N_MIXERS, i // N_MIXERS
        mixer_out = None
        if m == 0:
            o = _attention_layer(x, mod[i], norm1_g[i], attn_w_qkv[j], attn_q_gain[j], attn_k_gain[j], cos, sin)
            mixer_out = (o, attn_w_o[j])
        elif m == 1:
            kf = _hyena_filter_spectrum(seq, consts, hy_f_w1[j], hy_f_b1[j], hy_f_w2[j], hy_f_b2[j],
                                        hy_f_w3[j], hy_f_freq[j])
            zc = _hyena_layer(x, mod[i], norm1_g[i], hy_w_in[j], hy_conv_w[j], hy_conv_b[j], kf,
                              hy_skip[j], consts)
            mixer_out = (zc, hy_w_out[j])
        else:
            x = _s5_layer(x, mod[i], norm1_g[i], s5_A_re[j], s5_A_im[j], s5_log_dt[j], s5_B_re[j],
                          s5_B_im[j], s5_C_re[j], s5_C_im[j], s5_D[j], s5_w_glu[j])
        x = _conv_ffn_layer(x, mod[i], norm2_g[i], ffn_w_up[i], ffn_conv_w[i], ffn_conv_b[i],
                            ffn_w_down[i], final_g, final_norm=(i == DEPTH - 1), mixer_out=mixer_out)
    return x
```

```python
import functools
import math

import numpy as np
import jax
import jax.numpy as jnp
from jax import lax
from jax.experimental import pallas as pl
from jax.experimental.pallas import tpu as pltpu

D_MODEL = 1024
DEPTH = 4
N_MIXERS = 3
GRID_W = 64
EPS = 1e-6
N_HEADS = 16
HEAD_DIM = D_MODEL // N_HEADS
N_KV_HEADS = 4
KV_GROUP = N_HEADS // N_KV_HEADS
ROPE_THETA = 10000.0
HY_ORDER = 2
HY_EMB_DIM = 33
HY_FILTER_WIDTH = 64
HY_FAST_DECAY = 0.3
HY_SLOW_DECAY = 1.5
HY_TARGET = 1e-2
S5_GROUP = 16
S5_STATE = 64
D_FF = 2816

F32 = jnp.float32
BF16 = jnp.bfloat16
HIGHEST = lax.Precision.HIGHEST

LANES = 128
SUBLANES = 8
BF16_ROWS = 16
VMEM_LIMIT = 56 * 1024 * 1024

ADA_TILE = 1536
ROW_TILE = 512
HALO = BF16_ROWS
HY_CHUNK = 512
Q_TILE = 512
KV_TILE = 1024
MAX_FIXED_SHIFT = 40.0
DFT_N2 = 128
DFT_STEP = SUBLANES
DFT_LANES = 512
S5_SLAB = 128
S5_STEPS = 128


def _params(*sem):
    return pltpu.CompilerParams(dimension_semantics=sem, vmem_limit_bytes=VMEM_LIMIT)


def _resident(shape):
    zeros = (0,) * len(shape)
    return pl.BlockSpec(shape, lambda *_: zeros, pipeline_mode=pl.Buffered(1))


def _norm_mod(x, g, scale, shift):
    y = x * lax.rsqrt(jnp.mean(x * x, axis=-1, keepdims=True) + EPS)
    return (y * g) * (1.0 + scale) + shift


def _ada_kernel(c_ref, w_ref, b_ref, o_ref):
    c = c_ref[...]
    a = c * jax.nn.sigmoid(c)
    o_ref[0] = jnp.dot(a, w_ref[0], precision=HIGHEST, preferred_element_type=F32) + b_ref[0]


def _ada_mod(c, ada_w, ada_b):
    bsz, d = c.shape
    rows = -(-bsz // SUBLANES) * SUBLANES
    cp = jnp.pad(c, ((0, rows - bsz), (0, 0)))
    tn = ADA_TILE
    out = pl.pallas_call(
        _ada_kernel,
        grid=(DEPTH, 6 * d // tn),
        in_specs=[
            pl.BlockSpec((rows, d), lambda i, j: (0, 0)),
            pl.BlockSpec((1, d, tn), lambda i, j: (i, 0, j)),
            pl.BlockSpec((1, 1, tn), lambda i, j: (i, 0, j)),
        ],
        out_specs=pl.BlockSpec((1, rows, tn), lambda i, j: (i, 0, j)),
        out_shape=jax.ShapeDtypeStruct((DEPTH, rows, 6 * d), F32),
        compiler_params=_params("arbitrary", "arbitrary"),
        name="ada_mod",
    )(cp, ada_w, ada_b.reshape(DEPTH, 1, 6 * d))
    return out[:, :bsz].reshape(DEPTH, bsz, 6, d)


def _qkv_kernel(x_ref, mod_ref, g_ref, w_ref, seg_ref, c_ref, s1_ref, s2_ref, qg_ref, kg_ref,
                q_ref, klo_ref, khi_ref, v_ref):
    m = mod_ref[0]
    h = _norm_mod(x_ref[0], g_ref[...], m[1:2], m[0:1])
    r = jnp.dot(h.astype(BF16), w_ref[...], preferred_element_type=F32)
    c = c_ref[...]
    s1 = s1_ref[...]
    s2 = s2_ref[...]
    seg = seg_ref[...]
    quarter = HEAD_DIM // 2

    def norm_rope(xr, gain):
        ss = xr * xr
        hi = ss.astype(BF16)
        lo = (ss - hi.astype(F32)).astype(BF16)
        tot = (jnp.dot(hi, seg, preferred_element_type=F32) + jnp.dot(lo, seg, preferred_element_type=F32))
        n = xr * lax.rsqrt(tot * (1.0 / HEAD_DIM) + EPS) * gain
        return (n * c + pltpu.roll(n, LANES - quarter, axis=1) * s1 + pltpu.roll(n, quarter, axis=1) * s2)

    qg = qg_ref[...]
    kg = kg_ref[...]
    n_q = N_HEADS * HEAD_DIM // LANES
    for p in range(n_q):
        q_ref[0, p] = norm_rope(r[:, p * LANES:(p + 1) * LANES], qg).astype(BF16)
    low = lax.broadcasted_iota(jnp.int32, (r.shape[0], LANES), 1) < HEAD_DIM
    for g in range(N_KV_HEADS):
        kk = norm_rope(r[:, (n_q + g) * LANES:(n_q + g + 1) * LANES], kg)
        klo_ref[0, g] = jnp.where(low, kk, 0.0).astype(BF16)
        khi_ref[0, g] = jnp.where(low, 0.0, kk).astype(BF16)
        vv = r[:, (n_q + N_KV_HEADS + g) * LANES:(n_q + N_KV_HEADS + g + 1) * LANES]
        v_ref[0, g] = jnp.where(low, vv, 1.0).astype(BF16)


def _flash_kernel(q_ref, qn_ref, klo_ref, khi_ref, v_ref, shift_ref, o_ref, sa_ref, sb_ref, m_ref, acc_ref,
                  *, n_kv, fixed_shift):
    rows = KV_GROUP * Q_TILE
    m_ref[...] = jnp.full(m_ref.shape, -jnp.inf, F32)
    acc_ref[...] = jnp.zeros(acc_ref.shape, F32)

    def scores(s_ref, src_ref, j):
        start = pl.multiple_of(j * KV_TILE, KV_TILE)
        qq = src_ref[0].reshape(rows // 2, LANES)
        for half, kk_ref in enumerate((klo_ref, khi_ref)):
            kj = kk_ref[0, 0, pl.ds(start, KV_TILE), :]
            s_ref[half * (rows // 2):(half + 1) * (rows // 2)] = lax.dot_general(
                qq, kj, (((1,), (1,)), ((), ())), preferred_element_type=F32)

    def consume(s_ref, j):
        s = s_ref[...]
        start = pl.multiple_of(j * KV_TILE, KV_TILE)
        vj = v_ref[0, 0, pl.ds(start, KV_TILE), :]
        if fixed_shift:
            shift = shift_ref[...]
            p = jnp.concatenate([jnp.exp(s[:, c * LANES:(c + 1) * LANES] - shift)
                                 for c in range(KV_TILE // LANES)], axis=1)
            acc_ref[...] += jnp.dot(p.astype(BF16), vj, preferred_element_type=F32)
            return
        m_prev = m_ref[...]
        m_next = jnp.maximum(m_prev, jnp.max(s, axis=-1, keepdims=True))
        alpha = jnp.exp(m_prev - m_next)
        p = jnp.concatenate([jnp.exp(s[:, c * LANES:(c + 1) * LANES] - m_next)
                             for c in range(KV_TILE // LANES)], axis=1)
        acc_ref[...] = alpha * acc_ref[...] + jnp.dot(p.astype(BF16), vj, preferred_element_type=F32)
        m_ref[...] = m_next

    @pl.when(pl.program_id(2) == 0)
    def _():
        scores(sa_ref, q_ref, 0)

    def body(jj, carry):
        scores(sb_ref, q_ref, 2 * jj + 1)
        consume(sa_ref, 2 * jj)
        scores(sa_ref, q_ref, 2 * jj + 2)
        consume(sb_ref, 2 * jj + 1)
        return carry

    lax.fori_loop(0, n_kv // 2 - 1, body, 0)
    scores(sb_ref, q_ref, n_kv - 1)
    consume(sa_ref, n_kv - 2)
    scores(sa_ref, qn_ref, 0)
    consume(sb_ref, n_kv - 1)
    acc = acc_ref[...]
    o = acc[:, :HEAD_DIM] / acc[:, HEAD_DIM:HEAD_DIM + 1]
    o_ref[0] = jnp.concatenate(
        [o[g * Q_TILE:(g + 1) * Q_TILE] for g in ((j % 2) * (KV_GROUP // 2) + j // 2 for j in range(KV_GROUP))],
        axis=-1).astype(o_ref.dtype)


def _rope_tables(seq):
    rows = seq // GRID_W
    n_freq = HEAD_DIM // 4
    inv = 1.0 / (ROPE_THETA ** (jnp.arange(n_freq, dtype=F32) / n_freq))
    r = jnp.arange(rows, dtype=F32)
    col = jnp.arange(GRID_W, dtype=F32)
    ang_r = jnp.broadcast_to(r[:, None, None] * inv, (rows, GRID_W, n_freq))
    ang_c = jnp.broadcast_to(col[None, :, None] * inv, (rows, GRID_W, n_freq))
    ang = jnp.concatenate([ang_r, ang_c], axis=-1).reshape(seq, 2 * n_freq)
    return jnp.cos(ang), jnp.sin(ang)


def _attention_layer(x, mod_l, norm_g, w_qkv, q_gain, k_gain, cos, sin):
    bsz, seq, d = x.shape
    bm = min(ROW_TILE, seq)
    perm = np.concatenate([np.arange(0, HEAD_DIM, 2), np.arange(1, HEAD_DIM, 2)])
    head_cols = lambda hh: hh * HEAD_DIM + perm
    k0 = N_HEADS * HEAD_DIM
    v0 = k0 + N_KV_HEADS * HEAD_DIM
    q_cols = np.concatenate([head_cols(hh) for hh in range(N_HEADS)])
    k_cols = np.concatenate([np.tile(k0 + head_cols(g), 2) for g in range(N_KV_HEADS)])
    w_v = jnp.pad(w_qkv[:, v0:].reshape(d, N_KV_HEADS, HEAD_DIM), ((0, 0), (0, 0), (0, LANES - HEAD_DIM)))
    w = jnp.concatenate([w_qkv[:, q_cols], w_qkv[:, k_cols], w_v.reshape(d, N_KV_HEADS * LANES)],
                        axis=1).astype(BF16)
    n_out = w.shape[1]
    reps = LANES // (HEAD_DIM // 2)
    zero = jnp.zeros_like(sin)
    c_t = jnp.tile(cos, (1, reps))
    s1_t = jnp.tile(jnp.concatenate([-sin, zero], axis=1), (1, reps // 2))
    s2_t = jnp.tile(jnp.concatenate([zero, sin], axis=1), (1, reps // 2))
    seg = jnp.asarray(np.kron(np.eye(LANES // HEAD_DIM), np.ones((HEAD_DIM, HEAD_DIM))), dtype=BF16)
    tab = pl.BlockSpec((bm, LANES), lambda b, i: (i, 0))
    n_pairs = N_HEADS * HEAD_DIM // LANES
    kv_spec = pl.BlockSpec((1, N_KV_HEADS, bm, LANES), lambda b, i: (b, 0, i, 0))
    kv_shape = jax.ShapeDtypeStruct((bsz, N_KV_HEADS, seq, LANES), BF16)
    q, k_lo, k_hi, v = pl.pallas_call(
        _qkv_kernel,
        grid=(bsz, seq // bm),
        in_specs=[
            pl.BlockSpec((1, bm, d), lambda b, i: (b, i, 0)),
            pl.BlockSpec((1, 6, d), lambda b, i: (b, 0, 0)),
            _resident((1, d)),
            _resident((d, n_out)),
            _resident((LANES, LANES)),
            tab,
            tab,
            tab,
            _resident((1, LANES)),
            _resident((1, LANES)),
        ],
        out_specs=[
            pl.BlockSpec((1, n_pairs, bm, LANES), lambda b, i: (b, 0, i, 0)),
            kv_spec,
            kv_spec,
            kv_spec,
        ],
        out_shape=[
            jax.ShapeDtypeStruct((bsz, n_pairs, seq, LANES), BF16),
            kv_shape,
            kv_shape,
            kv_shape,
        ],
        compiler_params=_params("parallel", "parallel"),
        name="attn_qkv",
    )(x, mod_l, norm_g.reshape(1, d), w, seg, c_t, s1_t, s2_t,
      jnp.tile(q_gain[perm] * HEAD_DIM ** -0.5, 2).reshape(1, LANES), jnp.tile(k_gain[perm], 2).reshape(1, LANES))

    score_bound = HEAD_DIM ** 0.5 * jnp.max(jnp.abs(q_gain)) * jnp.max(jnp.abs(k_gain))
    return _flash_attention(q, k_lo, k_hi, v, score_bound)


def _flash_attention(q, k_lo, k_hi, v, score_bound):
    bsz, _, seq, _ = q.shape
    assert (seq // KV_TILE) % 2 == 0, seq
    rows = KV_GROUP * Q_TILE
    n_q = seq // Q_TILE
    kv_spec = pl.BlockSpec((1, 1, seq, LANES), lambda b, g, i: (b, g, 0, 0))
    shift = jnp.full((1, LANES), score_bound, F32)

    def attend(fixed_shift):
        return pl.pallas_call(
            functools.partial(_flash_kernel, n_kv=seq // KV_TILE, fixed_shift=fixed_shift),
            grid=(bsz, N_KV_HEADS, n_q),
            in_specs=[
                pl.BlockSpec((1, KV_GROUP // 2, Q_TILE, LANES), lambda b, g, i: (b, g, i, 0)),
                pl.BlockSpec((1, KV_GROUP // 2, Q_TILE, LANES),
                             lambda b, g, i: (b, g, jnp.minimum(i + 1, n_q - 1), 0)),
                kv_spec,
                kv_spec,
                kv_spec,
                _resident((1, LANES)),
            ],
            out_specs=pl.BlockSpec((1, Q_TILE, KV_GROUP * HEAD_DIM), lambda b, g, i: (b, i, g)),
            out_shape=jax.ShapeDtypeStruct((bsz, seq, N_HEADS * HEAD_DIM), BF16),
            scratch_shapes=[
                pltpu.VMEM((rows, KV_TILE), F32),
                pltpu.VMEM((rows, KV_TILE), F32),
                pltpu.VMEM((rows, LANES), F32),
                pltpu.VMEM((rows, 2 * HEAD_DIM), F32),
            ],
            compiler_params=_params("parallel", "parallel", "arbitrary"),
            name="attn_flash_bounded" if fixed_shift else "attn_flash",
        )(q, q, k_lo, k_hi, v, shift)

    return lax.cond(score_bound < MAX_FIXED_SHIFT, lambda: attend(True), lambda: attend(False))


def _halo_rows(i, bm, seq):
    t = i * bm - HALO + lax.broadcasted_iota(jnp.int32, (bm + 2 * HALO, 1), 0)
    return (t >= 0) & (t < seq)


def _fill_window(hs_ref, xp_ref, x_ref, xn_ref, g, scale, shift, bm):
    hs_ref[0:HALO] = _norm_mod(xp_ref[0], g, scale, shift).astype(BF16)
    hs_ref[HALO:HALO + bm] = _norm_mod(x_ref[0], g, scale, shift).astype(BF16)
    hs_ref[HALO + bm:] = _norm_mod(xn_ref[0], g, scale, shift).astype(BF16)


def _conv3(win_ref, cw, cb, bm):
    return (win_ref[pl.ds(HALO - 1, bm), :] * cw[0:1] + win_ref[pl.ds(HALO, bm), :] * cw[1:2]
            + win_ref[pl.ds(HALO + 1, bm), :] * cw[2:3] + cb)


def _ffn_tail(x_main, m, hs_ref, gs_ref, wu_ref, cw_ref, cb_ref, wd_ref, fg_ref, o_ref, *, bm, seq, final_norm):
    valid = _halo_rows(pl.program_id(1), bm, seq)
    gate = jnp.dot(hs_ref[...], wu_ref[:, :D_FF], preferred_element_type=F32)
    gs_ref[...] = jnp.where(valid, gate, 0.0)
    gate = _conv3(gs_ref, cw_ref[...], cb_ref[...], bm)
    val = jnp.dot(hs_ref[HALO:HALO + bm], wu_ref[:, D_FF:], preferred_element_type=F32)
    a = (gate * jax.nn.sigmoid(gate)) * val
    y = x_main + m[5:6] * jnp.dot(a.astype(BF16), wd_ref[...], preferred_element_type=F32)
    if final_norm:
        y = y * lax.rsqrt(jnp.mean(y * y, axis=-1, keepdims=True) + EPS) * fg_ref[...]
    o_ref[0] = y


def _ffn_kernel(xp_ref, x_ref, xn_ref, mod_ref, g_ref, wu_ref, cw_ref, cb_ref, wd_ref, fg_ref,
                o_ref, hs_ref, gs_ref, *, bm, seq, final_norm):
    m = mod_ref[0]
    _fill_window(hs_ref, xp_ref, x_ref, xn_ref, g_ref[...], m[4:5], m[3:4], bm)
    _ffn_tail(x_ref[0], m, hs_ref, gs_ref, wu_ref, cw_ref, cb_ref, wd_ref, fg_ref, o_ref,
              bm=bm, seq=seq, final_norm=final_norm)


def _proj_ffn_kernel(ap_ref, a_ref, an_ref, wp_ref, xp_ref, x_ref, xn_ref, mod_ref, g_ref, wu_ref, cw_ref,
                     cb_ref, wd_ref, fg_ref, o_ref, hs_ref, gs_ref, aw_ref, xw_ref, *, bm, seq, final_norm):
    m = mod_ref[0]
    aw_ref[0:HALO] = ap_ref[0].astype(BF16)
    aw_ref[HALO:HALO + bm] = a_ref[0].astype(BF16)
    aw_ref[HALO + bm:] = an_ref[0].astype(BF16)
    xw_ref[0:HALO] = xp_ref[0]
    xw_ref[HALO:HALO + bm] = x_ref[0]
    xw_ref[HALO + bm:] = xn_ref[0]
    xw_ref[...] = xw_ref[...] + m[2:3] * jnp.dot(aw_ref[...], wp_ref[...], preferred_element_type=F32)
    hs_ref[...] = _norm_mod(xw_ref[...], g_ref[...], m[4:5], m[3:4]).astype(BF16)
    _ffn_tail(xw_ref[HALO:HALO + bm], m, hs_ref, gs_ref, wu_ref, cw_ref, cb_ref, wd_ref, fg_ref, o_ref,
              bm=bm, seq=seq, final_norm=final_norm)


def _halo_specs(bm, seq, d):
    nh = seq // HALO
    per = bm // HALO
    return [
        pl.BlockSpec((1, HALO, d), lambda b, i: (b, jnp.maximum(i * per - 1, 0), 0)),
        pl.BlockSpec((1, bm, d), lambda b, i: (b, i, 0)),
        pl.BlockSpec((1, HALO, d), lambda b, i: (b, jnp.minimum((i + 1) * per, nh - 1), 0)),
    ]


def _conv_ffn_layer(x, mod_l, norm_g, w_up, conv_w, conv_b, w_down, final_g, final_norm, mixer_out=None):
    bsz, seq, d = x.shape
    bm = min(ROW_TILE, seq)
    win = bm + 2 * HALO
    ffn_specs = _halo_specs(bm, seq, d) + [
        pl.BlockSpec((1, 6, d), lambda b, i: (b, 0, 0)),
        _resident((1, d)),
        _resident((d, 2 * D_FF)),
        _resident((3, D_FF)),
        _resident((1, D_FF)),
        _resident((D_FF, d)),
        _resident((1, d)),
    ]
    ffn_args = (x, x, x, mod_l, norm_g.reshape(1, d), w_up.astype(BF16), conv_w, conv_b.reshape(1, D_FF),
                w_down.astype(BF16), final_g.reshape(1, d))
    scratch = [pltpu.VMEM((win, d), BF16), pltpu.VMEM((win, D_FF), F32)]
    if mixer_out is None:
        body, specs, args = _ffn_kernel, ffn_specs, ffn_args
    else:
        a, w = mixer_out
        kdim = a.shape[-1]
        body = _proj_ffn_kernel
        specs = _halo_specs(bm, seq, kdim) + [_resident((kdim, d))] + ffn_specs
        args = (a, a, a, w.astype(BF16)) + ffn_args
        scratch = scratch + [pltpu.VMEM((win, kdim), BF16), pltpu.VMEM((win, d), F32)]
    return pl.pallas_call(
        functools.partial(body, bm=bm, seq=seq, final_norm=final_norm),
        grid=(bsz, seq // bm),
        in_specs=specs,
        out_specs=pl.BlockSpec((1, bm, d), lambda b, i: (b, i, 0)),
        out_shape=jax.ShapeDtypeStruct(x.shape, F32),
        scratch_shapes=scratch,
        compiler_params=_params("parallel", "parallel"),
        name="conv_ffn",
    )(*args)


def _hy_in_kernel(xp_ref, x_ref, xn_ref, mod_ref, g_ref, w_ref, cw_ref, cb_ref,
                  v_ref, x1_ref, x2_ref, hs_ref, us_ref, *, bm, seq):
    m = mod_ref[0]
    _fill_window(hs_ref, xp_ref, x_ref, xn_ref, g_ref[...], m[1:2], m[0:1], bm)
    i = pl.program_id(1)
    keep_prev = (i > 0).astype(F32)
    keep_next = (i < seq // bm - 1).astype(F32)
    outs = (v_ref, x1_ref, x2_ref)
    for j in range(3 * D_MODEL // HY_CHUNK):
        c0 = j * HY_CHUNK
        us_ref[...] = jnp.dot(hs_ref[...], w_ref[:, c0:c0 + HY_CHUNK], preferred_element_type=F32)
        us_ref[HALO - SUBLANES:HALO] = us_ref[HALO - SUBLANES:HALO] * keep_prev
        us_ref[HALO + bm:HALO + bm + SUBLANES] = us_ref[HALO + bm:HALO + bm + SUBLANES] * keep_next
        y = _conv3(us_ref, cw_ref[:, c0:c0 + HY_CHUNK], cb_ref[:, c0:c0 + HY_CHUNK], bm)
        off = c0 % D_MODEL
        outs[c0 // D_MODEL][0, :, off:off + HY_CHUNK] = y


def _hy_filter_kernel(z_ref, w1_ref, b1_ref, w2_ref, b2_ref, w3_ref, fr_ref, dl_ref, k_ref, n_ref,
                      *, bm, seq):
    i = pl.program_id(0)
    z = z_ref[...]
    fr = fr_ref[...]
    a = jnp.sin(fr * (jnp.dot(z, w1_ref[...], precision=HIGHEST, preferred_element_type=F32) + b1_ref[...]))
    a = jnp.sin(fr * (jnp.dot(a, w2_ref[...], precision=HIGHEST, preferred_element_type=F32) + b2_ref[...]))
    decay = jnp.exp(-z[:, 0:1] * dl_ref[...])
    t = i * bm + lax.broadcasted_iota(jnp.int32, (bm, 1), 0)
    decay = jnp.where(t == seq, 0.0, decay)

    @pl.when(i == 0)
    def _():
        n_ref[...] = jnp.zeros(n_ref.shape, F32)

    a_hi = a.astype(BF16)
    a_lo = (a - a_hi.astype(F32)).astype(BF16)
    for o in range(HY_ORDER):
        w = w3_ref[0, o]
        w_hi = w.astype(BF16)
        w_lo = (w - w_hi.astype(F32)).astype(BF16)
        f = (jnp.dot(a_hi, w_hi, preferred_element_type=F32) + jnp.dot(a_hi, w_lo, preferred_element_type=F32)
             + jnp.dot(a_lo, w_hi, preferred_element_type=F32)) * decay
        k_ref[o] = f
        n_ref[o:o + 1] += jnp.sum(jnp.abs(f), axis=0, keepdims=True)


def _dft_dot(w_ref, x):
    return jnp.dot(w_ref[...], x.astype(BF16), preferred_element_type=F32)


def _dft_in_kernel(x_ref, w_ref, o_ref, *, n1):
    for s in range(DFT_STEP):
        r = _dft_dot(w_ref, jnp.concatenate([x_ref[0, :, s, :], x_ref[1, :, s, :]], axis=0))
        o_ref[0, s, 0] = r[:n1]
        o_ref[0, s, 1] = r[n1:]


def _dft_filter_in_kernel(x_ref, w_ref, o_ref, *, n1):
    for s in range(DFT_STEP):
        r = _dft_dot(w_ref, jnp.concatenate([x_ref[0, :n1 // 2, s, :], x_ref[0, n1 // 2:, s, :]], axis=0))
        o_ref[0, s, 0] = r[:n1]
        o_ref[0, s, 1] = r[n1:]


def _twiddle(ar, ai, twr, twi):
    return ar * twr - ai * twi, ar * twi + ai * twr


def _dft_filter_mid_kernel(a_ref, w_ref, n_ref, o_ref):
    inv = 1.0 / n_ref[0]
    for j in range(DFT_STEP):
        z = _dft_dot(w_ref.at[j], jnp.concatenate([a_ref[0, :, 0, j, :], a_ref[0, :, 1, j, :]], axis=0))
        o_ref[0, 0, j] = z[:DFT_N2] * inv
        o_ref[0, 1, j] = z[DFT_N2:] * inv


def _dft_mid_kernel(a_ref, kf_ref, w_ref, wc_ref, o_ref):
    for j in range(DFT_STEP):
        z = _dft_dot(w_ref.at[j], jnp.concatenate([a_ref[0, :, 0, j, :], a_ref[0, :, 1, j, :]], axis=0))
        yr, yi = _twiddle(z[:DFT_N2], z[DFT_N2:], kf_ref[0, 0, j], kf_ref[0, 1, j])
        b = _dft_dot(wc_ref.at[j], jnp.concatenate([yr, yi], axis=0))
        o_ref[0, j, 0] = b[:DFT_N2]
        o_ref[0, j, 1] = b[DFT_N2:]


def _dft_out_kernel(b_ref, w_ref, gate_ref, zc_ref, skip_ref, o_ref, *, half):
    skip = skip_ref[...]
    for s in range(DFT_STEP):
        y = _dft_dot(w_ref, jnp.concatenate([b_ref[0, :, 0, s, :], b_ref[0, :, 1, s, :]], axis=0))
        o_ref[0, :, s, :] = gate_ref[0, :, s, :] * (y[:half] + skip * zc_ref[0, :, s, :])
        o_ref[1, :, s, :] = gate_ref[1, :, s, :] * (y[half:] + skip * zc_ref[1, :, s, :])


def _dft_constants(seq):
    n = 2 * seq
    n2 = DFT_N2
    n1 = n // n2
    half = n1 // 2
    k1 = np.arange(n1, dtype=np.float64)
    f1 = np.exp(-2j * np.pi * np.outer(k1, k1) / n1)

    def c2(m):
        return np.block([[m.real, -m.imag], [m.imag, m.real]])

    w_in = c2(f1[:, :half])
    w_fin = np.concatenate([f1.real, f1.imag], axis=0)
    w_out = c2(np.conj(f1)[:half, :]) / n
    as16 = lambda a: jnp.asarray(a, dtype=F32).astype(BF16)
    i1 = jnp.arange(n1, dtype=jnp.int32)[:, None, None]
    i2 = jnp.arange(n2, dtype=jnp.int32)
    phase = (i1 * i2[None, None, :] + n1 * (i2[None, :, None] * i2[None, None, :])) % n
    ang = phase.astype(F32) * (-2.0 * math.pi / n)
    mr, mi = jnp.cos(ang), jnp.sin(ang)
    w_mid = jnp.concatenate([jnp.concatenate([mr, -mi], axis=2), jnp.concatenate([mi, mr], axis=2)], axis=1)
    return dict(n1=n1, half=half, w_in=as16(w_in), w_fin=as16(w_fin), w_out=as16(w_out),
                w_mid=w_mid.astype(BF16), w_mid_c=jnp.swapaxes(w_mid, 1, 2).astype(BF16))


def _hyena_pos_features(seq):
    t = jnp.linspace(0.0, 1.0, seq, dtype=F32)[:, None]
    bands = (HY_EMB_DIM - 1) // 2
    w = 2.0 * math.pi * jnp.arange(seq, dtype=F32) / seq
    f = jnp.linspace(1e-4, bands - 1, bands, dtype=F32)
    ang = w[:, None] * f[None, :]
    z = jnp.concatenate([t, jnp.cos(ang), -jnp.sin(ang)], axis=-1)
    deltas = jnp.abs(jnp.linspace(math.log(HY_TARGET) / HY_SLOW_DECAY,
                                  math.log(HY_TARGET) / HY_FAST_DECAY, D_MODEL, dtype=F32))
    return z, deltas


def _hyena_filter_spectrum(seq, consts, f_w1, f_b1, f_w2, f_b2, f_w3, f_freq):
    d = D_MODEL
    n = 2 * seq
    n1, n2 = consts["n1"], DFT_N2
    z, deltas = _hyena_pos_features(seq)
    z2 = jnp.concatenate([z, z[:1], z[1:][::-1]], axis=0)
    kpad = HY_FILTER_WIDTH - HY_EMB_DIM
    z2 = jnp.pad(z2, ((0, 0), (0, kpad)))
    w1 = jnp.pad(f_w1, ((0, kpad), (0, 0)))
    bm = min(ROW_TILE, seq)
    nblk = n // bm
    k2, norm = pl.pallas_call(
        functools.partial(_hy_filter_kernel, bm=bm, seq=seq),
        grid=(nblk,),
        in_specs=[
            pl.BlockSpec((bm, HY_FILTER_WIDTH), lambda i: (i, 0)),
            _resident((HY_FILTER_WIDTH, HY_FILTER_WIDTH)),
            _resident((1, HY_FILTER_WIDTH)),
            _resident((HY_FILTER_WIDTH, HY_FILTER_WIDTH)),
            _resident((1, HY_FILTER_WIDTH)),
            pl.BlockSpec((1, HY_ORDER, HY_FILTER_WIDTH, d), lambda i: ((2 * i) // nblk, 0, 0, 0)),
            _resident((1, HY_FILTER_WIDTH)),
            _resident((1, d)),
        ],
        out_specs=[
            pl.BlockSpec((HY_ORDER, bm, d), lambda i: (0, i, 0)),
            pl.BlockSpec((HY_ORDER, d), lambda i: (0, 0)),
        ],
        out_shape=[
            jax.ShapeDtypeStruct((HY_ORDER, n, d), F32),
            jax.ShapeDtypeStruct((HY_ORDER, d), F32),
        ],
        compiler_params=_params("arbitrary"),
        name="hy_filter",
    )(z2, w1, f_b1.reshape(1, -1), f_w2, f_b2.reshape(1, -1),
      f_w3.reshape(HY_FILTER_WIDTH, HY_ORDER, 2, d).transpose(2, 1, 0, 3), f_freq.reshape(1, -1),
      deltas.reshape(1, d))

    st, ct = DFT_STEP, DFT_LANES
    a = pl.pallas_call(
        functools.partial(_dft_filter_in_kernel, n1=n1),
        grid=(HY_ORDER, n2 // st),
        in_specs=[
            pl.BlockSpec((1, n1, st, d), lambda o, j: (o, 0, j, 0)),
            _resident((2 * n1, n1)),
        ],
        out_specs=pl.BlockSpec((1, st, 2, n1, d), lambda o, j: (o, j, 0, 0, 0)),
        out_shape=jax.ShapeDtypeStruct((HY_ORDER, n2, 2, n1, d), F32),
        compiler_params=_params("parallel", "parallel"),
        name="hy_filter_dft_in",
    )(k2.reshape(HY_ORDER, n1, n2, d), consts["w_fin"])

    return pl.pallas_call(
        _dft_filter_mid_kernel,
        grid=(HY_ORDER, n1 // st, d // ct),
        in_specs=[
            pl.BlockSpec((1, n2, 2, st, ct), lambda o, k, c: (o, 0, 0, k, c)),
            pl.BlockSpec((st, 2 * n2, 2 * n2), lambda o, k, c: (k, 0, 0)),
            pl.BlockSpec((1, 1, ct), lambda o, k, c: (o, 0, c)),
        ],
        out_specs=pl.BlockSpec((1, 2, st, n2, ct), lambda o, k, c: (o, 0, k, 0, c)),
        out_shape=jax.ShapeDtypeStruct((HY_ORDER, 2, n1, n2, d), F32),
        compiler_params=_params("parallel", "parallel", "parallel"),
        name="hy_filter_dft_mid",
    )(a, consts["w_mid"], norm.reshape(HY_ORDER, 1, d))


def _long_conv_gate(zc, gate, kf, order, skip, consts):
    bsz, seq, d = zc.shape
    n1, half, n2 = consts["n1"], consts["half"], DFT_N2
    pairs = bsz // 2
    st, ct = DFT_STEP, DFT_LANES
    zv = zc.reshape(bsz, half, n2, d)
    a = pl.pallas_call(
        functools.partial(_dft_in_kernel, n1=n1),
        grid=(pairs, n2 // st),
        in_specs=[
            pl.BlockSpec((2, half, st, d), lambda p, j: (p, 0, j, 0)),
            _resident((2 * n1, n1)),
        ],
        out_specs=pl.BlockSpec((1, st, 2, n1, d), lambda p, j: (p, j, 0, 0, 0)),
        out_shape=jax.ShapeDtypeStruct((pairs, n2, 2, n1, d), F32),
        compiler_params=_params("parallel", "parallel"),
        name="hy_dft_in",
    )(zv, consts["w_in"])

    b = pl.pallas_call(
        _dft_mid_kernel,
        grid=(n1 // st, d // ct, pairs),
        in_specs=[
            pl.BlockSpec((1, n2, 2, st, ct), lambda k, c, p: (p, 0, 0, k, c)),
            pl.BlockSpec((1, 2, st, n2, ct), lambda k, c, p: (order, 0, k, 0, c)),
            pl.BlockSpec((st, 2 * n2, 2 * n2), lambda k, c, p: (k, 0, 0)),
            pl.BlockSpec((st, 2 * n2, 2 * n2), lambda k, c, p: (k, 0, 0)),
        ],
        out_specs=pl.BlockSpec((1, st, 2, n2, ct), lambda k, c, p: (p, k, 0, 0, c)),
        out_shape=jax.ShapeDtypeStruct((pairs, n1, 2, n2, d), F32),
        compiler_params=_params("parallel", "parallel", "arbitrary"),
        name="hy_dft_mid",
    )(a, kf, consts["w_mid"], consts["w_mid_c"])

    sig = pl.BlockSpec((2, half, st, ct), lambda p, j, c: (p, 0, j, c))
    out = pl.pallas_call(
        functools.partial(_dft_out_kernel, half=half),
        grid=(pairs, n2 // st, d // ct),
        in_specs=[
            pl.BlockSpec((1, n1, 2, st, ct), lambda p, j, c: (p, 0, 0, j, c)),
            _resident((n1, 2 * n1)),
            sig,
            sig,
            pl.BlockSpec((1, ct), lambda p, j, c: (0, c)),
        ],
        out_specs=sig,
        out_shape=jax.ShapeDtypeStruct((bsz, half, n2, d), F32),
        compiler_params=_params("parallel", "parallel", "parallel"),
        name="hy_dft_out",
    )(b, consts["w_out"], gate.reshape(bsz, half, n2, d), zv, skip.reshape(1, d))
    return out.reshape(bsz, seq, d)


def _hyena_layer(x, mod_l, norm_g, w_in, conv_w, conv_b, kf, skip, consts):
    bsz, seq, d = x.shape
    bm = min(ROW_TILE, seq)
    blk = pl.BlockSpec((1, bm, d), lambda b, i: (b, i, 0))
    shp = jax.ShapeDtypeStruct(x.shape, F32)
    v, x1, x2 = pl.pallas_call(
        functools.partial(_hy_in_kernel, bm=bm, seq=seq),
        grid=(bsz, seq // bm),
        in_specs=_halo_specs(bm, seq, d) + [
            pl.BlockSpec((1, 6, d), lambda b, i: (b, 0, 0)),
            _resident((1, d)),
            _resident((d, 3 * d)),
            _resident((3, 3 * d)),
            _resident((1, 3 * d)),
        ],
        out_specs=[blk, blk, blk],
        out_shape=[shp, shp, shp],
        scratch_shapes=[
            pltpu.VMEM((bm + 2 * HALO, d), BF16),
            pltpu.VMEM((bm + 2 * HALO, HY_CHUNK), F32),
        ],
        compiler_params=_params("parallel", "parallel"),
        name="hy_in",
    )(x, x, x, mod_l, norm_g.reshape(1, d), w_in.astype(BF16), conv_w, conv_b.reshape(1, 3 * d))
    zc = v
    for o, gate in enumerate((x1, x2)):
        zc = _long_conv_gate(zc, gate, kf, o, skip[o], consts)
    return zc


def _s5_prep_kernel(x_ref, mod_ref, g_ref, p_ref, o_ref):
    hs = []
    for b in range(x_ref.shape[0]):
        m = mod_ref[b]
        hs.append(_norm_mod(x_ref[b], g_ref[...], m[1:2], m[0:1]).astype(BF16))
    o_ref[...] = jnp.dot(p_ref[...], jnp.concatenate(hs, axis=0),
                         preferred_element_type=F32).astype(o_ref.dtype)


def _s5_scan_kernel(h_ref, bm_ref, cm_ref, lam_ref, pt_ref, o_ref, sa_ref, sb_ref, sc_ref, y_ref, carry_ref,
                    *, rows, reverse):
    n_slab = D_MODEL // S5_SLAB
    width = S5_SLAB // S5_GROUP * S5_STATE
    n_tiles = rows // SUBLANES
    half = SUBLANES // 2
    bufs = (sa_ref, sb_ref, sc_ref)

    @pl.when(pl.program_id(0) == 0)
    def _():
        carry_ref[...] = jnp.zeros(carry_ref.shape, F32)

    low = lax.broadcasted_iota(jnp.int32, (SUBLANES, width), 0) < half
    take_rolled = jnp.logical_not(low) if reverse else low
    low3 = lax.broadcasted_iota(jnp.int32, (n_tiles, SUBLANES, S5_SLAB), 1) < half
    gets_other = low3 if reverse else jnp.logical_not(low3)

    def project_in(k):
        u = h_ref[:, k * S5_SLAB:(k + 1) * S5_SLAB]
        u3 = u.astype(F32).reshape(n_tiles, SUBLANES, S5_SLAB)
        other = jnp.where(gets_other, pltpu.roll(u3, half, axis=1), 0.0).reshape(rows, S5_SLAB)
        lhs = jnp.concatenate([u, other.astype(BF16)], axis=1)
        bufs[k % 3][...] = jnp.dot(lhs, bm_ref[k], preferred_element_type=F32)

    def project_out(k):
        y_ref[:, k * S5_SLAB:(k + 1) * S5_SLAB] = jnp.dot(
            bufs[k % 3][...].astype(BF16), cm_ref[k], preferred_element_type=F32).astype(y_ref.dtype)

    def scan(k):
        s_ref = bufs[k % 3]
        lr = lam_ref[k, 0]
        li = lam_ref[k, 1]
        cr = carry_ref[k, 0]
        ci = carry_ref[k, 1]
        for i in (reversed(range(n_tiles)) if reverse else range(n_tiles)):
            r0 = i * SUBLANES
            vr = s_ref[r0:r0 + SUBLANES, 0:width]
            vi = s_ref[r0:r0 + SUBLANES, width:2 * width]
            pr = jnp.where(take_rolled, pltpu.roll(cr, half, axis=0), cr)
            pi = jnp.where(take_rolled, pltpu.roll(ci, half, axis=0), ci)
            cr = vr + (lr * pr - li * pi)
            ci = vi + (lr * pi + li * pr)
            s_ref[r0:r0 + SUBLANES, 0:width] = cr
            s_ref[r0:r0 + SUBLANES, width:2 * width] = ci
        carry_ref[k, 0] = cr
        carry_ref[k, 1] = ci

    project_in(0)
    for k in range(n_slab):
        if k + 1 < n_slab:
            project_in(k + 1)
        if k >= 1:
            project_out(k - 1)
        scan(k)
    project_out(n_slab - 1)
    y = jnp.dot(pt_ref[...], y_ref[...], preferred_element_type=F32)
    steps = rows // o_ref.shape[0]
    for b in range(o_ref.shape[0]):
        o_ref[b] = y[b * steps:(b + 1) * steps].astype(o_ref.dtype)


def _s5_glu_kernel(yf_ref, yb_ref, x_ref, mod_ref, g_ref, dsk_ref, w_ref, o_ref):
    m = mod_ref[0]
    x = x_ref[0]
    h = _norm_mod(x, g_ref[...], m[1:2], m[0:1])
    y = jax.nn.gelu(yf_ref[0].astype(F32) + yb_ref[0].astype(F32) + dsk_ref[...] * h)
    g = jnp.dot(y.astype(BF16), w_ref[...], preferred_element_type=F32)
    d = x.shape[-1]
    o_ref[0] = x + m[2:3] * (g[:, :d] * jax.nn.sigmoid(g[:, d:]))


def _s5_tables(a_re, a_im, log_dt, b_re, b_im, c_re, c_im):
    n_slab = D_MODEL // S5_SLAB
    gps = S5_SLAB // S5_GROUP
    lam = lax.complex(jnp.minimum(a_re.astype(F32), -1e-4), a_im.astype(F32))
    dt = jnp.exp(log_dt.astype(F32))[..., None]
    lam_bar = jnp.exp(lam * dt)
    b_bar = ((lam_bar - 1.0) / lam)[..., None] * lax.complex(b_re.astype(F32), b_im.astype(F32))
    eye = jnp.eye(gps, dtype=F32)

    def b_block(part):
        m = jnp.einsum("dkgph,gj->dkghjp", part, eye)
        return m.reshape(2, n_slab, S5_SLAB, gps * S5_STATE)

    def b_rows(b_c):
        bb = b_c.reshape(2, n_slab, gps, S5_STATE, S5_GROUP)
        return jnp.concatenate([b_block(bb.real), b_block(bb.imag)], axis=-1)

    b_mat = jnp.concatenate([b_rows(b_bar), b_rows(b_bar * lam_bar[..., None])], axis=-2)
    cc = lax.complex(c_re.astype(F32), c_im.astype(F32)).reshape(2, n_slab, gps, S5_GROUP, S5_STATE)

    def c_block(part):
        m = jnp.einsum("dkghp,gj->dkgpjh", part, eye)
        return m.reshape(2, n_slab, gps * S5_STATE, S5_SLAB)

    c_mat = jnp.concatenate([c_block(cc.real), -c_block(cc.imag)], axis=-2)
    lam1 = lam_bar.reshape(2, n_slab, 1, gps * S5_STATE)
    lam2 = lam1 * lam1
    half = SUBLANES // 2

    def rows(lo, hi):
        return jnp.concatenate([jnp.broadcast_to(lo, lo.shape[:2] + (half,) + lo.shape[3:]),
                                jnp.broadcast_to(hi, hi.shape[:2] + (half,) + hi.shape[3:])], axis=2)

    carry = jnp.stack([rows(lam1, lam2)[0], rows(lam2, lam1)[1]])
    lam_t = jnp.stack([carry.real, carry.imag], axis=2)
    return b_mat, c_mat, lam_t


def _s5_layer(x, mod_l, norm_g, a_re, a_im, log_dt, b_re, b_im, c_re, c_im, d_skip, w_glu):
    bsz, seq, d = x.shape
    assert 2 * bsz == SUBLANES, bsz
    bm = min(ROW_TILE, seq)
    n_slab = d // S5_SLAB
    width = S5_SLAB // S5_GROUP * S5_STATE
    b_mat, c_mat, lam_t = _s5_tables(a_re, a_im, log_dt, b_re, b_im, c_re, c_im)
    steps = min(S5_STEPS, seq)
    rows = steps * bsz
    nc = seq // steps
    b_mat = b_mat.astype(BF16)
    c_mat = c_mat.astype(BF16)
    r = np.arange(rows)
    perm_np = np.zeros((rows, rows), np.float32)
    perm_np[r, (r % bsz) * steps + r // bsz] = 1.0
    perm = jnp.asarray(perm_np, dtype=BF16)

    h2 = pl.pallas_call(
        _s5_prep_kernel,
        grid=(nc,),
        in_specs=[
            pl.BlockSpec((bsz, steps, d), lambda i: (0, i, 0)),
            _resident((bsz, 6, d)),
            _resident((1, d)),
            _resident((rows, rows)),
        ],
        out_specs=pl.BlockSpec((rows, d), lambda i: (i, 0)),
        out_shape=jax.ShapeDtypeStruct((seq * bsz, d), BF16),
        compiler_params=_params("parallel"),
        name="s5_prep",
    )(x, mod_l, norm_g.reshape(1, d), perm)

    def scan(direction):
        reverse = direction == 1
        chunk = (lambda c: nc - 1 - c) if reverse else (lambda c: c)
        return pl.pallas_call(
            functools.partial(_s5_scan_kernel, rows=rows, reverse=reverse),
            grid=(nc,),
            in_specs=[
                pl.BlockSpec((rows, d), lambda c: (chunk(c), 0)),
                _resident((n_slab, 2 * S5_SLAB, 2 * width)),
                _resident((n_slab, 2 * width, S5_SLAB)),
                _resident((n_slab, 2, SUBLANES, width)),
                _resident((rows, rows)),
            ],
            out_specs=pl.BlockSpec((bsz, steps, d), lambda c: (0, chunk(c), 0)),
            out_shape=jax.ShapeDtypeStruct((bsz, seq, d), BF16),
            scratch_shapes=[
                pltpu.VMEM((rows, 2 * width), F32),
                pltpu.VMEM((rows, 2 * width), F32),
                pltpu.VMEM((rows, 2 * width), F32),
                pltpu.VMEM((rows, d), BF16),
                pltpu.VMEM((n_slab, 2, SUBLANES, width), F32),
            ],
            compiler_params=_params("arbitrary"),
            name="s5_scan_bwd" if reverse else "s5_scan_fwd",
        )(h2, b_mat[direction], c_mat[direction], lam_t[direction], perm.T)

    y_f = scan(0)
    y_b = scan(1)
    return pl.pallas_call(
        _s5_glu_kernel,
        grid=(bsz, seq // bm),
        in_specs=[
            pl.BlockSpec((1, bm, d), lambda b, i: (b, i, 0)),
            pl.BlockSpec((1, bm, d), lambda b, i: (b, i, 0)),
            pl.BlockSpec((1, bm, d), lambda b, i: (b, i, 0)),
            pl.BlockSpec((1, 6, d), lambda b, i: (b, 0, 0)),
            _resident((1, d)),
            _resident((1, d)),
            _resident((d, 2 * d)),
        ],
        out_specs=pl.BlockSpec((1, bm, d), lambda b, i: (b, i, 0)),
        out_shape=jax.ShapeDtypeStruct(x.shape, F32),
        compiler_params=_params("parallel", "parallel"),
        name="s5_glu",
    )(y_f, y_b, x, mod_l, norm_g.reshape(1, d), d_skip.reshape(1, d), w_glu.astype(BF16))


def kernel(x, c, ada_w, ada_b, norm1_g, norm2_g, final_g, attn_w_qkv, attn_w_o, attn_q_gain, attn_k_gain, hy_w_in, hy_conv_w, hy_conv_b, hy_f_w1, hy_f_b1, hy_f_w2, hy_f_b2, hy_f_w3, hy_f_freq, hy_skip, hy_w_out, s5_A_re, s5_A_im, s5_log_dt, s5_B_re, s5_B_im, s5_C_re, s5_C_im, s5_D, s5_w_glu, ffn_w_up, ffn_conv_w, ffn_conv_b, ffn_w_down):
    seq = x.shape[1]
    mod = _ada_mod(c, ada_w, ada_b)
    cos, sin = _rope_tables(seq)
    consts = _dft_constants(seq)
    for i in range(DEPTH):
        m, j = i % N_MIXERS, i // N_MIXERS
        mixer_out = None
        if m == 0:
            o = _attention_layer(x, mod[i], norm1_g[i], attn_w_qkv[j], attn_q_gain[j], attn_k_gain[j], cos, sin)
            mixer_out = (o, attn_w_o[j])
        elif m == 1:
            kf = _hyena_filter_spectrum(seq, consts, hy_f_w1[j], hy_f_b1[j], hy_f_w2[j], hy_f_b2[j],
                                        hy_f_w3[j], hy_f_freq[j])
            zc = _hyena_layer(x, mod[i], norm1_g[i], hy_w_in[j], hy_conv_w[j], hy_conv_b[j], kf,
                              hy_skip[j], consts)
            mixer_out = (zc, hy_w_out[j])
        else:
            x = _s5_layer(x, mod[i], norm1_g[i], s5_A_re[j], s5_A_im[j], s5_log_dt[j], s5_B_re[j],
                          s5_B_im[j], s5_C_re[j], s5_C_im[j], s5_D[j], s5_w_glu[j])
        x = _conv_ffn_layer(x, mod[i], norm2_g[i], ffn_w_up[i], ffn_conv_w[i], ffn_conv_b[i],
                            ffn_w_down[i], final_g, final_norm=(i == DEPTH - 1), mixer_out=mixer_out)
    return x
```

```python
import functools
import math

import numpy as np
import jax
import jax.numpy as jnp
from jax import lax
from jax.experimental import pallas as pl
from jax.experimental.pallas import tpu as pltpu

D_MODEL = 1024
DEPTH = 4
N_MIXERS = 3
GRID_W = 64
EPS = 1e-6
N_HEADS = 16
HEAD_DIM = D_MODEL // N_HEADS
N_KV_HEADS = 4
KV_GROUP = N_HEADS // N_KV_HEADS
ROPE_THETA = 10000.0
HY_ORDER = 2
HY_EMB_DIM = 33
HY_FILTER_WIDTH = 64
HY_FAST_DECAY = 0.3
HY_SLOW_DECAY = 1.5
HY_TARGET = 1e-2
S5_GROUP = 16
S5_STATE = 64
D_FF = 2816

F32 = jnp.float32
BF16 = jnp.bfloat16
HIGHEST = lax.Precision.HIGHEST

LANES = 128
SUBLANES = 8
BF16_ROWS = 16
VMEM_LIMIT = 56 * 1024 * 1024

ADA_TILE = 1536
ROW_TILE = 512
HALO = BF16_ROWS
HY_CHUNK = 512
Q_TILE = 512
KV_TILE = 1024
MAX_FIXED_SHIFT = 40.0
DFT_N2 = 128
DFT_STEP = SUBLANES
DFT_LANES = 512
S5_SLAB = 128
S5_STEPS = 128


def _params(*sem):
    return pltpu.CompilerParams(dimension_semantics=sem, vmem_limit_bytes=VMEM_LIMIT)


def _resident(shape):
    zeros = (0,) * len(shape)
    return pl.BlockSpec(shape, lambda *_: zeros, pipeline_mode=pl.Buffered(1))


def _norm_mod(x, g, scale, shift):
    y = x * lax.rsqrt(jnp.mean(x * x, axis=-1, keepdims=True) + EPS)
    return (y * g) * (1.0 + scale) + shift


def _ada_kernel(c_ref, w_ref, b_ref, o_ref):
    c = c_ref[...]
    a = c * jax.nn.sigmoid(c)
    o_ref[0] = jnp.dot(a, w_ref[0], precision=HIGHEST, preferred_element_type=F32) + b_ref[0]


def _ada_mod(c, ada_w, ada_b):
    bsz, d = c.shape
    rows = -(-bsz // SUBLANES) * SUBLANES
    cp = jnp.pad(c, ((0, rows - bsz), (0, 0)))
    tn = ADA_TILE
    out = pl.pallas_call(
        _ada_kernel,
        grid=(DEPTH, 6 * d // tn),
        in_specs=[
            pl.BlockSpec((rows, d), lambda i, j: (0, 0)),
            pl.BlockSpec((1, d, tn), lambda i, j: (i, 0, j)),
            pl.BlockSpec((1, 1, tn), lambda i, j: (i, 0, j)),
        ],
        out_specs=pl.BlockSpec((1, rows, tn), lambda i, j: (i, 0, j)),
        out_shape=jax.ShapeDtypeStruct((DEPTH, rows, 6 * d), F32),
        compiler_params=_params("arbitrary", "arbitrary"),
        name="ada_mod",
    )(cp, ada_w, ada_b.reshape(DEPTH, 1, 6 * d))
    return out[:, :bsz].reshape(DEPTH, bsz, 6, d)


def _qkv_kernel(x_ref, mod_ref, g_ref, w_ref, seg_ref, c_ref, s1_ref, s2_ref, qg_ref, kg_ref,
                q_ref, klo_ref, khi_ref, v_ref):
    m = mod_ref[0]
    h = _norm_mod(x_ref[0], g_ref[...], m[1:2], m[0:1])
    r = jnp.dot(h.astype(BF16), w_ref[...], preferred_element_type=F32)
    c = c_ref[...]
    s1 = s1_ref[...]
    s2 = s2_ref[...]
    seg = seg_ref[...]
    quarter = HEAD_DIM // 2

    def norm_rope(xr, gain):
        ss = xr * xr
        hi = ss.astype(BF16)
        lo = (ss - hi.astype(F32)).astype(BF16)
        tot = (jnp.dot(hi, seg, preferred_element_type=F32) + jnp.dot(lo, seg, preferred_element_type=F32))
        n = xr * lax.rsqrt(tot * (1.0 / HEAD_DIM) + EPS) * gain
        return (n * c + pltpu.roll(n, LANES - quarter, axis=1) * s1 + pltpu.roll(n, quarter, axis=1) * s2)

    qg = qg_ref[...]
    kg = kg_ref[...]
    n_q = N_HEADS * HEAD_DIM // LANES
    for p in range(n_q):
        q_ref[0, p] = norm_rope(r[:, p * LANES:(p + 1) * LANES], qg).astype(BF16)
    low = lax.broadcasted_iota(jnp.int32, (r.shape[0], LANES), 1) < HEAD_DIM
    for g in range(N_KV_HEADS):
        kk = norm_rope(r[:, (n_q + g) * LANES:(n_q + g + 1) * LANES], kg)
        klo_ref[0, g] = jnp.where(low, kk, 0.0).astype(BF16)
        khi_ref[0, g] = jnp.where(low, 0.0, kk).astype(BF16)
        vv = r[:, (n_q + N_KV_HEADS + g) * LANES:(n_q + N_KV_HEADS + g + 1) * LANES]
        v_ref[0, g] = jnp.where(low, vv, 1.0).astype(BF16)


def _flash_kernel(q_ref, qn_ref, klo_ref, khi_ref, v_ref, shift_ref, o_ref, sa_ref, sb_ref, m_ref, acc_ref,
                  *, n_kv, fixed_shift):
    rows = KV_GROUP * Q_TILE
    m_ref[...] = jnp.full(m_ref.shape, -jnp.inf, F32)
    acc_ref[...] = jnp.zeros(acc_ref.shape, F32)

    def scores(s_ref, src_ref, j):
        start = pl.multiple_of(j * KV_TILE, KV_TILE)
        qq = src_ref[0].reshape(rows // 2, LANES)
        for half, kk_ref in enumerate((klo_ref, khi_ref)):
            kj = kk_ref[0, 0, pl.ds(start, KV_TILE), :]
            s_ref[half * (rows // 2):(half + 1) * (rows // 2)] = lax.dot_general(
                qq, kj, (((1,), (1,)), ((), ())), preferred_element_type=F32)

    def consume(s_ref, j):
        s = s_ref[...]
        start = pl.multiple_of(j * KV_TILE, KV_TILE)
        vj = v_ref[0, 0, pl.ds(start, KV_TILE), :]
        if fixed_shift:
            shift = shift_ref[...]
            p = jnp.concatenate([jnp.exp(s[:, c * LANES:(c + 1) * LANES] - shift)
                                 for c in range(KV_TILE // LANES)], axis=1)
            acc_ref[...] += jnp.dot(p.astype(BF16), vj, preferred_element_type=F32)
            return
        m_prev = m_ref[...]
        m_next = jnp.maximum(m_prev, jnp.max(s, axis=-1, keepdims=True))
        alpha = jnp.exp(m_prev - m_next)
        p = jnp.concatenate([jnp.exp(s[:, c * LANES:(c + 1) * LANES] - m_next)
                             for c in range(KV_TILE // LANES)], axis=1)
        acc_ref[...] = alpha * acc_ref[...] + jnp.dot(p.astype(BF16), vj, preferred_element_type=F32)
        m_ref[...] = m_next

    @pl.when(pl.program_id(2) == 0)
    def _():
        scores(sa_ref, q_ref, 0)

    def body(jj, carry):
        scores(sb_ref, q_ref, 2 * jj + 1)
        consume(sa_ref, 2 * jj)
        scores(sa_ref, q_ref, 2 * jj + 2)
        consume(sb_ref, 2 * jj + 1)
        return carry

    lax.fori_loop(0, n_kv // 2 - 1, body, 0)
    scores(sb_ref, q_ref, n_kv - 1)
    consume(sa_ref, n_kv - 2)
    scores(sa_ref, qn_ref, 0)
    consume(sb_ref, n_kv - 1)
    acc = acc_ref[...]
    o = acc[:, :HEAD_DIM] / acc[:, HEAD_DIM:HEAD_DIM + 1]
    o_ref[0] = jnp.concatenate(
        [o[g * Q_TILE:(g + 1) * Q_TILE] for g in ((j % 2) * (KV_GROUP // 2) + j // 2 for j in range(KV_GROUP))],
        axis=-1).astype(o_ref.dtype)


def _rope_tables(seq):
    rows = seq // GRID_W
    n_freq = HEAD_DIM // 4
    inv = 1.0 / (ROPE_THETA ** (jnp.arange(n_freq, dtype=F32) / n_freq))
    r = jnp.arange(rows, dtype=F32)
    col = jnp.arange(GRID_W, dtype=F32)
    ang_r = jnp.broadcast_to(r[:, None, None] * inv, (rows, GRID_W, n_freq))
    ang_c = jnp.broadcast_to(col[None, :, None] * inv, (rows, GRID_W, n_freq))
    ang = jnp.concatenate([ang_r, ang_c], axis=-1).reshape(seq, 2 * n_freq)
    return jnp.cos(ang), jnp.sin(ang)


def _attention_layer(x, mod_l, norm_g, w_qkv, q_gain, k_gain, cos, sin):
    bsz, seq, d = x.shape
    bm = min(ROW_TILE, seq)
    perm = np.concatenate([np.arange(0, HEAD_DIM, 2), np.arange(1, HEAD_DIM, 2)])
    head_cols = lambda hh: hh * HEAD_DIM + perm
    k0 = N_HEADS * HEAD_DIM
    v0 = k0 + N_KV_HEADS * HEAD_DIM
    q_cols = np.concatenate([head_cols(hh) for hh in range(N_HEADS)])
    k_cols = np.concatenate([np.tile(k0 + head_cols(g), 2) for g in range(N_KV_HEADS)])
    w_v = jnp.pad(w_qkv[:, v0:].reshape(d, N_KV_HEADS, HEAD_DIM), ((0, 0), (0, 0), (0, LANES - HEAD_DIM)))
    w = jnp.concatenate([w_qkv[:, q_cols], w_qkv[:, k_cols], w_v.reshape(d, N_KV_HEADS * LANES)],
                        axis=1).astype(BF16)
    n_out = w.shape[1]
    reps = LANES // (HEAD_DIM // 2)
    zero = jnp.zeros_like(sin)
    c_t = jnp.tile(cos, (1, reps))
    s1_t = jnp.tile(jnp.concatenate([-sin, zero], axis=1), (1, reps // 2))
    s2_t = jnp.tile(jnp.concatenate([zero, sin], axis=1), (1, reps // 2))
    seg = jnp.asarray(np.kron(np.eye(LANES // HEAD_DIM), np.ones((HEAD_DIM, HEAD_DIM))), dtype=BF16)
    tab = pl.BlockSpec((bm, LANES), lambda b, i: (i, 0))
    n_pairs = N_HEADS * HEAD_DIM // LANES
    kv_spec = pl.BlockSpec((1, N_KV_HEADS, bm, LANES), lambda b, i: (b, 0, i, 0))
    kv_shape = jax.ShapeDtypeStruct((bsz, N_KV_HEADS, seq, LANES), BF16)
    q, k_lo, k_hi, v = pl.pallas_call(
        _qkv_kernel,
        grid=(bsz, seq // bm),
        in_specs=[
            pl.BlockSpec((1, bm, d), lambda b, i: (b, i, 0)),
            pl.BlockSpec((1, 6, d), lambda b, i: (b, 0, 0)),
            _resident((1, d)),
            _resident((d, n_out)),
            _resident((LANES, LANES)),
            tab,
            tab,
            tab,
            _resident((1, LANES)),
            _resident((1, LANES)),
        ],
        out_specs=[
            pl.BlockSpec((1, n_pairs, bm, LANES), lambda b, i: (b, 0, i, 0)),
            kv_spec,
            kv_spec,
            kv_spec,
        ],
        out_shape=[
            jax.ShapeDtypeStruct((bsz, n_pairs, seq, LANES), BF16),
            kv_shape,
            kv_shape,
            kv_shape,
        ],
        compiler_params=_params("parallel", "parallel"),
        name="attn_qkv",
    )(x, mod_l, norm_g.reshape(1, d), w, seg, c_t, s1_t, s2_t,
      jnp.tile(q_gain[perm] * HEAD_DIM ** -0.5, 2).reshape(1, LANES), jnp.tile(k_gain[perm], 2).reshape(1, LANES))

    score_bound = HEAD_DIM ** 0.5 * jnp.max(jnp.abs(q_gain)) * jnp.max(jnp.abs(k_gain))
    return _flash_attention(q, k_lo, k_hi, v, score_bound)


def _flash_attention(q, k_lo, k_hi, v, score_bound):
    bsz, _, seq, _ = q.shape
    assert (seq // KV_TILE) % 2 == 0, seq
    rows = KV_GROUP * Q_TILE
    n_q = seq // Q_TILE
    kv_spec = pl.BlockSpec((1, 1, seq, LANES), lambda b, g, i: (b, g, 0, 0))
    shift = jnp.full((1, LANES), score_bound, F32)

    def attend(fixed_shift):
        return pl.pallas_call(
            functools.partial(_flash_kernel, n_kv=seq // KV_TILE, fixed_shift=fixed_shift),
            grid=(bsz, N_KV_HEADS, n_q),
            in_specs=[
                pl.BlockSpec((1, KV_GROUP // 2, Q_TILE, LANES), lambda b, g, i: (b, g, i, 0)),
                pl.BlockSpec((1, KV_GROUP // 2, Q_TILE, LANES),
                             lambda b, g, i: (b, g, jnp.minimum(i + 1, n_q - 1), 0)),
                kv_spec,
                kv_spec,
                kv_spec,
                _resident((1, LANES)),
            ],
            out_specs=pl.BlockSpec((1, Q_TILE, KV_GROUP * HEAD_DIM), lambda b, g, i: (b, i, g)),
            out_shape=jax.ShapeDtypeStruct((bsz, seq, N_HEADS * HEAD_DIM), BF16),
            scratch_shapes=[
                pltpu.VMEM((rows, KV_TILE), F32),
                pltpu.VMEM((rows, KV_TILE), F32),
                pltpu.VMEM((rows, LANES), F32),
                pltpu.VMEM((rows, 2 * HEAD_DIM), F32),
            ],
            compiler_params=_params("parallel", "parallel", "arbitrary"),
            name="attn_flash_bounded" if fixed_shift else "attn_flash",
        )(q, q, k_lo, k_hi, v, shift)

    return lax.cond(score_bound < MAX_FIXED_SHIFT, lambda: attend(True), lambda: attend(False))


def _halo_rows(i, bm, seq):
    t = i * bm - HALO + lax.broadcasted_iota(jnp.int32, (bm + 2 * HALO, 1), 0)
    return (t >= 0) & (t < seq)


def _fill_window(hs_ref, xp_ref, x_ref, xn_ref, g, scale, shift, bm):
    hs_ref[0:HALO] = _norm_mod(xp_ref[0], g, scale, shift).astype(BF16)
    hs_ref[HALO:HALO + bm] = _norm_mod(x_ref[0], g, scale, shift).astype(BF16)
    hs_ref[HALO + bm:] = _norm_mod(xn_ref[0], g, scale, shift).astype(BF16)


def _conv3(win_ref, cw, cb, bm):
    return (win_ref[pl.ds(HALO - 1, bm), :] * cw[0:1] + win_ref[pl.ds(HALO, bm), :] * cw[1:2]
            + win_ref[pl.ds(HALO + 1, bm), :] * cw[2:3] + cb)


def _ffn_tail(x_main, m, hs_ref, gs_ref, wu_ref, cw_ref, cb_ref, wd_ref, fg_ref, o_ref, *, bm, seq, final_norm):
    valid = _halo_rows(pl.program_id(1), bm, seq)
    gate = jnp.dot(hs_ref[...], wu_ref[:, :D_FF], preferred_element_type=F32)
    gs_ref[...] = jnp.where(valid, gate, 0.0)
    gate = _conv3(gs_ref, cw_ref[...], cb_ref[...], bm)
    val = jnp.dot(hs_ref[HALO:HALO + bm], wu_ref[:, D_FF:], preferred_element_type=F32)
    a = (gate * jax.nn.sigmoid(gate)) * val
    y = x_main + m[5:6] * jnp.dot(a.astype(BF16), wd_ref[...], preferred_element_type=F32)
    if final_norm:
        y = y * lax.rsqrt(jnp.mean(y * y, axis=-1, keepdims=True) + EPS) * fg_ref[...]
    o_ref[0] = y


def _ffn_kernel(xp_ref, x_ref, xn_ref, mod_ref, g_ref, wu_ref, cw_ref, cb_ref, wd_ref, fg_ref,
                o_ref, hs_ref, gs_ref, *, bm, seq, final_norm):
    m = mod_ref[0]
    _fill_window(hs_ref, xp_ref, x_ref, xn_ref, g_ref[...], m[4:5], m[3:4], bm)
    _ffn_tail(x_ref[0], m, hs_ref, gs_ref, wu_ref, cw_ref, cb_ref, wd_ref, fg_ref, o_ref,
              bm=bm, seq=seq, final_norm=final_norm)


def _proj_ffn_kernel(ap_ref, a_ref, an_ref, wp_ref, xp_ref, x_ref, xn_ref, mod_ref, g_ref, wu_ref, cw_ref,
                     cb_ref, wd_ref, fg_ref, o_ref, hs_ref, gs_ref, aw_ref, xw_ref, *, bm, seq, final_norm):
    m = mod_ref[0]
    aw_ref[0:HALO] = ap_ref[0].astype(BF16)
    aw_ref[HALO:HALO + bm] = a_ref[0].astype(BF16)
    aw_ref[HALO + bm:] = an_ref[0].astype(BF16)
    xw_ref[0:HALO] = xp_ref[0]
    xw_ref[HALO:HALO + bm] = x_ref[0]
    xw_ref[HALO + bm:] = xn_ref[0]
    xw_ref[...] = xw_ref[...] + m[2:3] * jnp.dot(aw_ref[...], wp_ref[...], preferred_element_type=F32)
    hs_ref[...] = _norm_mod(xw_ref[...], g_ref[...], m[4:5], m[3:4]).astype(BF16)
    _ffn_tail(xw_ref[HALO:HALO + bm], m, hs_ref, gs_ref, wu_ref, cw_ref, cb_ref, wd_ref, fg_ref, o_ref,
              bm=bm, seq=seq, final_norm=final_norm)


def _halo_specs(bm, seq, d):
    nh = seq // HALO
    per = bm // HALO
    return [
        pl.BlockSpec((1, HALO, d), lambda b, i: (b, jnp.maximum(i * per - 1, 0), 0)),
        pl.BlockSpec((1, bm, d), lambda b, i: (b, i, 0)),
        pl.BlockSpec((1, HALO, d), lambda b, i: (b, jnp.minimum((i + 1) * per, nh - 1), 0)),
    ]


def _conv_ffn_layer(x, mod_l, norm_g, w_up, conv_w, conv_b, w_down, final_g, final_norm, mixer_out=None):
    bsz, seq, d = x.shape
    bm = min(ROW_TILE, seq)
    win = bm + 2 * HALO
    ffn_specs = _halo_specs(bm, seq, d) + [
        pl.BlockSpec((1, 6, d), lambda b, i: (b, 0, 0)),
        _resident((1, d)),
        _resident((d, 2 * D_FF)),
        _resident((3, D_FF)),
        _resident((1, D_FF)),
        _resident((D_FF, d)),
        _resident((1, d)),
    ]
    ffn_args = (x, x, x, mod_l, norm_g.reshape(1, d), w_up.astype(BF16), conv_w, conv_b.reshape(1, D_FF),
                w_down.astype(BF16), final_g.reshape(1, d))
    scratch = [pltpu.VMEM((win, d), BF16), pltpu.VMEM((win, D_FF), F32)]
    if mixer_out is None:
        body, specs, args = _ffn_kernel, ffn_specs, ffn_args
    else:
        a, w = mixer_out
        kdim = a.shape[-1]
        body = _proj_ffn_kernel
        specs = _halo_specs(bm, seq, kdim) + [_resident((kdim, d))] + ffn_specs
        args = (a, a, a, w.astype(BF16)) + ffn_args
        scratch = scratch + [pltpu.VMEM((win, kdim), BF16), pltpu.VMEM((win, d), F32)]
    return pl.pallas_call(
        functools.partial(body, bm=bm, seq=seq, final_norm=final_norm),
        grid=(bsz, seq // bm),
        in_specs=specs,
        out_specs=pl.BlockSpec((1, bm, d), lambda b, i: (b, i, 0)),
        out_shape=jax.ShapeDtypeStruct(x.shape, F32),
        scratch_shapes=scratch,
        compiler_params=_params("parallel", "parallel"),
        name="conv_ffn",
    )(*args)


def _hy_in_kernel(xp_ref, x_ref, xn_ref, mod_ref, g_ref, w_ref, cw_ref, cb_ref,
                  v_ref, x1_ref, x2_ref, hs_ref, us_ref, *, bm, seq):
    m = mod_ref[0]
    _fill_window(hs_ref, xp_ref, x_ref, xn_ref, g_ref[...], m[1:2], m[0:1], bm)
    i = pl.program_id(1)
    keep_prev = (i > 0).astype(F32)
    keep_next = (i < seq // bm - 1).astype(F32)
    outs = (v_ref, x1_ref, x2_ref)
    for j in range(3 * D_MODEL // HY_CHUNK):
        c0 = j * HY_CHUNK
        us_ref[...] = jnp.dot(hs_ref[...], w_ref[:, c0:c0 + HY_CHUNK], preferred_element_type=F32)
        us_ref[HALO - SUBLANES:HALO] = us_ref[HALO - SUBLANES:HALO] * keep_prev
        us_ref[HALO + bm:HALO + bm + SUBLANES] = us_ref[HALO + bm:HALO + bm + SUBLANES] * keep_next
        y = _conv3(us_ref, cw_ref[:, c0:c0 + HY_CHUNK], cb_ref[:, c0:c0 + HY_CHUNK], bm)
        off = c0 % D_MODEL
        outs[c0 // D_MODEL][0, :, off:off + HY_CHUNK] = y


def _hy_filter_kernel(z_ref, w1_ref, b1_ref, w2_ref, b2_ref, w3_ref, fr_ref, dl_ref, k_ref, n_ref,
                      *, bm, seq):
    i = pl.program_id(0)
    z = z_ref[...]
    fr = fr_ref[...]
    a = jnp.sin(fr * (jnp.dot(z, w1_ref[...], precision=HIGHEST, preferred_element_type=F32) + b1_ref[...]))
    a = jnp.sin(fr * (jnp.dot(a, w2_ref[...], precision=HIGHEST, preferred_element_type=F32) + b2_ref[...]))
    decay = jnp.exp(-z[:, 0:1] * dl_ref[...])
    t = i * bm + lax.broadcasted_iota(jnp.int32, (bm, 1), 0)
    decay = jnp.where(t == seq, 0.0, decay)

    @pl.when(i == 0)
    def _():
        n_ref[...] = jnp.zeros(n_ref.shape, F32)

    a_hi = a.astype(BF16)
    a_lo = (a - a_hi.astype(F32)).astype(BF16)
    for o in range(HY_ORDER):
        w = w3_ref[0, o]
        w_hi = w.astype(BF16)
        w_lo = (w - w_hi.astype(F32)).astype(BF16)
        f = (jnp.dot(a_hi, w_hi, preferred_element_type=F32) + jnp.dot(a_hi, w_lo, preferred_element_type=F32)
             + jnp.dot(a_lo, w_hi, preferred_element_type=F32)) * decay
        k_ref[o] = f
        n_ref[o:o + 1] += jnp.sum(jnp.abs(f), axis=0, keepdims=True)


def _dft_dot(w_ref, x):
    return jnp.dot(w_ref[...], x.astype(BF16), preferred_element_type=F32)


def _dft_in_kernel(x_ref, w_ref, o_ref, *, n1):
    for s in range(DFT_STEP):
        r = _dft_dot(w_ref, jnp.concatenate([x_ref[0, :, s, :], x_ref[1, :, s, :]], axis=0))
        o_ref[0, s, 0] = r[:n1]
        o_ref[0, s, 1] = r[n1:]


def _dft_filter_in_kernel(x_ref, w_ref, o_ref, *, n1):
    for s in range(DFT_STEP):
        r = _dft_dot(w_ref, jnp.concatenate([x_ref[0, :n1 // 2, s, :], x_ref[0, n1 // 2:, s, :]], axis=0))
        o_ref[0, s, 0] = r[:n1]
        o_ref[0, s, 1] = r[n1:]


def _twiddle(ar, ai, twr, twi):
    return ar * twr - ai * twi, ar * twi + ai * twr


def _dft_filter_mid_kernel(a_ref, w_ref, n_ref, o_ref):
    inv = 1.0 / n_ref[0]
    for j in range(DFT_STEP):
        z = _dft_dot(w_ref.at[j], jnp.concatenate([a_ref[0, :, 0, j, :], a_ref[0, :, 1, j, :]], axis=0))
        o_ref[0, 0, j] = z[:DFT_N2] * inv
        o_ref[0, 1, j] = z[DFT_N2:] * inv


def _dft_mid_kernel(a_ref, kf_ref, w_ref, wc_ref, o_ref):
    for j in range(DFT_STEP):
        z = _dft_dot(w_ref.at[j], jnp.concatenate([a_ref[0, :, 0, j, :], a_ref[0, :, 1, j, :]], axis=0))
        yr, yi = _twiddle(z[:DFT_N2], z[DFT_N2:], kf_ref[0, 0, j], kf_ref[0, 1, j])
        b = _dft_dot(wc_ref.at[j], jnp.concatenate([yr, yi], axis=0))
        o_ref[0, j, 0] = b[:DFT_N2]
        o_ref[0, j, 1] = b[DFT_N2:]


def _dft_out_kernel(b_ref, w_ref, gate_ref, zc_ref, skip_ref, o_ref, *, half):
    skip = skip_ref[...]
    for s in range(DFT_STEP):
        y = _dft_dot(w_ref, jnp.concatenate([b_ref[0, :, 0, s, :], b_ref[0, :, 1, s, :]], axis=0))
        o_ref[0, :, s, :] = gate_ref[0, :, s, :] * (y[:half] + skip * zc_ref[0, :, s, :])
        o_ref[1, :, s, :] = gate_ref[1, :, s, :] * (y[half:] + skip * zc_ref[1, :, s, :])


def _dft_constants(seq):
    n = 2 * seq
    n2 = DFT_N2
    n1 = n // n2
    half = n1 // 2
    k1 = np.arange(n1, dtype=np.float64)
    f1 = np.exp(-2j * np.pi * np.outer(k1, k1) / n1)

    def c2(m):
        return np.block([[m.real, -m.imag], [m.imag, m.real]])

    w_in = c2(f1[:, :half])
    w_fin = np.concatenate([f1.real, f1.imag], axis=0)
    w_out = c2(np.conj(f1)[:half, :]) / n
    as16 = lambda a: jnp.asarray(a, dtype=F32).astype(BF16)
    i1 = jnp.arange(n1, dtype=jnp.int32)[:, None, None]
    i2 = jnp.arange(n2, dtype=jnp.int32)
    phase = (i1 * i2[None, None, :] + n1 * (i2[None, :, None] * i2[None, None, :])) % n
    ang = phase.astype(F32) * (-2.0 * math.pi / n)
    mr, mi = jnp.cos(ang), jnp.sin(ang)
    w_mid = jnp.concatenate([jnp.concatenate([mr, -mi], axis=2), jnp.concatenate([mi, mr], axis=2)], axis=1)
    return dict(n1=n1, half=half, w_in=as16(w_in), w_fin=as16(w_fin), w_out=as16(w_out),
                w_mid=w_mid.astype(BF16), w_mid_c=jnp.swapaxes(w_mid, 1, 2).astype(BF16))


def _hyena_pos_features(seq):
    t = jnp.linspace(0.0, 1.0, seq, dtype=F32)[:, None]
    bands = (HY_EMB_DIM - 1) // 2
    w = 2.0 * math.pi * jnp.arange(seq, dtype=F32) / seq
    f = jnp.linspace(1e-4, bands - 1, bands, dtype=F32)
    ang = w[:, None] * f[None, :]
    z = jnp.concatenate([t, jnp.cos(ang), -jnp.sin(ang)], axis=-1)
    deltas = jnp.abs(jnp.linspace(math.log(HY_TARGET) / HY_SLOW_DECAY,
                                  math.log(HY_TARGET) / HY_FAST_DECAY, D_MODEL, dtype=F32))
    return z, deltas


def _hyena_filter_spectrum(seq, consts, f_w1, f_b1, f_w2, f_b2, f_w3, f_freq):
    d = D_MODEL
    n = 2 * seq
    n1, n2 = consts["n1"], DFT_N2
    z, deltas = _hyena_pos_features(seq)
    z2 = jnp.concatenate([z, z[:1], z[1:][::-1]], axis=0)
    kpad = HY_FILTER_WIDTH - HY_EMB_DIM
    z2 = jnp.pad(z2, ((0, 0), (0, kpad)))
    w1 = jnp.pad(f_w1, ((0, kpad), (0, 0)))
    bm = min(ROW_TILE, seq)
    nblk = n // bm
    k2, norm = pl.pallas_call(
        functools.partial(_hy_filter_kernel, bm=bm, seq=seq),
        grid=(nblk,),
        in_specs=[
            pl.BlockSpec((bm, HY_FILTER_WIDTH), lambda i: (i, 0)),
            _resident((HY_FILTER_WIDTH, HY_FILTER_WIDTH)),
            _resident((1, HY_FILTER_WIDTH)),
            _resident((HY_FILTER_WIDTH, HY_FILTER_WIDTH)),
            _resident((1, HY_FILTER_WIDTH)),
            pl.BlockSpec((1, HY_ORDER, HY_FILTER_WIDTH, d), lambda i: ((2 * i) // nblk, 0, 0, 0)),
            _resident((1, HY_FILTER_WIDTH)),
            _resident((1, d)),
        ],
        out_specs=[
            pl.BlockSpec((HY_ORDER, bm, d), lambda i: (0, i, 0)),
            pl.BlockSpec((HY_ORDER, d), lambda i: (0, 0)),
        ],
        out_shape=[
            jax.ShapeDtypeStruct((HY_ORDER, n, d), F32),
            jax.ShapeDtypeStruct((HY_ORDER, d), F32),
        ],
        compiler_params=_params("arbitrary"),
        name="hy_filter",
    )(z2, w1, f_b1.reshape(1, -1), f_w2, f_b2.reshape(1, -1),
      f_w3.reshape(HY_FILTER_WIDTH, HY_ORDER, 2, d).transpose(2, 1, 0, 3), f_freq.reshape(1, -1),
      deltas.reshape(1, d))

    st, ct = DFT_STEP, DFT_LANES
    a = pl.pallas_call(
        functools.partial(_dft_filter_in_kernel, n1=n1),
        grid=(HY_ORDER, n2 // st),
        in_specs=[
            pl.BlockSpec((1, n1, st, d), lambda o, j: (o, 0, j, 0)),
            _resident((2 * n1, n1)),
        ],
        out_specs=pl.BlockSpec((1, st, 2, n1, d), lambda o, j: (o, j, 0, 0, 0)),
        out_shape=jax.ShapeDtypeStruct((HY_ORDER, n2, 2, n1, d), F32),
        compiler_params=_params("parallel", "parallel"),
        name="hy_filter_dft_in",
    )(k2.reshape(HY_ORDER, n1, n2, d), consts["w_fin"])

    return pl.pallas_call(
        _dft_filter_mid_kernel,
        grid=(HY_ORDER, n1 // st, d // ct),
        in_specs=[
            pl.BlockSpec((1, n2, 2, st, ct), lambda o, k, c: (o, 0, 0, k, c)),
            pl.BlockSpec((st, 2 * n2, 2 * n2), lambda o, k, c: (k, 0, 0)),
            pl.BlockSpec((1, 1, ct), lambda o, k, c: (o, 0, c)),
        ],
        out_specs=pl.BlockSpec((1, 2, st, n2, ct), lambda o, k, c: (o, 0, k, 0, c)),
        out_shape=jax.ShapeDtypeStruct((HY_ORDER, 2, n1, n2, d), F32),
        compiler_params=_params("parallel", "parallel", "parallel"),
        name="hy_filter_dft_mid",
    )(a, consts["w_mid"], norm.reshape(HY_ORDER, 1, d))


def _long_conv_gate(zc, gate, kf, order, skip, consts):
    bsz, seq, d = zc.shape
    n1, half, n2 = consts["n1"], consts["half"], DFT_N2
    pairs = bsz // 2
    st, ct = DFT_STEP, DFT_LANES
    zv = zc.reshape(bsz, half, n2, d)
    a = pl.pallas_call(
        functools.partial(_dft_in_kernel, n1=n1),
        grid=(pairs, n2 // st),
        in_specs=[
            pl.BlockSpec((2, half, st, d), lambda p, j: (p, 0, j, 0)),
            _resident((2 * n1, n1)),
        ],
        out_specs=pl.BlockSpec((1, st, 2, n1, d), lambda p, j: (p, j, 0, 0, 0)),
        out_shape=jax.ShapeDtypeStruct((pairs, n2, 2, n1, d), F32),
        compiler_params=_params("parallel", "parallel"),
        name="hy_dft_in",
    )(zv, consts["w_in"])

    b = pl.pallas_call(
        _dft_mid_kernel,
        grid=(n1 // st, d // ct, pairs),
        in_specs=[
            pl.BlockSpec((1, n2, 2, st, ct), lambda k, c, p: (p, 0, 0, k, c)),
            pl.BlockSpec((1, 2, st, n2, ct), lambda k, c, p: (order, 0, k, 0, c)),
            pl.BlockSpec((st, 2 * n2, 2 * n2), lambda k, c, p: (k, 0, 0)),
            pl.BlockSpec((st, 2 * n2, 2 * n2), lambda k, c, p: (k, 0, 0)),
        ],
        out_specs=pl.BlockSpec((1, st, 2, n2, ct), lambda k, c, p: (p, k, 0, 0, c)),
        out_shape=jax.ShapeDtypeStruct((pairs, n1, 2, n2, d), F32),
        compiler_params=_params("parallel", "parallel", "arbitrary"),
        name="hy_dft_mid",
    )(a, kf, consts["w_mid"], consts["w_mid_c"])

    sig = pl.BlockSpec((2, half, st, ct), lambda p, j, c: (p, 0, j, c))
    out = pl.pallas_call(
        functools.partial(_dft_out_kernel, half=half),
        grid=(pairs, n2 // st, d // ct),
        in_specs=[
            pl.BlockSpec((1, n1, 2, st, ct), lambda p, j, c: (p, 0, 0, j, c)),
            _resident((n1, 2 * n1)),
            sig,
            sig,
            pl.BlockSpec((1, ct), lambda p, j, c: (0, c)),
        ],
        out_specs=sig,
        out_shape=jax.ShapeDtypeStruct((bsz, half, n2, d), F32),
        compiler_params=_params("parallel", "parallel", "parallel"),
        name="hy_dft_out",
    )(b, consts["w_out"], gate.reshape(bsz, half, n2, d), zv, skip.reshape(1, d))
    return out.reshape(bsz, seq, d)


def _hyena_layer(x, mod_l, norm_g, w_in, conv_w, conv_b, kf, skip, consts):
    bsz, seq, d = x.shape
    bm = min(ROW_TILE, seq)
    blk = pl.BlockSpec((1, bm, d), lambda b, i: (b, i, 0))
    shp = jax.ShapeDtypeStruct(x.shape, F32)
    v, x1, x2 = pl.pallas_call(
        functools.partial(_hy_in_kernel, bm=bm, seq=seq),
        grid=(bsz, seq // bm),
        in_specs=_halo_specs(bm, seq, d) + [
            pl.BlockSpec((1, 6, d), lambda b, i: (b, 0, 0)),
            _resident((1, d)),
            _resident((d, 3 * d)),
            _resident((3, 3 * d)),
            _resident((1, 3 * d)),
        ],
        out_specs=[blk, blk, blk],
        out_shape=[shp, shp, shp],
        scratch_shapes=[
            pltpu.VMEM((bm + 2 * HALO, d), BF16),
            pltpu.VMEM((bm + 2 * HALO, HY_CHUNK), F32),
        ],
        compiler_params=_params("parallel", "parallel"),
        name="hy_in",
    )(x, x, x, mod_l, norm_g.reshape(1, d), w_in.astype(BF16), conv_w, conv_b.reshape(1, 3 * d))
    zc = v
    for o, gate in enumerate((x1, x2)):
        zc = _long_conv_gate(zc, gate, kf, o, skip[o], consts)
    return zc


def _s5_prep_kernel(x_ref, mod_ref, g_ref, p_ref, o_ref):
    hs = []
    for b in range(x_ref.shape[0]):
        m = mod_ref[b]
        hs.append(_norm_mod(x_ref[b], g_ref[...], m[1:2], m[0:1]).astype(BF16))
    o_ref[...] = jnp.dot(p_ref[...], jnp.concatenate(hs, axis=0),
                         preferred_element_type=F32).astype(o_ref.dtype)


def _s5_scan_kernel(h_ref, bm_ref, cm_ref, lam_ref, *rest, rows, reverse, merge):
    if merge:
        prev_ref, pt_ref, o_ref, sa_ref, sb_ref, sc_ref, y_ref, carry_ref = rest
    else:
        o_ref, sa_ref, sb_ref, sc_ref, carry_ref = rest
        y_ref = o_ref
    n_slab = D_MODEL // S5_SLAB
    width = S5_SLAB // S5_GROUP * S5_STATE
    n_tiles = rows // SUBLANES
    half = SUBLANES // 2
    bufs = (sa_ref, sb_ref, sc_ref)

    @pl.when(pl.program_id(0) == 0)
    def _():
        carry_ref[...] = jnp.zeros(carry_ref.shape, F32)

    low = lax.broadcasted_iota(jnp.int32, (SUBLANES, width), 0) < half
    take_rolled = jnp.logical_not(low) if reverse else low
    low3 = lax.broadcasted_iota(jnp.int32, (n_tiles, SUBLANES, S5_SLAB), 1) < half
    gets_other = low3 if reverse else jnp.logical_not(low3)

    def project_in(k):
        u = h_ref[:, k * S5_SLAB:(k + 1) * S5_SLAB]
        u3 = u.astype(F32).reshape(n_tiles, SUBLANES, S5_SLAB)
        other = jnp.where(gets_other, pltpu.roll(u3, half, axis=1), 0.0).reshape(rows, S5_SLAB)
        lhs = jnp.concatenate([u, other.astype(BF16)], axis=1)
        bufs[k % 3][...] = jnp.dot(lhs, bm_ref[k], preferred_element_type=F32)

    def project_out(k):
        y_ref[:, k * S5_SLAB:(k + 1) * S5_SLAB] = jnp.dot(
            bufs[k % 3][...].astype(BF16), cm_ref[k], preferred_element_type=F32).astype(y_ref.dtype)

    def scan(k):
        s_ref = bufs[k % 3]
        lr = lam_ref[k, 0]
        li = lam_ref[k, 1]
        cr = carry_ref[k, 0]
        ci = carry_ref[k, 1]
        for i in (reversed(range(n_tiles)) if reverse else range(n_tiles)):
            r0 = i * SUBLANES
            vr = s_ref[r0:r0 + SUBLANES, 0:width]
            vi = s_ref[r0:r0 + SUBLANES, width:2 * width]
            pr = jnp.where(take_rolled, pltpu.roll(cr, half, axis=0), cr)
            pi = jnp.where(take_rolled, pltpu.roll(ci, half, axis=0), ci)
            cr = vr + (lr * pr - li * pi)
            ci = vi + (lr * pi + li * pr)
            s_ref[r0:r0 + SUBLANES, 0:width] = cr
            s_ref[r0:r0 + SUBLANES, width:2 * width] = ci
        carry_ref[k, 0] = cr
        carry_ref[k, 1] = ci

    project_in(0)
    for k in range(n_slab):
        if k + 1 < n_slab:
            project_in(k + 1)
        if k >= 1:
            project_out(k - 1)
        scan(k)
    project_out(n_slab - 1)
    if merge:
        both = (y_ref[...].astype(F32) + prev_ref[...].astype(F32)).astype(BF16)
        y = jnp.dot(pt_ref[...], both, preferred_element_type=F32)
        steps = rows // o_ref.shape[0]
        for b in range(o_ref.shape[0]):
            o_ref[b] = y[b * steps:(b + 1) * steps].astype(o_ref.dtype)


def _s5_glu_kernel(y_ref, x_ref, mod_ref, g_ref, dsk_ref, w_ref, o_ref):
    m = mod_ref[0]
    x = x_ref[0]
    h = _norm_mod(x, g_ref[...], m[1:2], m[0:1])
    y = jax.nn.gelu(y_ref[0].astype(F32) + dsk_ref[...] * h)
    g = jnp.dot(y.astype(BF16), w_ref[...], preferred_element_type=F32)
    d = x.shape[-1]
    o_ref[0] = x + m[2:3] * (g[:, :d] * jax.nn.sigmoid(g[:, d:]))


def _s5_tables(a_re, a_im, log_dt, b_re, b_im, c_re, c_im):
    n_slab = D_MODEL // S5_SLAB
    gps = S5_SLAB // S5_GROUP
    lam = lax.complex(jnp.minimum(a_re.astype(F32), -1e-4), a_im.astype(F32))
    dt = jnp.exp(log_dt.astype(F32))[..., None]
    lam_bar = jnp.exp(lam * dt)
    b_bar = ((lam_bar - 1.0) / lam)[..., None] * lax.complex(b_re.astype(F32), b_im.astype(F32))
    eye = jnp.eye(gps, dtype=F32)

    def b_block(part):
        m = jnp.einsum("dkgph,gj->dkghjp", part, eye)
        return m.reshape(2, n_slab, S5_SLAB, gps * S5_STATE)

    def b_rows(b_c):
        bb = b_c.reshape(2, n_slab, gps, S5_STATE, S5_GROUP)
        return jnp.concatenate([b_block(bb.real), b_block(bb.imag)], axis=-1)

    b_mat = jnp.concatenate([b_rows(b_bar), b_rows(b_bar * lam_bar[..., None])], axis=-2)
    cc = lax.complex(c_re.astype(F32), c_im.astype(F32)).reshape(2, n_slab, gps, S5_GROUP, S5_STATE)

    def c_block(part):
        m = jnp.einsum("dkghp,gj->dkgpjh", part, eye)
        return m.reshape(2, n_slab, gps * S5_STATE, S5_SLAB)

    c_mat = jnp.concatenate([c_block(cc.real), -c_block(cc.imag)], axis=-2)
    lam1 = lam_bar.reshape(2, n_slab, 1, gps * S5_STATE)
    lam2 = lam1 * lam1
    half = SUBLANES // 2

    def rows(lo, hi):
        return jnp.concatenate([jnp.broadcast_to(lo, lo.shape[:2] + (half,) + lo.shape[3:]),
                                jnp.broadcast_to(hi, hi.shape[:2] + (half,) + hi.shape[3:])], axis=2)

    carry = jnp.stack([rows(lam1, lam2)[0], rows(lam2, lam1)[1]])
    lam_t = jnp.stack([carry.real, carry.imag], axis=2)
    return b_mat, c_mat, lam_t


def _s5_layer(x, mod_l, norm_g, a_re, a_im, log_dt, b_re, b_im, c_re, c_im, d_skip, w_glu):
    bsz, seq, d = x.shape
    assert 2 * bsz == SUBLANES, bsz
    bm = min(ROW_TILE, seq)
    n_slab = d // S5_SLAB
    width = S5_SLAB // S5_GROUP * S5_STATE
    b_mat, c_mat, lam_t = _s5_tables(a_re, a_im, log_dt, b_re, b_im, c_re, c_im)
    steps = min(S5_STEPS, seq)
    rows = steps * bsz
    nc = seq // steps
    b_mat = b_mat.astype(BF16)
    c_mat = c_mat.astype(BF16)
    r = np.arange(rows)
    perm_np = np.zeros((rows, rows), np.float32)
    perm_np[r, (r % bsz) * steps + r // bsz] = 1.0
    perm = jnp.asarray(perm_np, dtype=BF16)

    h2 = pl.pallas_call(
        _s5_prep_kernel,
        grid=(nc,),
        in_specs=[
            pl.BlockSpec((bsz, steps, d), lambda i: (0, i, 0)),
            _resident((bsz, 6, d)),
            _resident((1, d)),
            _resident((rows, rows)),
        ],
        out_specs=pl.BlockSpec((rows, d), lambda i: (i, 0)),
        out_shape=jax.ShapeDtypeStruct((seq * bsz, d), BF16),
        compiler_params=_params("parallel"),
        name="s5_prep",
    )(x, mod_l, norm_g.reshape(1, d), perm)

    def scan(direction, prev=None):
        reverse = direction == 1
        merge = prev is not None
        chunk = (lambda c: nc - 1 - c) if reverse else (lambda c: c)
        rows_spec = pl.BlockSpec((rows, d), lambda c: (chunk(c), 0))
        specs = [
            rows_spec,
            _resident((n_slab, 2 * S5_SLAB, 2 * width)),
            _resident((n_slab, 2 * width, S5_SLAB)),
            _resident((n_slab, 2, SUBLANES, width)),
        ]
        args = (h2, b_mat[direction], c_mat[direction], lam_t[direction])
        scratch = [pltpu.VMEM((rows, 2 * width), F32)] * 3
        if merge:
            specs += [rows_spec, _resident((rows, rows))]
            args += (prev, perm.T)
            scratch = scratch + [pltpu.VMEM((rows, d), BF16)]
            out_spec = pl.BlockSpec((bsz, steps, d), lambda c: (0, chunk(c), 0))
            out_shape = jax.ShapeDtypeStruct((bsz, seq, d), BF16)
        else:
            out_spec = rows_spec
            out_shape = jax.ShapeDtypeStruct((seq * bsz, d), BF16)
        return pl.pallas_call(
            functools.partial(_s5_scan_kernel, rows=rows, reverse=reverse, merge=merge),
            grid=(nc,),
            in_specs=specs,
            out_specs=out_spec,
            out_shape=out_shape,
            scratch_shapes=scratch + [pltpu.VMEM((n_slab, 2, SUBLANES, width), F32)],
            compiler_params=_params("arbitrary"),
            name="s5_scan_bwd" if reverse else "s5_scan_fwd",
        )(*args)

    y = scan(1, prev=scan(0))
    return pl.pallas_call(
        _s5_glu_kernel,
        grid=(bsz, seq // bm),
        in_specs=[
            pl.BlockSpec((1, bm, d), lambda b, i: (b, i, 0)),
            pl.BlockSpec((1, bm, d), lambda b, i: (b, i, 0)),
            pl.BlockSpec((1, 6, d), lambda b, i: (b, 0, 0)),
            _resident((1, d)),
            _resident((1, d)),
            _resident((d, 2 * d)),
        ],
        out_specs=pl.BlockSpec((1, bm, d), lambda b, i: (b, i, 0)),
        out_shape=jax.ShapeDtypeStruct(x.shape, F32),
        compiler_params=_params("parallel", "parallel"),
        name="s5_glu",
    )(y, x, mod_l, norm_g.reshape(1, d), d_skip.reshape(1, d), w_glu.astype(BF16))


def kernel(x, c, ada_w, ada_b, norm1_g, norm2_g, final_g, attn_w_qkv, attn_w_o, attn_q_gain, attn_k_gain, hy_w_in, hy_conv_w, hy_conv_b, hy_f_w1, hy_f_b1, hy_f_w2, hy_f_b2, hy_f_w3, hy_f_freq, hy_skip, hy_w_out, s5_A_re, s5_A_im, s5_log_dt, s5_B_re, s5_B_im, s5_C_re, s5_C_im, s5_D, s5_w_glu, ffn_w_up, ffn_conv_w, ffn_conv_b, ffn_w_down):
    seq = x.shape[1]
    mod = _ada_mod(c, ada_w, ada_b)
    cos, sin = _rope_tables(seq)
    consts = _dft_constants(seq)
    for i in range(DEPTH):
        m, j = i % N_MIXERS, i // N_MIXERS
        mixer_out = None
        if m == 0:
            o = _attention_layer(x, mod[i], norm1_g[i], attn_w_qkv[j], attn_q_gain[j], attn_k_gain[j], cos, sin)
            mixer_out = (o, attn_w_o[j])
        elif m == 1:
            kf = _hyena_filter_spectrum(seq, consts, hy_f_w1[j], hy_f_b1[j], hy_f_w2[j], hy_f_b2[j],
                                        hy_f_w3[j], hy_f_freq[j])
            zc = _hyena_layer(x, mod[i], norm1_g[i], hy_w_in[j], hy_conv_w[j], hy_conv_b[j], kf,
                              hy_skip[j], consts)
            mixer_out = (zc, hy_w_out[j])
        else:
            x = _s5_layer(x, mod[i], norm1_g[i], s5_A_re[j], s5_A_im[j], s5_log_dt[j], s5_B_re[j],
                          s5_B_im[j], s5_C_re[j], s5_C_im[j], s5_D[j], s5_w_glu[j])
        x = _conv_ffn_layer(x, mod[i], norm2_g[i], ffn_w_up[i], ffn_conv_w[i], ffn_conv_b[i],
                            ffn_w_down[i], final_g, final_norm=(i == DEPTH - 1), mixer_out=mixer_out)
    return x
```
